```python
import math
import jax, jax.numpy as jnp
from jax import lax
import numpy as np

D_MODEL = 1024
BATCH = 8
SEQ = 2048
DEPTH = 1

GRID_W = 64
CTX_LEN = 256
LRU_WIDTH = 1024
LRU_HEADS = 8
LRU_HEAD_DIM = LRU_WIDTH // LRU_HEADS
LRU_C = 8.0
CONV_WIDTH = 4
CONV_PAD_LEFT = 2
CONV_PAD_RIGHT = CONV_WIDTH - 1 - CONV_PAD_LEFT
FOURIER_WIDTH = 512
FOURIER_GROUPS = 4
FOURIER_GROUP_DIM = FOURIER_WIDTH // FOURIER_GROUPS
N_BRANCHES = 2
IN_COLS = 2 * LRU_WIDTH + FOURIER_WIDTH + N_BRANCHES * D_MODEL
D_FF = 4 * D_MODEL
N_MOD = 6
EPS = 1e-6
POS_MAX_PERIOD = 10000.0

kernel_name = "hybrid_rglru_fnet_dit_block"


def rms_norm(x, g):
    xf = x.astype(jnp.float32)
    y = xf * lax.rsqrt(jnp.mean(xf * xf, axis=-1, keepdims=True) + EPS)
    return (y * g.astype(jnp.float32)).astype(x.dtype)


def modulate(h, shift, scale):
    return h * (1 + scale) + shift


def sincos_1d(pos, dim):
    half = dim // 2
    freqs = jnp.exp(-math.log(POS_MAX_PERIOD) * jnp.arange(half, dtype=jnp.float32) / half)
    ang = pos[:, None] * freqs[None, :]
    return jnp.concatenate([jnp.sin(ang), jnp.cos(ang)], axis=-1)


def grid_pos_embed(n_tokens, dtype):
    rows = n_tokens // GRID_W
    er = sincos_1d(jnp.arange(rows, dtype=jnp.float32), D_MODEL // 2)
    ec = sincos_1d(jnp.arange(GRID_W, dtype=jnp.float32), D_MODEL // 2)
    emb = jnp.concatenate([
        jnp.broadcast_to(er[:, None, :], (rows, GRID_W, D_MODEL // 2)),
        jnp.broadcast_to(ec[None, :, :], (rows, GRID_W, D_MODEL // 2)),
    ], axis=-1)
    return emb.reshape(rows * GRID_W, D_MODEL).astype(dtype)


def depthwise_conv(u, w, b):
    t = u.shape[1]
    up = jnp.pad(u, ((0, 0), (CONV_PAD_LEFT, CONV_PAD_RIGHT), (0, 0)))
    y = b
    for k in range(CONV_WIDTH):
        y = y + up[:, k:k + t] * w[k]
    return y


def rglru_coeffs(xc, w_a, b_a, w_x, b_x, lam):
    bsz, t, _ = xc.shape
    xh = xc.reshape(bsz, t, LRU_HEADS, LRU_HEAD_DIM)
    r = jax.nn.sigmoid(jnp.einsum('bthd,nhde->nbthe', xh, w_a).reshape(2, bsz, t, LRU_WIDTH)
                       + b_a[:, None, None, :])
    i = jax.nn.sigmoid(jnp.einsum('bthd,nhde->nbthe', xh, w_x).reshape(2, bsz, t, LRU_WIDTH)
                       + b_x[:, None, None, :])
    log_a = -LRU_C * r.astype(jnp.float32) * jax.nn.softplus(-lam.astype(jnp.float32))[:, None, None, :]
    a = jnp.exp(log_a)
    mult = jnp.sqrt(-jnp.expm1(2.0 * log_a))
    bterm = mult * (i * xc[None]).astype(jnp.float32)
    return a, bterm


def linear_scan(a, b, h0, reverse):
    def combine(l, r):
        a_l, b_l = l
        a_r, b_r = r
        return a_l * a_r, a_r * b_l + b_r
    a_cum, h = lax.associative_scan(combine, (a, b), reverse=reverse, axis=1)
    return h + a_cum * h0[:, None, :]


def bidir_rglru(xc_ctx, xc_lat, w_a, b_a, w_x, b_x, lam):
    a_c, b_c = rglru_coeffs(xc_ctx, w_a, b_a, w_x, b_x, lam)
    zeros = jnp.zeros((xc_ctx.shape[0], LRU_WIDTH), jnp.float32)
    h_cf = linear_scan(a_c[0], b_c[0], zeros, reverse=False)
    h_cb = linear_scan(a_c[1], b_c[1], zeros, reverse=True)
    a_l, b_l = rglru_coeffs(xc_lat, w_a, b_a, w_x, b_x, lam)
    h_lf = linear_scan(a_l[0], b_l[0], h_cf[:, -1], reverse=False)
    h_lb = linear_scan(a_l[1], b_l[1], h_cb[:, 0], reverse=True)
    return h_cf, h_cb, (h_lf + h_lb).astype(xc_lat.dtype)


def fourier_mix(u):
    bsz, t, _ = u.shape
    ug = u.astype(jnp.float32).reshape(bsz, t, FOURIER_GROUPS, FOURIER_GROUP_DIM)
    y = jnp.fft.fft2(ug, axes=(1, 3), norm="ortho").real
    return y.reshape(bsz, t, FOURIER_WIDTH).astype(u.dtype)


def split_in_cols(u):
    ux = u[..., :LRU_WIDTH]
    uy = u[..., LRU_WIDTH:2 * LRU_WIDTH]
    uf = u[..., 2 * LRU_WIDTH:2 * LRU_WIDTH + FOURIER_WIDTH]
    ug = u[..., 2 * LRU_WIDTH + FOURIER_WIDTH:]
    return ux, uy, uf, ug


def branch_merge(y_lru, uy, uf, ug, w_lru_out, w_f_out, w_out):
    y_a = (y_lru * jax.nn.gelu(uy)) @ w_lru_out
    y_b = fourier_mix(uf) @ w_f_out
    g = jax.nn.sigmoid(ug)
    merged = g[..., :D_MODEL] * y_a + g[..., D_MODEL:] * y_b
    return merged @ w_out


def sq_relu_mlp(h, w1, w2):
    return jnp.square(jax.nn.relu(h @ w1)) @ w2


def setup_inputs(seed: int = 0) -> dict:
    key = jax.random.key(seed)
    ks = jax.random.split(key, 24)
    f32 = jnp.float32
    nrm = lambda k, shape, fan_in: jax.random.normal(k, shape, f32) * (fan_in ** -0.5)
    u = jax.random.uniform(ks[12], (DEPTH, 2, LRU_WIDTH), f32, minval=0.9, maxval=0.999)
    s = u ** (1.0 / LRU_C)
    lam = jnp.log(s) - jnp.log1p(-s)
    return {
        "x": jax.random.normal(ks[0], (BATCH, SEQ, D_MODEL), f32),
        "c": jax.random.normal(ks[1], (BATCH, D_MODEL), f32),
        "ctx": jax.random.normal(ks[2], (BATCH, CTX_LEN, D_MODEL), f32),
        "c_ctx": jax.random.normal(ks[3], (D_MODEL,), f32),
        "w_mod": nrm(ks[4], (DEPTH, D_MODEL, N_MOD * D_MODEL), D_MODEL),
        "b_mod": 0.01 * jax.random.normal(ks[5], (DEPTH, N_MOD * D_MODEL), f32),
        "g_mix": 1.0 + 0.05 * jax.random.normal(ks[6], (DEPTH, D_MODEL), f32),
        "w_in": nrm(ks[7], (DEPTH, D_MODEL, IN_COLS), D_MODEL),
        "conv_w": nrm(ks[8], (DEPTH, CONV_WIDTH, LRU_WIDTH), CONV_WIDTH),
        "conv_b": 0.01 * jax.random.normal(ks[9], (DEPTH, LRU_WIDTH), f32),
        "w_a": nrm(ks[10], (DEPTH, 2, LRU_HEADS, LRU_HEAD_DIM, LRU_HEAD_DIM), LRU_HEAD_DIM),
        "b_a": 0.01 * jax.random.normal(ks[11], (DEPTH, 2, LRU_WIDTH), f32),
        "w_x": nrm(ks[13], (DEPTH, 2, LRU_HEADS, LRU_HEAD_DIM, LRU_HEAD_DIM), LRU_HEAD_DIM),
        "b_x": 0.01 * jax.random.normal(ks[14], (DEPTH, 2, LRU_WIDTH), f32),
        "lam": lam,
        "w_lru_out": nrm(ks[15], (DEPTH, LRU_WIDTH, D_MODEL), LRU_WIDTH),
        "w_f_out": nrm(ks[16], (DEPTH, FOURIER_WIDTH, D_MODEL), FOURIER_WIDTH),
        "w_out": nrm(ks[17], (DEPTH, D_MODEL, D_MODEL), D_MODEL),
        "g_mlp": 1.0 + 0.05 * jax.random.normal(ks[18], (DEPTH, D_MODEL), f32),
        "w1": nrm(ks[19], (DEPTH, D_MODEL, D_FF), D_MODEL),
        "w2": nrm(ks[20], (DEPTH, D_FF, D_MODEL), D_FF),
        "g_final": 1.0 + 0.05 * jax.random.normal(ks[21], (D_MODEL,), f32),
    }


def reference(x, c, ctx, c_ctx, w_mod, b_mod, g_mix, w_in, conv_w, conv_b, w_a, b_a, w_x, b_x,
              lam, w_lru_out, w_f_out, w_out, g_mlp, w1, w2, g_final):
    n_lat = x.shape[1]
    x = x + grid_pos_embed(n_lat, x.dtype)[None]
    for i in range(DEPTH):
        last = i == DEPTH - 1
        mod_l = [m[:, None, :] for m in jnp.split(jax.nn.silu(c) @ w_mod[i] + b_mod[i], N_MOD, axis=-1)]
        mod_c = jnp.split(jax.nn.silu(c_ctx) @ w_mod[i] + b_mod[i], N_MOD, axis=-1)
        sh1, sc1, gt1, sh2, sc2, gt2 = mod_l
        sh1c, sc1c, gt1c, sh2c, sc2c, gt2c = mod_c

        h_lat = modulate(rms_norm(x, g_mix[i]), sh1, sc1)
        h_ctx = modulate(rms_norm(ctx, g_mix[i]), sh1c, sc1c)
        ux_l, uy_l, uf_l, ug_l = split_in_cols(h_lat @ w_in[i])
        if last:
            ux_c = h_ctx @ w_in[i][:, :LRU_WIDTH]
        else:
            ux_c, uy_c, uf_c, ug_c = split_in_cols(h_ctx @ w_in[i])
        xc_l = depthwise_conv(ux_l, conv_w[i], conv_b[i])
        xc_c = depthwise_conv(ux_c, conv_w[i], conv_b[i])
        h_cf, h_cb, y_lru_l = bidir_rglru(xc_c, xc_l, w_a[i], b_a[i], w_x[i], b_x[i], lam[i])
        x = x + gt1 * branch_merge(y_lru_l, uy_l, uf_l, ug_l, w_lru_out[i], w_f_out[i], w_out[i])

        x = x + gt2 * sq_relu_mlp(modulate(rms_norm(x, g_mlp[i]), sh2, sc2), w1[i], w2[i])

        if not last:
            y_lru_c = (h_cf + h_cb).astype(ctx.dtype)
            ctx = ctx + gt1c * branch_merge(y_lru_c, uy_c, uf_c, ug_c, w_lru_out[i], w_f_out[i], w_out[i])
            ctx = ctx + gt2c * sq_relu_mlp(modulate(rms_norm(ctx, g_mlp[i]), sh2c, sc2c), w1[i], w2[i])
    return rms_norm(x, g_final)
```

```python
import functools
import math

import numpy as np
import jax
import jax.numpy as jnp
from jax import lax
from jax.experimental import pallas as pl
from jax.experimental.pallas import tpu as pltpu

F32 = jnp.float32
BF16 = jnp.bfloat16

D_MODEL = 1024
BATCH = 8
SEQ = 2048
CTX_LEN = 256
GRID_W = 64
LRU_WIDTH = 1024
LRU_HEADS = 8
LRU_HEAD_DIM = LRU_WIDTH // LRU_HEADS
LRU_C = 8.0
CONV_WIDTH = 4
FOURIER_WIDTH = 512
FOURIER_GROUPS = 4
FOURIER_GROUP_DIM = FOURIER_WIDTH // FOURIER_GROUPS
IN_COLS = 2 * LRU_WIDTH + FOURIER_WIDTH + 2 * D_MODEL
D_FF = 4 * D_MODEL
N_MOD = 6
EPS = 1e-6
POS_MAX_PERIOD = 10000.0

LANES = 128
SUBLANES = 8
assert BATCH == SUBLANES

TT = 64
ROWS = TT * BATCH
PITCH = TT + SUBLANES
N_CTX_CHUNKS = CTX_LEN // TT
N_LAT_CHUNKS = SEQ // TT
N_CHUNKS = N_CTX_CHUNKS + N_LAT_CHUNKS
HALO = 16
HALO_PER_CHUNK = ROWS // HALO
TM = 512
NCHUNK = 512
MOD_TN = 512
VMEM_LIMIT = 48 * 1024 * 1024


def _sigmoid(x):
    return 0.5 * jnp.tanh(0.5 * x) + 0.5


def _gelu_tanh(x):
    return 0.5 * x * (1.0 + jnp.tanh(math.sqrt(2.0 / math.pi) * (x + 0.044715 * (x * x * x))))


def _dot(a, b):
    return jnp.dot(a, b, preferred_element_type=F32)


def _rms(x, g):
    ms = jnp.mean(x * x, axis=-1, keepdims=True)
    return x * lax.rsqrt(ms + EPS) * g


@functools.cache
def _constants():
    half = D_MODEL // 4
    freqs = np.exp(-math.log(POS_MAX_PERIOD) * np.arange(half, dtype=np.float32) / half).astype(np.float32)

    def sincos(n):
        ang = np.arange(n, dtype=np.float32)[:, None] * freqs[None, :]
        return np.concatenate([np.sin(ang), np.cos(ang)], axis=-1).astype(np.float32)

    rows = SEQ // GRID_W
    er, ec = sincos(rows), sincos(GRID_W)
    pos = np.concatenate([
        np.broadcast_to(er[:, None, :], (rows, GRID_W, D_MODEL // 2)),
        np.broadcast_to(ec[None, :, :], (rows, GRID_W, D_MODEL // 2)),
    ], axis=-1).reshape(SEQ, D_MODEL).astype(np.float32)

    def dft(n):
        k = np.arange(n, dtype=np.int64)
        ang = 2.0 * np.pi * ((k[:, None] * k[None, :]) % n).astype(np.float64) / n
        return np.cos(ang) / math.sqrt(n), np.sin(ang) / math.sqrt(n)

    cc, sc = dft(FOURIER_GROUP_DIM)
    chan = np.concatenate([cc, sc], axis=1).astype(np.float32)
    ct, st = dft(SEQ)
    time = np.concatenate([ct, -st], axis=1).astype(np.float32)
    return pos, chan, time


def _mod_kernel(c_ref, w_ref, b_ref, o_ref):
    c = c_ref[...]
    a = c * _sigmoid(c)
    w = w_ref[...]
    a_hi = a.astype(BF16)
    a_lo = (a - a_hi.astype(F32)).astype(BF16)
    w_hi = w.astype(BF16)
    w_lo = (w - w_hi.astype(F32)).astype(BF16)
    o_ref[...] = _dot(a_hi, w_hi) + _dot(a_hi, w_lo) + _dot(a_lo, w_hi) + b_ref[...]


def _mod_call(cc, w_mod, b_mod):
    n = w_mod.shape[1]
    return pl.pallas_call(
        _mod_kernel,
        grid=(n // MOD_TN,),
        in_specs=[
            pl.BlockSpec((2 * SUBLANES, D_MODEL), lambda i: (0, 0)),
            pl.BlockSpec((D_MODEL, MOD_TN), lambda i: (0, i)),
            pl.BlockSpec((1, MOD_TN), lambda i: (0, i)),
        ],
        out_specs=pl.BlockSpec((2 * SUBLANES, MOD_TN), lambda i: (0, i)),
        out_shape=jax.ShapeDtypeStruct((2 * SUBLANES, n), F32),
        name="mod",
        compiler_params=pltpu.CompilerParams(dimension_semantics=("arbitrary",)),
    )(cc, w_mod, b_mod)


def _inproj_kernel(*refs, with_pos, full):
    refs = list(refs)
    x_ref = refs.pop(0)
    pos_ref = refs.pop(0) if with_pos else None
    sh_ref, sc_ref, g_ref, w_ref = refs[:4]
    refs = refs[4:]
    if full:
        refs.pop(0)
        ux_ref, uy_ref, uf_ref, ug_ref, s_ref, tm_ref = refs
    else:
        ux_ref, s_ref, tm_ref = refs

    x = x_ref[...]
    if with_pos:
        x = x + pos_ref[...][None]
    h = _rms(x, g_ref[...]) * (1.0 + sc_ref[...][:, None, :]) + sh_ref[...][:, None, :]
    hb = h.reshape(ROWS, D_MODEL).astype(BF16)

    n_cols = w_ref.shape[1]
    for n0 in range(0, n_cols, NCHUNK):
        r = _dot(hb, w_ref[:, n0:n0 + NCHUNK])
        if n0 < LRU_WIDTH:
            for b in range(BATCH):
                for jj in range(NCHUNK // LANES):
                    s_ref[n0 // LANES + jj, b * PITCH:b * PITCH + TT, :] = (
                        r[b * TT:(b + 1) * TT, jj * LANES:(jj + 1) * LANES])
        elif n0 < 2 * LRU_WIDTH:
            c0 = n0 - LRU_WIDTH
            uy_ref[:, :, c0:c0 + NCHUNK] = r.reshape(BATCH, TT, NCHUNK).astype(BF16)
        elif n0 < 2 * LRU_WIDTH + FOURIER_WIDTH:
            for b in range(BATCH):
                uf_ref[:, b * FOURIER_WIDTH:(b + 1) * FOURIER_WIDTH] = r[b * TT:(b + 1) * TT].astype(BF16)
        else:
            c0 = n0 - 2 * LRU_WIDTH - FOURIER_WIDTH
            ug_ref[:, :, c0:c0 + NCHUNK] = r.reshape(BATCH, TT, NCHUNK).astype(BF16)

    for t in range(TT):
        for j in range(LRU_WIDTH // LANES):
            tm_ref[t * BATCH:(t + 1) * BATCH, j * LANES:(j + 1) * LANES] = (
                s_ref[j, pl.ds(t, BATCH, stride=PITCH), :])
    ux_ref[...] = tm_ref[...].astype(BF16)


def _inproj_scratch():
    return [
        pltpu.VMEM((LRU_WIDTH // LANES, BATCH * PITCH, LANES), F32),
        pltpu.VMEM((ROWS, LRU_WIDTH), F32),
    ]


def _inproj_ctx_call(ctx, sh, sc, g, w):
    return pl.pallas_call(
        functools.partial(_inproj_kernel, with_pos=False, full=False),
        grid=(N_CTX_CHUNKS,),
        in_specs=[
            pl.BlockSpec((BATCH, TT, D_MODEL), lambda i: (0, i, 0)),
            pl.BlockSpec((BATCH, D_MODEL), lambda i: (0, 0)),
            pl.BlockSpec((BATCH, D_MODEL), lambda i: (0, 0)),
            pl.BlockSpec((1, D_MODEL), lambda i: (0, 0)),
            pl.BlockSpec((D_MODEL, LRU_WIDTH), lambda i: (0, 0)),
        ],
        out_specs=pl.BlockSpec((ROWS, LRU_WIDTH), lambda i: (i, 0)),
        out_shape=jax.ShapeDtypeStruct((N_CHUNKS * ROWS, LRU_WIDTH), BF16),
        scratch_shapes=_inproj_scratch(),
        name="inproj_ctx",
        compiler_params=pltpu.CompilerParams(
            dimension_semantics=("arbitrary",), vmem_limit_bytes=VMEM_LIMIT),
    )(ctx, sh, sc, g, w)


def _inproj_lat_call(x, pos, sh, sc, g, w, u_tm):
    return pl.pallas_call(
        functools.partial(_inproj_kernel, with_pos=True, full=True),
        grid=(N_LAT_CHUNKS,),
        in_specs=[
            pl.BlockSpec((BATCH, TT, D_MODEL), lambda i: (0, i, 0)),
            pl.BlockSpec((TT, D_MODEL), lambda i: (i, 0)),
            pl.BlockSpec((BATCH, D_MODEL), lambda i: (0, 0)),
            pl.BlockSpec((BATCH, D_MODEL), lambda i: (0, 0)),
            pl.BlockSpec((1, D_MODEL), lambda i: (0, 0)),
            pl.BlockSpec((D_MODEL, IN_COLS), lambda i: (0, 0)),
            pl.BlockSpec(memory_space=pl.ANY),
        ],
        out_specs=[
            pl.BlockSpec((ROWS, LRU_WIDTH), lambda i: (i + N_CTX_CHUNKS, 0)),
            pl.BlockSpec((BATCH, TT, LRU_WIDTH), lambda i: (0, i, 0)),
            pl.BlockSpec((TT, BATCH * FOURIER_WIDTH), lambda i: (i, 0)),
            pl.BlockSpec((BATCH, TT, 2 * D_MODEL), lambda i: (0, i, 0)),
        ],
        out_shape=[
            jax.ShapeDtypeStruct((N_CHUNKS * ROWS, LRU_WIDTH), BF16),
            jax.ShapeDtypeStruct((BATCH, SEQ, LRU_WIDTH), BF16),
            jax.ShapeDtypeStruct((SEQ, BATCH * FOURIER_WIDTH), BF16),
            jax.ShapeDtypeStruct((BATCH, SEQ, 2 * D_MODEL), BF16),
        ],
        scratch_shapes=_inproj_scratch(),
        input_output_aliases={6: 0},
        name="inproj_lat",
        compiler_params=pltpu.CompilerParams(
            dimension_semantics=("arbitrary",), vmem_limit_bytes=VMEM_LIMIT),
    )(x, pos, sh, sc, g, w, u_tm)


def _fwd_chunk(j):
    return j


def _bwd_chunk(j):
    return jnp.where(j < N_CTX_CHUNKS, N_CTX_CHUNKS - 1 - j, N_CHUNKS + N_CTX_CHUNKS - 1 - j)


def _scan_kernel(fm_ref, fp_ref, fn_ref, bm_ref, bp_ref, bn_ref, cw_ref, cb_ref, wg_ref,
                 ba_ref, bx_ref, lam_ref, hf_ref, hb_ref,
                 xs_ref, a_ref, b_ref, h_ref, st_ref):
    j = pl.program_id(0)

    @pl.when(j == 0)
    def _():
        h_ref[...] = jnp.zeros_like(h_ref)

    lam = lam_ref[...]
    sp = jnp.maximum(-lam, 0.0) + jnp.log1p(jnp.exp(-jnp.abs(lam)))

    chunks = (_fwd_chunk(j), _bwd_chunk(j))
    mains = ((fm_ref, fp_ref, fn_ref), (bm_ref, bp_ref, bn_ref))
    for d in range(2):
        m_ref, p_ref, n_ref = mains[d]
        c = chunks[d]
        first = jnp.logical_or(c == 0, c == N_CTX_CHUNKS)
        last = jnp.logical_or(c == N_CTX_CHUNKS - 1, c == N_CHUNKS - 1)
        xs_ref[0:HALO, :] = p_ref[...].astype(F32) * jnp.where(first, 0.0, 1.0)
        xs_ref[HALO:HALO + ROWS, :] = m_ref[...].astype(F32)
        xs_ref[HALO + ROWS:HALO + ROWS + BATCH, :] = (
            n_ref[...].astype(F32)[0:BATCH] * jnp.where(last, 0.0, 1.0))
        for hd in range(LRU_HEADS):
            sl = slice(hd * LRU_HEAD_DIM, (hd + 1) * LRU_HEAD_DIM)
            xc = cb_ref[:, sl]
            for k in range(CONV_WIDTH):
                xc = xc + cw_ref[k:k + 1, sl] * xs_ref[k * BATCH:k * BATCH + ROWS, sl]
            g = _dot(xc.astype(BF16), wg_ref[d, hd])
            r = _sigmoid(g[:, :LRU_HEAD_DIM] + ba_ref[d:d + 1, sl])
            i = _sigmoid(g[:, LRU_HEAD_DIM:] + bx_ref[d:d + 1, sl])
            a = jnp.exp((-LRU_C * sp[d:d + 1, sl]) * r)
            a_ref[d, :, sl] = a
            b_ref[d, :, sl] = jnp.sqrt(1.0 - a * a) * (i * xc)

    def step(t, carry):
        hf, hb = carry
        tb = TT - 1 - t
        rf = pl.multiple_of(t * BATCH, BATCH)
        rb = pl.multiple_of(tb * BATCH, BATCH)
        hf = a_ref[0, pl.ds(rf, BATCH), :] * hf + b_ref[0, pl.ds(rf, BATCH), :]
        hb = a_ref[1, pl.ds(rb, BATCH), :] * hb + b_ref[1, pl.ds(rb, BATCH), :]
        for jj in range(LRU_WIDTH // LANES):
            st_ref[0, jj, pl.ds(t, BATCH, stride=PITCH), :] = hf[:, jj * LANES:(jj + 1) * LANES]
            st_ref[1, jj, pl.ds(tb, BATCH, stride=PITCH), :] = hb[:, jj * LANES:(jj + 1) * LANES]
        return hf, hb

    hf, hb = lax.fori_loop(0, TT, step, (h_ref[0], h_ref[1]))
    h_ref[0] = hf
    h_ref[1] = hb

    for d, o_ref in enumerate((hf_ref, hb_ref)):
        for b in range(BATCH):
            for jj in range(LRU_WIDTH // LANES):
                o_ref[b, :, jj * LANES:(jj + 1) * LANES] = (
                    st_ref[d, jj, b * PITCH:b * PITCH + TT, :].astype(BF16))


def _scan_call(u_tm, conv_w, conv_b, wg, b_a, b_x, lam):
    n_halo = N_CHUNKS * HALO_PER_CHUNK

    def main(chunk):
        return pl.BlockSpec((ROWS, LRU_WIDTH), lambda j: (chunk(j), 0))

    def prev(chunk):
        return pl.BlockSpec((HALO, LRU_WIDTH), lambda j: (jnp.maximum(chunk(j) * HALO_PER_CHUNK - 1, 0), 0))

    def nxt(chunk):
        return pl.BlockSpec(
            (HALO, LRU_WIDTH), lambda j: (jnp.minimum((chunk(j) + 1) * HALO_PER_CHUNK, n_halo - 1), 0))

    def full(shape):
        return pl.BlockSpec(shape, lambda j: (0,) * len(shape))

    def out_f(j):
        return (0, jnp.maximum(j - N_CTX_CHUNKS, 0), 0)

    def out_b(j):
        return (0, jnp.where(j < N_CTX_CHUNKS, N_LAT_CHUNKS - 1, N_CHUNKS - 1 - j), 0)

    hshape = jax.ShapeDtypeStruct((BATCH, SEQ, LRU_WIDTH), BF16)
    return pl.pallas_call(
        _scan_kernel,
        grid=(N_CHUNKS,),
        in_specs=[
            main(_fwd_chunk), prev(_fwd_chunk), nxt(_fwd_chunk),
            main(_bwd_chunk), prev(_bwd_chunk), nxt(_bwd_chunk),
            full((CONV_WIDTH, LRU_WIDTH)), full((1, LRU_WIDTH)),
            full((2, LRU_HEADS, LRU_HEAD_DIM, 2 * LRU_HEAD_DIM)),
            full((2, LRU_WIDTH)), full((2, LRU_WIDTH)), full((2, LRU_WIDTH)),
        ],
        out_specs=[
            pl.BlockSpec((BATCH, TT, LRU_WIDTH), out_f),
            pl.BlockSpec((BATCH, TT, LRU_WIDTH), out_b),
        ],
        out_shape=[hshape, hshape],
        scratch_shapes=[
            pltpu.VMEM((HALO + ROWS + BATCH, LRU_WIDTH), F32),
            pltpu.VMEM((2, ROWS, LRU_WIDTH), F32),
            pltpu.VMEM((2, ROWS, LRU_WIDTH), F32),
            pltpu.VMEM((2, BATCH, LRU_WIDTH), F32),
            pltpu.VMEM((2, LRU_WIDTH // LANES, BATCH * PITCH, LANES), F32),
        ],
        name="scan",
        compiler_params=pltpu.CompilerParams(
            dimension_semantics=("arbitrary",), vmem_limit_bytes=VMEM_LIMIT),
    )(u_tm, u_tm, u_tm, u_tm, u_tm, u_tm, conv_w, conv_b, wg, b_a, b_x, lam)


def _fourier_kernel(x_ref, chan_ref, time_ref, o_ref, z_ref):
    for g in range(FOURIER_GROUPS):
        sl = slice(g * FOURIER_GROUP_DIM, (g + 1) * FOURIER_GROUP_DIM)
        r = _dot(x_ref[:, sl], chan_ref[...])
        z_ref[0:SEQ, sl] = r[:, :FOURIER_GROUP_DIM].astype(BF16)
        z_ref[SEQ:2 * SEQ, sl] = r[:, FOURIER_GROUP_DIM:].astype(BF16)
    for m0 in range(0, SEQ, TM):
        o_ref[m0:m0 + TM, :] = _dot(time_ref[m0:m0 + TM, :], z_ref[...]).astype(BF16)


def _fourier_call(uf, chan, time):
    return pl.pallas_call(
        _fourier_kernel,
        grid=(BATCH,),
        in_specs=[
            pl.BlockSpec((SEQ, FOURIER_WIDTH), lambda n: (0, n)),
            pl.BlockSpec((FOURIER_GROUP_DIM, 2 * FOURIER_GROUP_DIM), lambda n: (0, 0)),
            pl.BlockSpec((SEQ, 2 * SEQ), lambda n: (0, 0)),
        ],
        out_specs=pl.BlockSpec((SEQ, FOURIER_WIDTH), lambda n: (0, n)),
        out_shape=jax.ShapeDtypeStruct((SEQ, BATCH * FOURIER_WIDTH), BF16),
        scratch_shapes=[pltpu.VMEM((2 * SEQ, FOURIER_WIDTH), BF16)],
        name="fourier",
        compiler_params=pltpu.CompilerParams(
            dimension_semantics=("arbitrary",), vmem_limit_bytes=VMEM_LIMIT),
    )(uf, chan, time)


def _merge_kernel(hf_ref, hb_ref, uy_ref, ug_ref, yf_ref, x_ref, pos_ref, gt_ref,
                  wl_ref, wf_ref, wo_ref, o_ref):
    y_lru = hf_ref[...].astype(F32) + hb_ref[...].astype(F32)
    z = (y_lru * _gelu_tanh(uy_ref[...].astype(F32))).astype(BF16)
    y_a = _dot(z, wl_ref[...])
    y_b = _dot(yf_ref[...], wf_ref[...])
    g_a = _sigmoid(ug_ref[:, :D_MODEL].astype(F32))
    g_b = _sigmoid(ug_ref[:, D_MODEL:].astype(F32))
    merged = (g_a * y_a + g_b * y_b).astype(BF16)
    o_ref[...] = x_ref[...] + pos_ref[...] + gt_ref[...] * _dot(merged, wo_ref[...])


def _tok(width):
    return pl.BlockSpec((None, TM, width), lambda b, t: (b, t, 0))


def _per_batch():
    return pl.BlockSpec((None, 1, D_MODEL), lambda b, t: (b, 0, 0))


def _resident(shape):
    return pl.BlockSpec(shape, lambda b, t: (0,) * len(shape))


def _merge_call(hf, hb, uy, ug, yf, x, pos, gt1, wl, wf, wo):
    return pl.pallas_call(
        _merge_kernel,
        grid=(BATCH, SEQ // TM),
        in_specs=[
            _tok(LRU_WIDTH), _tok(LRU_WIDTH), _tok(LRU_WIDTH), _tok(2 * D_MODEL),
            pl.BlockSpec((TM, FOURIER_WIDTH), lambda b, t: (t, b)),
            _tok(D_MODEL),
            pl.BlockSpec((TM, D_MODEL), lambda b, t: (t, 0)),
            _per_batch(),
            _resident((LRU_WIDTH, D_MODEL)), _resident((FOURIER_WIDTH, D_MODEL)),
            _resident((D_MODEL, D_MODEL)),
        ],
        out_specs=_tok(D_MODEL),
        out_shape=jax.ShapeDtypeStruct((BATCH, SEQ, D_MODEL), F32),
        name="merge",
        compiler_params=pltpu.CompilerParams(
            dimension_semantics=("arbitrary", "arbitrary"), vmem_limit_bytes=VMEM_LIMIT),
    )(hf, hb, uy, ug, yf, x, pos, gt1, wl, wf, wo)


def _mlp_kernel(x_ref, sh_ref, sc_ref, gt_ref, g_ref, gf_ref, w1_ref, w2_ref, o_ref):
    x = x_ref[...]
    h = (_rms(x, g_ref[...]) * (1.0 + sc_ref[...]) + sh_ref[...]).astype(BF16)
    acc = jnp.zeros((TM, D_MODEL), F32)
    for k0 in range(0, D_FF, D_MODEL):
        a = jnp.maximum(_dot(h, w1_ref[:, k0:k0 + D_MODEL]), 0.0)
        acc = acc + _dot((a * a).astype(BF16), w2_ref[k0:k0 + D_MODEL, :])
    o_ref[...] = _rms(x + gt_ref[...] * acc, gf_ref[...])


def _mlp_call(x, sh, sc, gt, g, gf, w1, w2):
    return pl.pallas_call(
        _mlp_kernel,
        grid=(BATCH, SEQ // TM),
        in_specs=[
            _tok(D_MODEL), _per_batch(), _per_batch(), _per_batch(),
            _resident((1, D_MODEL)), _resident((1, D_MODEL)),
            _resident((D_MODEL, D_FF)), _resident((D_FF, D_MODEL)),
        ],
        out_specs=_tok(D_MODEL),
        out_shape=jax.ShapeDtypeStruct((BATCH, SEQ, D_MODEL), F32),
        name="mlp",
        compiler_params=pltpu.CompilerParams(
            dimension_semantics=("arbitrary", "arbitrary"), vmem_limit_bytes=VMEM_LIMIT),
    )(x, sh, sc, gt, g, gf, w1, w2)


def kernel(x, c, ctx, c_ctx, w_mod, b_mod, g_mix, w_in, conv_w, conv_b, w_a, b_a, w_x, b_x,
           lam, w_lru_out, w_f_out, w_out, g_mlp, w1, w2, g_final):
    pos_np, chan_np, time_np = _constants()
    pos = jnp.asarray(pos_np)
    chan, time = jnp.asarray(chan_np).astype(BF16), jnp.asarray(time_np).astype(BF16)

    cc = jnp.concatenate(
        [c, c_ctx[None], jnp.zeros((2 * SUBLANES - BATCH - 1, D_MODEL), F32)], axis=0)
    mod = _mod_call(cc, w_mod[0], b_mod[0][None])
    sh1, sc1, gt1, sh2, sc2, gt2 = [
        mod[:BATCH, k * D_MODEL:(k + 1) * D_MODEL] for k in range(N_MOD)]
    sh1c = jnp.broadcast_to(mod[BATCH:BATCH + 1, 0:D_MODEL], (BATCH, D_MODEL))
    sc1c = jnp.broadcast_to(mod[BATCH:BATCH + 1, D_MODEL:2 * D_MODEL], (BATCH, D_MODEL))

    w_in_b = w_in[0].astype(BF16)
    g_mix2 = g_mix[0][None]
    u_tm = _inproj_ctx_call(ctx, sh1c, sc1c, g_mix2, w_in_b[:, :LRU_WIDTH])
    u_tm, uy, uf, ug = _inproj_lat_call(x, pos, sh1, sc1, g_mix2, w_in_b, u_tm)

    wg = jnp.concatenate([w_a[0], w_x[0]], axis=-1).astype(BF16)
    hf, hb = _scan_call(u_tm, conv_w[0], conv_b[0][None], wg, b_a[0], b_x[0], lam[0])

    yf = _fourier_call(uf, chan, time)

    x2 = _merge_call(hf, hb, uy, ug, yf, x, pos, gt1[:, None, :],
                     w_lru_out[0].astype(BF16), w_f_out[0].astype(BF16), w_out[0].astype(BF16))
    return _mlp_call(x2, sh2[:, None, :], sc2[:, None, :], gt2[:, None, :],
                     g_mlp[0][None], g_final[None], w1[0].astype(BF16), w2[0].astype(BF16))
```

```python
import functools
import math

import numpy as np
import jax
import jax.numpy as jnp
from jax import lax
from jax.experimental import pallas as pl
from jax.experimental.pallas import tpu as pltpu

F32 = jnp.float32
BF16 = jnp.bfloat16

D_MODEL = 1024
BATCH = 8
SEQ = 2048
CTX_LEN = 256
GRID_W = 64
LRU_WIDTH = 1024
LRU_HEADS = 8
LRU_HEAD_DIM = LRU_WIDTH // LRU_HEADS
LRU_C = 8.0
CONV_WIDTH = 4
CONV_PAD_LEFT = 2
FOURIER_WIDTH = 512
FOURIER_GROUPS = 4
FOURIER_GROUP_DIM = FOURIER_WIDTH // FOURIER_GROUPS
IN_COLS = 2 * LRU_WIDTH + FOURIER_WIDTH + 2 * D_MODEL
D_FF = 4 * D_MODEL
N_MOD = 6
EPS = 1e-6
POS_MAX_PERIOD = 10000.0

LANES = 128
SUBLANES = 8
BF16_ROWS = 16
assert BATCH == SUBLANES

TT = 64
ROWS = TT * BATCH
PITCH = TT + SUBLANES
HALO_T = SUBLANES
LEAD = CONV_PAD_LEFT * BATCH
TRAIL = (CONV_WIDTH - 1 - CONV_PAD_LEFT) * BATCH
XROWS = LEAD + ROWS + TRAIL
XROWS_PAD = -(-XROWS // BF16_ROWS) * BF16_ROWS
N_CTX_CHUNKS = CTX_LEN // TT
N_LAT_CHUNKS = SEQ // TT
NCHUNK = 512
SCAN_LANES = 256
TM = 512
MOD_TN = 512
VMEM_LIMIT = 58 * 1024 * 1024
LOG2E = math.log2(math.e)
TINY = 1e-30


def _gelu_tanh(x):
    return 0.5 * x * (1.0 + jnp.tanh(math.sqrt(2.0 / math.pi) * (x + 0.044715 * (x * x * x))))


def _sigmoid(x):
    return 0.5 * jnp.tanh(0.5 * x) + 0.5


def _dot(a, b):
    return jnp.dot(a, b, preferred_element_type=F32)


def _rms(x, g):
    ms = jnp.mean(x * x, axis=-1, keepdims=True)
    return x * lax.rsqrt(ms + EPS) * g


@functools.cache
def _constants():
    half = D_MODEL // 4
    freqs = np.exp(-math.log(POS_MAX_PERIOD) * np.arange(half, dtype=np.float32) / half).astype(np.float32)

    def sincos(n):
        ang = np.arange(n, dtype=np.float32)[:, None] * freqs[None, :]
        return np.concatenate([np.sin(ang), np.cos(ang)], axis=-1).astype(np.float32)

    rows = SEQ // GRID_W
    er, ec = sincos(rows), sincos(GRID_W)
    pos = np.concatenate([
        np.broadcast_to(er[:, None, :], (rows, GRID_W, D_MODEL // 2)),
        np.broadcast_to(ec[None, :, :], (rows, GRID_W, D_MODEL // 2)),
    ], axis=-1).reshape(SEQ, D_MODEL).astype(np.float32)

    def dft(n):
        k = np.arange(n, dtype=np.int64)
        ang = 2.0 * np.pi * ((k[:, None] * k[None, :]) % n).astype(np.float64) / n
        return np.cos(ang) / math.sqrt(n), np.sin(ang) / math.sqrt(n)

    cc, sc = dft(FOURIER_GROUP_DIM)
    chan = np.concatenate([cc, sc], axis=1).astype(np.float32)
    ct, st = dft(SEQ)
    time = np.concatenate([ct, -st], axis=1).astype(np.float32)
    return pos, chan, time


def _mod_kernel(c_ref, w_ref, b_ref, o_ref):
    c = c_ref[...]
    a = c * _sigmoid(c)
    w = w_ref[...]
    a_hi = a.astype(BF16)
    a_lo = (a - a_hi.astype(F32)).astype(BF16)
    w_hi = w.astype(BF16)
    w_lo = (w - w_hi.astype(F32)).astype(BF16)
    o_ref[...] = _dot(a_hi, w_hi) + _dot(a_hi, w_lo) + _dot(a_lo, w_hi) + b_ref[...]


def _mod_call(cc, w_mod, b_mod):
    n = w_mod.shape[1]
    return pl.pallas_call(
        _mod_kernel,
        grid=(n // MOD_TN,),
        in_specs=[
            pl.BlockSpec((2 * SUBLANES, D_MODEL), lambda i: (0, 0)),
            pl.BlockSpec((D_MODEL, MOD_TN), lambda i: (0, i)),
            pl.BlockSpec((1, MOD_TN), lambda i: (0, i)),
        ],
        out_specs=pl.BlockSpec((2 * SUBLANES, MOD_TN), lambda i: (0, i)),
        out_shape=jax.ShapeDtypeStruct((2 * SUBLANES, n), F32),
        name="mod",
        compiler_params=pltpu.CompilerParams(dimension_semantics=("arbitrary",)),
    )(cc, w_mod, b_mod)


def _head_kernel(*refs, lat, n_chunks):
    refs = list(refs)
    xm_ref, xp_ref, xn_ref = refs[:3]
    refs = refs[3:]
    if lat:
        pm_ref, pp_ref, pn_ref = refs[:3]
        refs = refs[3:]
    (sh_ref, sc_ref, g_ref, w_ref, cw_ref, cb_ref, wg_ref, ba_ref, bx_ref, lam_ref) = refs[:10]
    refs = refs[10:]
    if lat:
        xt_out, uy_out, uf_out, ug_out, ls_out, pf_out, pb_out = refs[:7]
        refs = refs[7:]
    e_out, slab_ref, hslab_ref, xs_ref, a_ref, b_ref = refs

    c = pl.program_id(0)
    n_slab = D_MODEL // LANES

    xm = xm_ref[...]
    xp = xp_ref[...]
    xn = xn_ref[...]
    if lat:
        xm = xm + pm_ref[...][None]
        xp = xp + pp_ref[...][None]
        xn = xn + pn_ref[...][None]
    for b in range(BATCH):
        for j in range(n_slab):
            ls = slice(j * LANES, (j + 1) * LANES)
            slab_ref[j, b * PITCH:b * PITCH + TT, :] = xm[b, :, ls]
            hslab_ref[0, j, b * HALO_T:(b + 1) * HALO_T, :] = xp[b, :, ls]
            hslab_ref[1, j, b * HALO_T:(b + 1) * HALO_T, :] = xn[b, :, ls]
    for j in range(n_slab):
        ls = slice(j * LANES, (j + 1) * LANES)
        for k in range(CONV_PAD_LEFT):
            xs_ref[k * BATCH:(k + 1) * BATCH, ls] = (
                hslab_ref[0, j, pl.ds(HALO_T - CONV_PAD_LEFT + k, BATCH, stride=HALO_T), :])
        for t in range(TT):
            xs_ref[LEAD + t * BATCH:LEAD + (t + 1) * BATCH, ls] = slab_ref[j, pl.ds(t, BATCH, stride=PITCH), :]
        for k in range(CONV_WIDTH - 1 - CONV_PAD_LEFT):
            r0 = LEAD + ROWS + k * BATCH
            xs_ref[r0:r0 + BATCH, ls] = hslab_ref[1, j, pl.ds(k, BATCH, stride=HALO_T), :]
    xs_ref[XROWS:XROWS_PAD, :] = jnp.zeros((XROWS_PAD - XROWS, D_MODEL), F32)

    xt = xs_ref[...]
    if lat:
        xt_out[...] = xt[LEAD:LEAD + ROWS]

    n_t = XROWS_PAD // BATCH
    h = _rms(xt, g_ref[...]).reshape(n_t, BATCH, D_MODEL)
    h = h * (1.0 + sc_ref[...][None]) + sh_ref[...][None]
    hb = h.reshape(XROWS_PAD, D_MODEL).astype(BF16)

    keep_first = jnp.where(c == 0, 0.0, 1.0)
    keep_last = jnp.where(c == n_chunks - 1, 0.0, 1.0)
    for n0 in range(0, LRU_WIDTH, NCHUNK):
        xs_ref[:, n0:n0 + NCHUNK] = _dot(hb, w_ref[:, n0:n0 + NCHUNK])
    xs_ref[0:LEAD, :] = xs_ref[0:LEAD, :] * keep_first
    xs_ref[LEAD + ROWS:XROWS, :] = xs_ref[LEAD + ROWS:XROWS, :] * keep_last

    if lat:
        hm = hb[LEAD:LEAD + ROWS]
        for n0 in range(LRU_WIDTH, IN_COLS, NCHUNK):
            r = _dot(hm, w_ref[:, n0:n0 + NCHUNK])
            if n0 < 2 * LRU_WIDTH:
                uy_out[:, n0 - LRU_WIDTH:n0 - LRU_WIDTH + NCHUNK] = r.astype(BF16)
            elif n0 < 2 * LRU_WIDTH + FOURIER_WIDTH:
                for s in range(NCHUNK // LANES):
                    slab_ref[s, 0:ROWS, :] = r[:, s * LANES:(s + 1) * LANES]
                for b in range(BATCH):
                    for s in range(NCHUNK // LANES):
                        c0 = b * FOURIER_WIDTH + (n0 - 2 * LRU_WIDTH) + s * LANES
                        uf_out[:, c0:c0 + LANES] = slab_ref[s, pl.ds(b, TT, stride=BATCH), :].astype(BF16)
            else:
                c0 = n0 - 2 * LRU_WIDTH - FOURIER_WIDTH
                ug_out[:, c0:c0 + NCHUNK] = r.astype(BF16)

    lam = lam_ref[...]
    sp = jnp.maximum(-lam, 0.0) + jnp.log1p(jnp.exp(-jnp.abs(lam)))
    c1 = (-0.5 * LRU_C * LOG2E) * sp
    for hd in range(LRU_HEADS):
        sl = slice(hd * LRU_HEAD_DIM, (hd + 1) * LRU_HEAD_DIM)
        xh = 0.5 * cb_ref[:, sl]
        for k in range(CONV_WIDTH):
            xh = xh + (0.5 * cw_ref[k:k + 1, sl]) * xs_ref[k * BATCH:k * BATCH + ROWS, sl]
        xhb = xh.astype(BF16)
        for d in range(2):
            g = _dot(xhb, wg_ref[d, hd])
            ta = jnp.tanh(g[:, :LRU_HEAD_DIM] + 0.5 * ba_ref[d:d + 1, sl])
            ti = jnp.tanh(g[:, LRU_HEAD_DIM:] + 0.5 * bx_ref[d:d + 1, sl])
            a = jnp.exp2(c1[d:d + 1, sl] * ta + c1[d:d + 1, sl])
            v = 1.0 - a * a
            m = v * lax.rsqrt(jnp.maximum(v, TINY))
            a_ref[d, :, sl] = a
            b_ref[d, :, sl] = (m * xh) * (ti + 1.0)

    for grp in range(LRU_WIDTH // SCAN_LANES):
        sl = slice(grp * SCAN_LANES, (grp + 1) * SCAN_LANES)
        lf = pf = lb = pb = None
        for t in range(TT):
            rf = slice(t * BATCH, (t + 1) * BATCH)
            rb = slice((TT - 1 - t) * BATCH, (TT - t) * BATCH)
            af, bf = a_ref[0, rf, sl], b_ref[0, rf, sl]
            ab, bb = a_ref[1, rb, sl], b_ref[1, rb, sl]
            if t == 0:
                lf, pf, lb, pb = bf, af, bb, ab
            else:
                lf, pf = af * lf + bf, af * pf
                lb, pb = ab * lb + bb, ab * pb
                b_ref[0, rf, sl] = lf
                a_ref[0, rf, sl] = pf
                b_ref[1, rb, sl] = lb
                a_ref[1, rb, sl] = pb
        e_out[0, :, sl] = lf
        e_out[1, :, sl] = pf
        e_out[2, :, sl] = lb
        e_out[3, :, sl] = pb

    if lat:
        ls_out[...] = (b_ref[0] + b_ref[1]).astype(BF16)
        pf_out[...] = a_ref[0].astype(BF16)
        pb_out[...] = a_ref[1].astype(BF16)


def _head_call(x, pos, sh, sc, g, w, conv_w, conv_b, wg, b_a, b_x, lam, *, lat):
    seq = x.shape[1]
    n_chunks = seq // TT
    n_halo = seq // HALO_T
    per_chunk = TT // HALO_T

    def const(shape):
        return pl.BlockSpec(shape, lambda c: (0,) * len(shape))

    def prev_idx(c):
        return jnp.maximum(c * per_chunk - 1, 0)

    def next_idx(c):
        return jnp.minimum((c + 1) * per_chunk, n_halo - 1)

    in_specs = [
        pl.BlockSpec((BATCH, TT, D_MODEL), lambda c: (0, c, 0)),
        pl.BlockSpec((BATCH, HALO_T, D_MODEL), lambda c: (0, prev_idx(c), 0)),
        pl.BlockSpec((BATCH, HALO_T, D_MODEL), lambda c: (0, next_idx(c), 0)),
    ]
    args = [x, x, x]
    if lat:
        in_specs += [
            pl.BlockSpec((TT, D_MODEL), lambda c: (c, 0)),
            pl.BlockSpec((HALO_T, D_MODEL), lambda c: (prev_idx(c), 0)),
            pl.BlockSpec((HALO_T, D_MODEL), lambda c: (next_idx(c), 0)),
        ]
        args += [pos, pos, pos]
    in_specs += [
        const((BATCH, D_MODEL)), const((BATCH, D_MODEL)), const((1, D_MODEL)),
        const(w.shape),
        const((CONV_WIDTH, LRU_WIDTH)), const((1, LRU_WIDTH)),
        const((2, LRU_HEADS, LRU_HEAD_DIM, 2 * LRU_HEAD_DIM)),
        const((2, LRU_WIDTH)), const((2, LRU_WIDTH)), const((2, LRU_WIDTH)),
    ]
    args += [sh, sc, g, w, conv_w, conv_b, wg, b_a, b_x, lam]

    def rows(width):
        return pl.BlockSpec((ROWS, width), lambda c: (c, 0))

    out_specs, out_shape = [], []
    if lat:
        n = n_chunks * ROWS
        out_specs += [rows(D_MODEL), rows(LRU_WIDTH),
                      pl.BlockSpec((TT, BATCH * FOURIER_WIDTH), lambda c: (c, 0)),
                      rows(2 * D_MODEL), rows(LRU_WIDTH), rows(LRU_WIDTH), rows(LRU_WIDTH)]
        out_shape += [
            jax.ShapeDtypeStruct((n, D_MODEL), F32),
            jax.ShapeDtypeStruct((n, LRU_WIDTH), BF16),
            jax.ShapeDtypeStruct((seq, BATCH * FOURIER_WIDTH), BF16),
            jax.ShapeDtypeStruct((n, 2 * D_MODEL), BF16),
            jax.ShapeDtypeStruct((n, LRU_WIDTH), BF16),
            jax.ShapeDtypeStruct((n, LRU_WIDTH), BF16),
            jax.ShapeDtypeStruct((n, LRU_WIDTH), BF16),
        ]
    out_specs.append(pl.BlockSpec((None, 4, BATCH, LRU_WIDTH), lambda c: (c, 0, 0, 0)))
    out_shape.append(jax.ShapeDtypeStruct((n_chunks, 4, BATCH, LRU_WIDTH), F32))

    return pl.pallas_call(
        functools.partial(_head_kernel, lat=lat, n_chunks=n_chunks),
        grid=(n_chunks,),
        in_specs=in_specs,
        out_specs=out_specs,
        out_shape=out_shape,
        scratch_shapes=[
            pltpu.VMEM((D_MODEL // LANES, BATCH * PITCH, LANES), F32),
            pltpu.VMEM((2, D_MODEL // LANES, BATCH * HALO_T, LANES), F32),
            pltpu.VMEM((XROWS_PAD, D_MODEL), F32),
            pltpu.VMEM((2, ROWS, LRU_WIDTH), F32),
            pltpu.VMEM((2, ROWS, LRU_WIDTH), F32),
        ],
        name="head_lat" if lat else "head_ctx",
        compiler_params=pltpu.CompilerParams(
            dimension_semantics=("arbitrary",), vmem_limit_bytes=VMEM_LIMIT),
    )(*args)


def _carry_kernel(ec_ref, el_ref, hf_ref, hb_ref):
    n_ctx, n_lat = ec_ref.shape[0], el_ref.shape[0]
    h = jnp.zeros((BATCH, LRU_WIDTH), F32)
    for c in range(n_ctx):
        h = ec_ref[c, 1] * h + ec_ref[c, 0]
    for c in range(n_lat):
        hf_ref[c] = h
        h = el_ref[c, 1] * h + el_ref[c, 0]
    h = jnp.zeros((BATCH, LRU_WIDTH), F32)
    for c in reversed(range(n_ctx)):
        h = ec_ref[c, 3] * h + ec_ref[c, 2]
    for c in reversed(range(n_lat)):
        hb_ref[c] = h
        h = el_ref[c, 3] * h + el_ref[c, 2]


def _carry_call(e_ctx, e_lat):
    shape = jax.ShapeDtypeStruct((e_lat.shape[0], BATCH, LRU_WIDTH), F32)
    return pl.pallas_call(_carry_kernel, out_shape=[shape, shape], name="carry")(e_ctx, e_lat)


def _fourier_kernel(x_ref, chan_ref, time_ref, o_ref, z_ref):
    for g in range(FOURIER_GROUPS):
        sl = slice(g * FOURIER_GROUP_DIM, (g + 1) * FOURIER_GROUP_DIM)
        r = _dot(x_ref[:, sl], chan_ref[...])
        z_ref[0:SEQ, sl] = r[:, :FOURIER_GROUP_DIM].astype(BF16)
        z_ref[SEQ:2 * SEQ, sl] = r[:, FOURIER_GROUP_DIM:].astype(BF16)
    for m0 in range(0, SEQ, TM):
        o_ref[m0:m0 + TM, :] = _dot(time_ref[m0:m0 + TM, :], z_ref[...]).astype(BF16)


def _fourier_call(uf, chan, time):
    return pl.pallas_call(
        _fourier_kernel,
        grid=(BATCH,),
        in_specs=[
            pl.BlockSpec((SEQ, FOURIER_WIDTH), lambda n: (0, n)),
            pl.BlockSpec((FOURIER_GROUP_DIM, 2 * FOURIER_GROUP_DIM), lambda n: (0, 0)),
            pl.BlockSpec((SEQ, 2 * SEQ), lambda n: (0, 0)),
        ],
        out_specs=pl.BlockSpec((SEQ, FOURIER_WIDTH), lambda n: (0, n)),
        out_shape=jax.ShapeDtypeStruct((SEQ, BATCH * FOURIER_WIDTH), BF16),
        scratch_shapes=[pltpu.VMEM((2 * SEQ, FOURIER_WIDTH), BF16)],
        name="fourier",
        compiler_params=pltpu.CompilerParams(
            dimension_semantics=("arbitrary",), vmem_limit_bytes=VMEM_LIMIT),
    )(uf, chan, time)


def _tail_kernel(ls_ref, pf_ref, pb_ref, hf_ref, hb_ref, uy_ref, ug_ref, yf_ref, xt_ref,
                 gt1_ref, sh2_ref, sc2_ref, gt2_ref, gm_ref, gf_ref,
                 wl_ref, wf_ref, wo_ref, w1_ref, w2_ref, o_ref, fslab_ref, oslab_ref):
    def tb(v):
        return v.reshape(TT, BATCH, v.shape[-1])

    def flat(v):
        return v.reshape(ROWS, v.shape[-1])

    y_lru = (tb(ls_ref[...].astype(F32))
             + tb(pf_ref[...].astype(F32)) * hf_ref[...][None]
             + tb(pb_ref[...].astype(F32)) * hb_ref[...][None])
    z = (flat(y_lru) * _gelu_tanh(uy_ref[...].astype(F32))).astype(BF16)
    y_a = _dot(z, wl_ref[...])

    for b in range(BATCH):
        v = yf_ref[:, b * FOURIER_WIDTH:(b + 1) * FOURIER_WIDTH].astype(F32)
        for s in range(FOURIER_WIDTH // LANES):
            fslab_ref[s, pl.ds(b, TT, stride=BATCH), :] = v[:, s * LANES:(s + 1) * LANES]
    yf = jnp.concatenate([fslab_ref[s] for s in range(FOURIER_WIDTH // LANES)], axis=-1).astype(BF16)
    y_b = _dot(yf, wf_ref[...])

    g_a = _sigmoid(ug_ref[:, :D_MODEL].astype(F32))
    g_b = _sigmoid(ug_ref[:, D_MODEL:].astype(F32))
    merged = (g_a * y_a + g_b * y_b).astype(BF16)
    x2 = tb(xt_ref[...]) + gt1_ref[...][None] * tb(_dot(merged, wo_ref[...]))

    h = (tb(_rms(flat(x2), gm_ref[...])) * (1.0 + sc2_ref[...][None]) + sh2_ref[...][None])
    h = flat(h).astype(BF16)
    acc = jnp.zeros((ROWS, D_MODEL), F32)
    for k0 in range(0, D_FF, D_MODEL):
        a = jnp.maximum(_dot(h, w1_ref[:, k0:k0 + D_MODEL]), 0.0)
        acc = acc + _dot((a * a).astype(BF16), w2_ref[k0:k0 + D_MODEL, :])
    out = _rms(flat(x2 + gt2_ref[...][None] * tb(acc)), gf_ref[...])

    for j in range(D_MODEL // LANES):
        oslab_ref[j] = out[:, j * LANES:(j + 1) * LANES]
    for b in range(BATCH):
        for j in range(D_MODEL // LANES):
            o_ref[b, :, j * LANES:(j + 1) * LANES] = oslab_ref[j, pl.ds(b, TT, stride=BATCH), :]


def _tail_call(ls, pf, pb, hf, hb, uy, ug, yf, xt, gt1, sh2, sc2, gt2, gm, gf, wl, wf, wo, w1, w2):
    def rows(width):
        return pl.BlockSpec((ROWS, width), lambda c: (c, 0))

    def const(shape):
        return pl.BlockSpec(shape, lambda c: (0,) * len(shape))

    state = pl.BlockSpec((None, BATCH, LRU_WIDTH), lambda c: (c, 0, 0))
    vec8 = const((BATCH, D_MODEL))
    vec1 = const((1, D_MODEL))
    return pl.pallas_call(
        _tail_kernel,
        grid=(N_LAT_CHUNKS,),
        in_specs=[
            rows(LRU_WIDTH), rows(LRU_WIDTH), rows(LRU_WIDTH), state, state,
            rows(LRU_WIDTH), rows(2 * D_MODEL),
            pl.BlockSpec((TT, BATCH * FOURIER_WIDTH), lambda c: (c, 0)),
            rows(D_MODEL),
            vec8, vec8, vec8, vec8, vec1, vec1,
            const((LRU_WIDTH, D_MODEL)), const((FOURIER_WIDTH, D_MODEL)), const((D_MODEL, D_MODEL)),
            const((D_MODEL, D_FF)), const((D_FF, D_MODEL)),
        ],
        out_specs=pl.BlockSpec((BATCH, TT, D_MODEL), lambda c: (0, c, 0)),
        out_shape=jax.ShapeDtypeStruct((BATCH, SEQ, D_MODEL), F32),
        scratch_shapes=[
            pltpu.VMEM((FOURIER_WIDTH // LANES, ROWS, LANES), F32),
            pltpu.VMEM((D_MODEL // LANES, ROWS, LANES), F32),
        ],
        name="tail",
        compiler_params=pltpu.CompilerParams(
            dimension_semantics=("arbitrary",), vmem_limit_bytes=VMEM_LIMIT),
    )(ls, pf, pb, hf, hb, uy, ug, yf, xt, gt1, sh2, sc2, gt2, gm, gf, wl, wf, wo, w1, w2)


def kernel(x, c, ctx, c_ctx, w_mod, b_mod, g_mix, w_in, conv_w, conv_b, w_a, b_a, w_x, b_x,
           lam, w_lru_out, w_f_out, w_out, g_mlp, w1, w2, g_final):
    pos_np, chan_np, time_np = _constants()
    pos = jnp.asarray(pos_np)
    chan, time = jnp.asarray(chan_np).astype(BF16), jnp.asarray(time_np).astype(BF16)

    cc = jnp.concatenate(
        [c, c_ctx[None], jnp.zeros((2 * SUBLANES - BATCH - 1, D_MODEL), F32)], axis=0)
    mod = _mod_call(cc, w_mod[0], b_mod[0][None])
    sh1, sc1, gt1, sh2, sc2, gt2 = [
        mod[:BATCH, k * D_MODEL:(k + 1) * D_MODEL] for k in range(N_MOD)]
    sh1c = jnp.broadcast_to(mod[BATCH:BATCH + 1, 0:D_MODEL], (BATCH, D_MODEL))
    sc1c = jnp.broadcast_to(mod[BATCH:BATCH + 1, D_MODEL:2 * D_MODEL], (BATCH, D_MODEL))

    w_in_b = w_in[0].astype(BF16)
    g_mix2 = g_mix[0][None]
    wg = jnp.concatenate([w_a[0], w_x[0]], axis=-1).astype(BF16)
    lru = (conv_w[0], conv_b[0][None], wg, b_a[0], b_x[0], lam[0])

    (e_ctx,) = _head_call(ctx, None, sh1c, sc1c, g_mix2, w_in_b[:, :LRU_WIDTH], *lru, lat=False)
    xt, uy, uf, ug, ls, pf, pb, e_lat = _head_call(x, pos, sh1, sc1, g_mix2, w_in_b, *lru, lat=True)
    hf, hb = _carry_call(e_ctx, e_lat)
    yf = _fourier_call(uf, chan, time)
    return _tail_call(ls, pf, pb, hf, hb, uy, ug, yf, xt, gt1, sh2, sc2, gt2,
                      g_mlp[0][None], g_final[None],
                      w_lru_out[0].astype(BF16), w_f_out[0].astype(BF16), w_out[0].astype(BF16),
                      w1[0].astype(BF16), w2[0].astype(BF16))
```

```python
import functools
import math

import numpy as np
import jax
import jax.numpy as jnp
from jax import lax
from jax.experimental import pallas as pl
from jax.experimental.pallas import tpu as pltpu

F32 = jnp.float32
BF16 = jnp.bfloat16

D_MODEL = 1024
BATCH = 8
SEQ = 2048
CTX_LEN = 256
GRID_W = 64
LRU_WIDTH = 1024
LRU_HEADS = 8
LRU_HEAD_DIM = LRU_WIDTH // LRU_HEADS
LRU_C = 8.0
CONV_WIDTH = 4
CONV_PAD_LEFT = 2
FOURIER_WIDTH = 512
FOURIER_GROUPS = 4
FOURIER_GROUP_DIM = FOURIER_WIDTH // FOURIER_GROUPS
IN_COLS = 2 * LRU_WIDTH + FOURIER_WIDTH + 2 * D_MODEL
D_FF = 4 * D_MODEL
N_MOD = 6
EPS = 1e-6
POS_MAX_PERIOD = 10000.0

LANES = 128
SUBLANES = 8
BF16_ROWS = 16
assert BATCH == SUBLANES

TT = 64
ROWS = TT * BATCH
PITCH = TT + SUBLANES
HALO_T = SUBLANES
LEAD = CONV_PAD_LEFT * BATCH
TRAIL = (CONV_WIDTH - 1 - CONV_PAD_LEFT) * BATCH
XROWS = LEAD + ROWS + TRAIL
XROWS_PAD = -(-XROWS // BF16_ROWS) * BF16_ROWS
N_CTX_CHUNKS = CTX_LEN // TT
N_LAT_CHUNKS = SEQ // TT
NCHUNK = 512
FRONT_ROWS = 128
TM = 512
MOD_TN = 1024
VMEM_LIMIT = 58 * 1024 * 1024
LOG2E = math.log2(math.e)
TINY = 1e-30


def _gelu_tanh(x):
    return 0.5 * x * (1.0 + jnp.tanh(math.sqrt(2.0 / math.pi) * (x + 0.044715 * (x * x * x))))


def _sigmoid(x):
    return 0.5 * jnp.tanh(0.5 * x) + 0.5


def _dot(a, b):
    return jnp.dot(a, b, preferred_element_type=F32)


def _rms(x, g):
    ms = jnp.mean(x * x, axis=-1, keepdims=True)
    return x * lax.rsqrt(ms + EPS) * g


@functools.cache
def _constants():
    half = D_MODEL // 4
    freqs = np.exp(-math.log(POS_MAX_PERIOD) * np.arange(half, dtype=np.float32) / half).astype(np.float32)

    def sincos(n):
        ang = np.arange(n, dtype=np.float32)[:, None] * freqs[None, :]
        return np.concatenate([np.sin(ang), np.cos(ang)], axis=-1).astype(np.float32)

    rows = SEQ // GRID_W
    er, ec = sincos(rows), sincos(GRID_W)
    pos = np.concatenate([
        np.broadcast_to(er[:, None, :], (rows, GRID_W, D_MODEL // 2)),
        np.broadcast_to(ec[None, :, :], (rows, GRID_W, D_MODEL // 2)),
    ], axis=-1).reshape(SEQ, D_MODEL).astype(np.float32)

    def dft(n):
        k = np.arange(n, dtype=np.int64)
        ang = 2.0 * np.pi * ((k[:, None] * k[None, :]) % n).astype(np.float64) / n
        return np.cos(ang) / math.sqrt(n), np.sin(ang) / math.sqrt(n)

    cc, sc = dft(FOURIER_GROUP_DIM)
    chan = np.concatenate([cc, sc], axis=1).astype(np.float32)
    ct, st = dft(SEQ)
    time = np.stack([
        np.concatenate([ct[p::2, :SEQ // 2], -st[p::2, :SEQ // 2]], axis=1) for p in range(2)
    ]).astype(np.float32)
    return pos, chan, time


def _mod_kernel(c_ref, w_ref, b_ref, o_ref):
    c = c_ref[...]
    a = c * _sigmoid(c)
    w = w_ref[...]
    a_hi = a.astype(BF16)
    a_lo = (a - a_hi.astype(F32)).astype(BF16)
    w_hi = w.astype(BF16)
    w_lo = (w - w_hi.astype(F32)).astype(BF16)
    o_ref[...] = _dot(a_hi, w_hi) + _dot(a_hi, w_lo) + _dot(a_lo, w_hi) + b_ref[...]


def _mod_call(cc, w_mod, b_mod):
    n = w_mod.shape[1]
    return pl.pallas_call(
        _mod_kernel,
        grid=(n // MOD_TN,),
        in_specs=[
            pl.BlockSpec((2 * SUBLANES, D_MODEL), lambda i: (0, 0)),
            pl.BlockSpec((D_MODEL, MOD_TN), lambda i: (0, i)),
            pl.BlockSpec((1, MOD_TN), lambda i: (0, i)),
        ],
        out_specs=pl.BlockSpec((2 * SUBLANES, MOD_TN), lambda i: (0, i)),
        out_shape=jax.ShapeDtypeStruct((2 * SUBLANES, n), F32),
        name="mod",
        compiler_params=pltpu.CompilerParams(dimension_semantics=("arbitrary",)),
    )(cc, w_mod, b_mod)


def _head_kernel(*refs, lat, n_chunks):
    refs = list(refs)
    xm_ref, xp_ref, xn_ref = refs[:3]
    refs = refs[3:]
    if lat:
        pm_ref, pp_ref, pn_ref = refs[:3]
        refs = refs[3:]
    (sh_ref, sc_ref, g_ref, w_ref, cw_ref, cb_ref, wg_ref, ba_ref, bx_ref, lam_ref) = refs[:10]
    refs = refs[10:]
    if lat:
        xt_out, uy_out, uf_out, ug_out, ls_out, pf_out, pb_out = refs[:7]
        refs = refs[7:]
    e_out, slab_ref, hslab_ref, xs_ref, hb_ref, a_ref, b_ref, xh_ref, gate_ref = refs

    c = pl.program_id(0)
    n_slab = D_MODEL // LANES

    xm = xm_ref[...]
    xp = xp_ref[...]
    xn = xn_ref[...]
    if lat:
        xm = xm + pm_ref[...][None]
        xp = xp + pp_ref[...][None]
        xn = xn + pn_ref[...][None]
    for b in range(BATCH):
        for j in range(n_slab):
            ls = slice(j * LANES, (j + 1) * LANES)
            slab_ref[j, b * PITCH:b * PITCH + TT, :] = xm[b, :, ls]
            hslab_ref[0, j, b * HALO_T:(b + 1) * HALO_T, :] = xp[b, :, ls]
            hslab_ref[1, j, b * HALO_T:(b + 1) * HALO_T, :] = xn[b, :, ls]
    for j in range(n_slab):
        ls = slice(j * LANES, (j + 1) * LANES)
        for k in range(CONV_PAD_LEFT):
            xs_ref[k * BATCH:(k + 1) * BATCH, ls] = (
                hslab_ref[0, j, pl.ds(HALO_T - CONV_PAD_LEFT + k, BATCH, stride=HALO_T), :])
        for t in range(TT):
            xs_ref[LEAD + t * BATCH:LEAD + (t + 1) * BATCH, ls] = slab_ref[j, pl.ds(t, BATCH, stride=PITCH), :]
        for k in range(CONV_WIDTH - 1 - CONV_PAD_LEFT):
            r0 = LEAD + ROWS + k * BATCH
            xs_ref[r0:r0 + BATCH, ls] = hslab_ref[1, j, pl.ds(k, BATCH, stride=HALO_T), :]
    xs_ref[XROWS:XROWS_PAD, :] = jnp.zeros((XROWS_PAD - XROWS, D_MODEL), F32)

    assert LEAD % BF16_ROWS == 0 and FRONT_ROWS % BF16_ROWS == 0
    blocks = ([(0, LEAD)] + [(r, r + FRONT_ROWS) for r in range(LEAD, LEAD + ROWS, FRONT_ROWS)]
              + [(LEAD + ROWS, XROWS_PAD)])
    for r0, r1 in blocks:
        xt = xs_ref[r0:r1, :]
        if lat and LEAD <= r0 < LEAD + ROWS:
            xt_out[r0 - LEAD:r1 - LEAD, :] = xt
        h = _rms(xt, g_ref[...]).reshape((r1 - r0) // BATCH, BATCH, D_MODEL)
        h = h * (1.0 + sc_ref[...][None]) + sh_ref[...][None]
        hb_ref[r0:r1, :] = h.reshape(r1 - r0, D_MODEL).astype(BF16)

    keep_first = jnp.where(c == 0, 0.0, 1.0)
    keep_last = jnp.where(c == n_chunks - 1, 0.0, 1.0)
    for n0 in range(0, LRU_WIDTH, NCHUNK):
        xs_ref[:, n0:n0 + NCHUNK] = _dot(hb_ref[...], w_ref[:, n0:n0 + NCHUNK])
    xs_ref[0:LEAD, :] = xs_ref[0:LEAD, :] * keep_first
    xs_ref[LEAD + ROWS:XROWS, :] = xs_ref[LEAD + ROWS:XROWS, :] * keep_last

    def branch_piece(n0):
        r = _dot(hb_ref[LEAD:LEAD + ROWS, :], w_ref[:, n0:n0 + NCHUNK])
        if n0 < 2 * LRU_WIDTH:
            uy_out[:, n0 - LRU_WIDTH:n0 - LRU_WIDTH + NCHUNK] = r.astype(BF16)
        elif n0 < 2 * LRU_WIDTH + FOURIER_WIDTH:
            for s in range(NCHUNK // LANES):
                slab_ref[s, 0:ROWS, :] = r[:, s * LANES:(s + 1) * LANES]
            for b in range(BATCH):
                for s in range(NCHUNK // LANES):
                    c0 = b * FOURIER_WIDTH + (n0 - 2 * LRU_WIDTH) + s * LANES
                    uf_out[:, c0:c0 + LANES] = slab_ref[s, pl.ds(b, TT, stride=BATCH), :].astype(BF16)
        else:
            c0 = n0 - 2 * LRU_WIDTH - FOURIER_WIDTH
            ug_out[:, c0:c0 + NCHUNK] = r.astype(BF16)

    pieces = list(range(LRU_WIDTH, IN_COLS, NCHUNK)) if lat else []

    lam = lam_ref[...]
    sp = jnp.maximum(-lam, 0.0) + jnp.log1p(jnp.exp(-jnp.abs(lam)))
    c1 = (-0.5 * LRU_C * LOG2E) * sp
    def conv_gates(hd):
        sl = slice(hd * LRU_HEAD_DIM, (hd + 1) * LRU_HEAD_DIM)
        xh = 0.5 * cb_ref[:, sl]
        for k in range(CONV_WIDTH):
            xh = xh + (0.5 * cw_ref[k:k + 1, sl]) * xs_ref[k * BATCH:k * BATCH + ROWS, sl]
        xh_ref[hd % 2] = xh
        xhb = xh.astype(BF16)
        for d in range(2):
            gate_ref[hd % 2, d] = _dot(xhb, wg_ref[d, hd])

    conv_gates(0)
    for hd in range(LRU_HEADS):
        if hd + 1 < LRU_HEADS:
            conv_gates(hd + 1)
        if pieces:
            branch_piece(pieces.pop(0))
        sl = slice(hd * LRU_HEAD_DIM, (hd + 1) * LRU_HEAD_DIM)
        xh = xh_ref[hd % 2]
        for d in range(2):
            g = gate_ref[hd % 2, d]
            ta = jnp.tanh(g[:, :LRU_HEAD_DIM] + 0.5 * ba_ref[d:d + 1, sl])
            ti = jnp.tanh(g[:, LRU_HEAD_DIM:] + 0.5 * bx_ref[d:d + 1, sl])
            a = jnp.exp2(c1[d:d + 1, sl] * ta + c1[d:d + 1, sl])
            v = 1.0 - a * a
            m = v * lax.rsqrt(jnp.maximum(v, TINY))
            a_ref[d, :, sl] = a
            b_ref[d, :, sl] = (m * xh) * (ti + 1.0)

        lf = pf = lb = pb = None
        for t in range(TT):
            rf = slice(t * BATCH, (t + 1) * BATCH)
            rb = slice((TT - 1 - t) * BATCH, (TT - t) * BATCH)
            af, bf = a_ref[0, rf, sl], b_ref[0, rf, sl]
            ab, bb = a_ref[1, rb, sl], b_ref[1, rb, sl]
            if t == 0:
                lf, pf, lb, pb = bf, af, bb, ab
            else:
                lf, pf = af * lf + bf, af * pf
                lb, pb = ab * lb + bb, ab * pb
                b_ref[0, rf, sl] = lf
                a_ref[0, rf, sl] = pf
                b_ref[1, rb, sl] = lb
                a_ref[1, rb, sl] = pb
        e_out[0, :, sl] = lf
        e_out[1, :, sl] = pf
        e_out[2, :, sl] = lb
        e_out[3, :, sl] = pb
        if lat:
            ls_out[:, sl] = (b_ref[0, :, sl] + b_ref[1, :, sl]).astype(BF16)
            pf_out[:, sl] = a_ref[0, :, sl].astype(BF16)
            pb_out[:, sl] = a_ref[1, :, sl].astype(BF16)
    assert not pieces


def _head_call(x, pos, sh, sc, g, w, conv_w, conv_b, wg, b_a, b_x, lam, *, lat):
    seq = x.shape[1]
    n_chunks = seq // TT
    n_halo = seq // HALO_T
    per_chunk = TT // HALO_T

    def const(shape):
        return pl.BlockSpec(shape, lambda c: (0,) * len(shape))

    def prev_idx(c):
        return jnp.maximum(c * per_chunk - 1, 0)

    def next_idx(c):
        return jnp.minimum((c + 1) * per_chunk, n_halo - 1)

    in_specs = [
        pl.BlockSpec((BATCH, TT, D_MODEL), lambda c: (0, c, 0)),
        pl.BlockSpec((BATCH, HALO_T, D_MODEL), lambda c: (0, prev_idx(c), 0)),
        pl.BlockSpec((BATCH, HALO_T, D_MODEL), lambda c: (0, next_idx(c), 0)),
    ]
    args = [x, x, x]
    if lat:
        in_specs += [
            pl.BlockSpec((TT, D_MODEL), lambda c: (c, 0)),
            pl.BlockSpec((HALO_T, D_MODEL), lambda c: (prev_idx(c), 0)),
            pl.BlockSpec((HALO_T, D_MODEL), lambda c: (next_idx(c), 0)),
        ]
        args += [pos, pos, pos]
    in_specs += [
        const((BATCH, D_MODEL)), const((BATCH, D_MODEL)), const((1, D_MODEL)),
        const(w.shape),
        const((CONV_WIDTH, LRU_WIDTH)), const((1, LRU_WIDTH)),
        const((2, LRU_HEADS, LRU_HEAD_DIM, 2 * LRU_HEAD_DIM)),
        const((2, LRU_WIDTH)), const((2, LRU_WIDTH)), const((2, LRU_WIDTH)),
    ]
    args += [sh, sc, g, w, conv_w, conv_b, wg, b_a, b_x, lam]

    def rows(width):
        return pl.BlockSpec((ROWS, width), lambda c: (c, 0))

    out_specs, out_shape = [], []
    if lat:
        n = n_chunks * ROWS
        out_specs += [rows(D_MODEL), rows(LRU_WIDTH),
                      pl.BlockSpec((TT, BATCH * FOURIER_WIDTH), lambda c: (c, 0)),
                      rows(2 * D_MODEL), rows(LRU_WIDTH), rows(LRU_WIDTH), rows(LRU_WIDTH)]
        out_shape += [
            jax.ShapeDtypeStruct((n, D_MODEL), F32),
            jax.ShapeDtypeStruct((n, LRU_WIDTH), BF16),
            jax.ShapeDtypeStruct((seq, BATCH * FOURIER_WIDTH), BF16),
            jax.ShapeDtypeStruct((n, 2 * D_MODEL), BF16),
            jax.ShapeDtypeStruct((n, LRU_WIDTH), BF16),
            jax.ShapeDtypeStruct((n, LRU_WIDTH), BF16),
            jax.ShapeDtypeStruct((n, LRU_WIDTH), BF16),
        ]
    out_specs.append(pl.BlockSpec((None, 4, BATCH, LRU_WIDTH), lambda c: (c, 0, 0, 0)))
    out_shape.append(jax.ShapeDtypeStruct((n_chunks, 4, BATCH, LRU_WIDTH), F32))

    return pl.pallas_call(
        functools.partial(_head_kernel, lat=lat, n_chunks=n_chunks),
        grid=(n_chunks,),
        in_specs=in_specs,
        out_specs=out_specs,
        out_shape=out_shape,
        scratch_shapes=[
            pltpu.VMEM((D_MODEL // LANES, BATCH * PITCH, LANES), F32),
            pltpu.VMEM((2, D_MODEL // LANES, BATCH * HALO_T, LANES), F32),
            pltpu.VMEM((XROWS_PAD, D_MODEL), F32),
            pltpu.VMEM((XROWS_PAD, D_MODEL), BF16),
            pltpu.VMEM((2, ROWS, LRU_WIDTH), F32),
            pltpu.VMEM((2, ROWS, LRU_WIDTH), F32),
            pltpu.VMEM((2, ROWS, LRU_HEAD_DIM), F32),
            pltpu.VMEM((2, 2, ROWS, 2 * LRU_HEAD_DIM), F32),
        ],
        name="head_lat" if lat else "head_ctx",
        compiler_params=pltpu.CompilerParams(
            dimension_semantics=("arbitrary",), vmem_limit_bytes=VMEM_LIMIT),
    )(*args)


def _carry_kernel(ec_ref, el_ref, hf_ref, hb_ref):
    n_ctx, n_lat = ec_ref.shape[0], el_ref.shape[0]
    h = jnp.zeros((BATCH, LRU_WIDTH), F32)
    for c in range(n_ctx):
        h = ec_ref[c, 1] * h + ec_ref[c, 0]
    for c in range(n_lat):
        hf_ref[c] = h
        h = el_ref[c, 1] * h + el_ref[c, 0]
    h = jnp.zeros((BATCH, LRU_WIDTH), F32)
    for c in reversed(range(n_ctx)):
        h = ec_ref[c, 3] * h + ec_ref[c, 2]
    for c in reversed(range(n_lat)):
        hb_ref[c] = h
        h = el_ref[c, 3] * h + el_ref[c, 2]


def _carry_call(e_ctx, e_lat):
    shape = jax.ShapeDtypeStruct((e_lat.shape[0], BATCH, LRU_WIDTH), F32)
    return pl.pallas_call(_carry_kernel, out_shape=[shape, shape], name="carry")(e_ctx, e_lat)


def _fourier_kernel(x_ref, chan_ref, time_ref, o_ref, z_ref):
    half = SEQ // 2
    for g in range(FOURIER_GROUPS):
        sl = slice(g * FOURIER_GROUP_DIM, (g + 1) * FOURIER_GROUP_DIM)
        r = _dot(x_ref[:, sl], chan_ref[...])
        lo, hi = r[:half], r[half:]
        for p, v in enumerate((lo + hi, lo - hi)):
            z_ref[p, 0:half, sl] = v[:, :FOURIER_GROUP_DIM].astype(BF16)
            z_ref[p, half:SEQ, sl] = v[:, FOURIER_GROUP_DIM:].astype(BF16)
    for p in range(2):
        for m0 in range(0, half, TM):
            o_ref[p, m0:m0 + TM, :] = _dot(time_ref[p, m0:m0 + TM, :], z_ref[p]).astype(BF16)


def _fourier_call(uf, chan, time):
    return pl.pallas_call(
        _fourier_kernel,
        grid=(BATCH,),
        in_specs=[
            pl.BlockSpec((SEQ, FOURIER_WIDTH), lambda n: (0, n)),
            pl.BlockSpec((FOURIER_GROUP_DIM, 2 * FOURIER_GROUP_DIM), lambda n: (0, 0)),
            pl.BlockSpec((2, SEQ // 2, SEQ), lambda n: (0, 0, 0)),
        ],
        out_specs=pl.BlockSpec((2, SEQ // 2, FOURIER_WIDTH), lambda n: (0, 0, n)),
        out_shape=jax.ShapeDtypeStruct((2, SEQ // 2, BATCH * FOURIER_WIDTH), BF16),
        scratch_shapes=[pltpu.VMEM((2, SEQ, FOURIER_WIDTH), BF16)],
        name="fourier",
        compiler_params=pltpu.CompilerParams(
            dimension_semantics=("arbitrary",), vmem_limit_bytes=VMEM_LIMIT),
    )(uf, chan, time)


def _tail_kernel(ls_ref, pf_ref, pb_ref, hf_ref, hb_ref, uy_ref, ug_ref, yf_ref, xt_ref,
                 gt1_ref, sh2_ref, sc2_ref, gt2_ref, gm_ref, gf_ref,
                 wl_ref, wf_ref, wo_ref, w1_ref, w2_ref, o_ref, fslab_ref, oslab_ref):
    def tb(v):
        return v.reshape(TT, BATCH, v.shape[-1])

    def flat(v):
        return v.reshape(ROWS, v.shape[-1])

    y_lru = (tb(ls_ref[...].astype(F32))
             + tb(pf_ref[...].astype(F32)) * hf_ref[...][None]
             + tb(pb_ref[...].astype(F32)) * hb_ref[...][None])
    z = (flat(y_lru) * _gelu_tanh(uy_ref[...].astype(F32))).astype(BF16)
    y_a = _dot(z, wl_ref[...])

    for b in range(BATCH):
        for p in range(2):
            v = yf_ref[p, :, b * FOURIER_WIDTH:(b + 1) * FOURIER_WIDTH].astype(F32)
            for s in range(FOURIER_WIDTH // LANES):
                fslab_ref[s, pl.ds(p * BATCH + b, TT // 2, stride=2 * BATCH), :] = (
                    v[:, s * LANES:(s + 1) * LANES])
    yf = jnp.concatenate([fslab_ref[s] for s in range(FOURIER_WIDTH // LANES)], axis=-1).astype(BF16)
    y_b = _dot(yf, wf_ref[...])

    g_a = _sigmoid(ug_ref[:, :D_MODEL].astype(F32))
    g_b = _sigmoid(ug_ref[:, D_MODEL:].astype(F32))
    merged = (g_a * y_a + g_b * y_b).astype(BF16)
    x2 = tb(xt_ref[...]) + gt1_ref[...][None] * tb(_dot(merged, wo_ref[...]))

    h = (tb(_rms(flat(x2), gm_ref[...])) * (1.0 + sc2_ref[...][None]) + sh2_ref[...][None])
    h = flat(h).astype(BF16)
    acc = jnp.zeros((ROWS, D_MODEL), F32)
    for k0 in range(0, D_FF, D_MODEL):
        a = jnp.maximum(_dot(h, w1_ref[:, k0:k0 + D_MODEL]), 0.0)
        acc = acc + _dot((a * a).astype(BF16), w2_ref[k0:k0 + D_MODEL, :])
    out = _rms(flat(x2 + gt2_ref[...][None] * tb(acc)), gf_ref[...])

    for j in range(D_MODEL // LANES):
        oslab_ref[j] = out[:, j * LANES:(j + 1) * LANES]
    for b in range(BATCH):
        for j in range(D_MODEL // LANES):
            o_ref[b, :, j * LANES:(j + 1) * LANES] = oslab_ref[j, pl.ds(b, TT, stride=BATCH), :]


def _tail_call(ls, pf, pb, hf, hb, uy, ug, yf, xt, gt1, sh2, sc2, gt2, gm, gf, wl, wf, wo, w1, w2):
    def rows(width):
        return pl.BlockSpec((ROWS, width), lambda c: (c, 0))

    def const(shape):
        return pl.BlockSpec(shape, lambda c: (0,) * len(shape))

    state = pl.BlockSpec((None, BATCH, LRU_WIDTH), lambda c: (c, 0, 0))
    vec8 = const((BATCH, D_MODEL))
    vec1 = const((1, D_MODEL))
    return pl.pallas_call(
        _tail_kernel,
        grid=(N_LAT_CHUNKS,),
        in_specs=[
            rows(LRU_WIDTH), rows(LRU_WIDTH), rows(LRU_WIDTH), state, state,
            rows(LRU_WIDTH), rows(2 * D_MODEL),
            pl.BlockSpec((2, TT // 2, BATCH * FOURIER_WIDTH), lambda c: (0, c, 0)),
            rows(D_MODEL),
            vec8, vec8, vec8, vec8, vec1, vec1,
            const((LRU_WIDTH, D_MODEL)), const((FOURIER_WIDTH, D_MODEL)), const((D_MODEL, D_MODEL)),
            const((D_MODEL, D_FF)), const((D_FF, D_MODEL)),
        ],
        out_specs=pl.BlockSpec((BATCH, TT, D_MODEL), lambda c: (0, c, 0)),
        out_shape=jax.ShapeDtypeStruct((BATCH, SEQ, D_MODEL), F32),
        scratch_shapes=[
            pltpu.VMEM((FOURIER_WIDTH // LANES, ROWS, LANES), F32),
            pltpu.VMEM((D_MODEL // LANES, ROWS, LANES), F32),
        ],
        name="tail",
        compiler_params=pltpu.CompilerParams(
            dimension_semantics=("arbitrary",), vmem_limit_bytes=VMEM_LIMIT),
    )(ls, pf, pb, hf, hb, uy, ug, yf, xt, gt1, sh2, sc2, gt2, gm, gf, wl, wf, wo, w1, w2)


def kernel(x, c, ctx, c_ctx, w_mod, b_mod, g_mix, w_in, conv_w, conv_b, w_a, b_a, w_x, b_x,
           lam, w_lru_out, w_f_out, w_out, g_mlp, w1, w2, g_final):
    pos_np, chan_np, time_np = _constants()
    pos = jnp.asarray(pos_np)
    chan, time = jnp.asarray(chan_np).astype(BF16), jnp.asarray(time_np).astype(BF16)

    cc = jnp.concatenate(
        [c, c_ctx[None], jnp.zeros((2 * SUBLANES - BATCH - 1, D_MODEL), F32)], axis=0)
    mod = _mod_call(cc, w_mod[0], b_mod[0][None])
    sh1, sc1, gt1, sh2, sc2, gt2 = [
        mod[:BATCH, k * D_MODEL:(k + 1) * D_MODEL] for k in range(N_MOD)]
    sh1c = jnp.broadcast_to(mod[BATCH:BATCH + 1, 0:D_MODEL], (BATCH, D_MODEL))
    sc1c = jnp.broadcast_to(mod[BATCH:BATCH + 1, D_MODEL:2 * D_MODEL], (BATCH, D_MODEL))

    w_in_b = w_in[0].astype(BF16)
    g_mix2 = g_mix[0][None]
    wg = jnp.concatenate([w_a[0], w_x[0]], axis=-1).astype(BF16)
    lru = (conv_w[0], conv_b[0][None], wg, b_a[0], b_x[0], lam[0])

    (e_ctx,) = _head_call(ctx, None, sh1c, sc1c, g_mix2, w_in_b[:, :LRU_WIDTH], *lru, lat=False)
    xt, uy, uf, ug, ls, pf, pb, e_lat = _head_call(x, pos, sh1, sc1, g_mix2, w_in_b, *lru, lat=True)
    hf, hb = _carry_call(e_ctx, e_lat)
    yf = _fourier_call(uf, chan, time)
    return _tail_call(ls, pf, pb, hf, hb, uy, ug, yf, xt, gt1, sh2, sc2, gt2,
                      g_mlp[0][None], g_final[None],
                      w_lru_out[0].astype(BF16), w_f_out[0].astype(BF16), w_out[0].astype(BF16),
                      w1[0].astype(BF16), w2[0].astype(BF16))
```

```python
import functools
import math

import numpy as np
import jax
import jax.numpy as jnp
from jax import lax
from jax.experimental import pallas as pl
from jax.experimental.pallas import tpu as pltpu

F32 = jnp.float32
BF16 = jnp.bfloat16

D_MODEL = 1024
BATCH = 8
SEQ = 2048
CTX_LEN = 256
GRID_W = 64
LRU_WIDTH = 1024
LRU_HEADS = 8
LRU_HEAD_DIM = LRU_WIDTH // LRU_HEADS
LRU_C = 8.0
CONV_WIDTH = 4
CONV_PAD_LEFT = 2
FOURIER_WIDTH = 512
FOURIER_GROUPS = 4
FOURIER_GROUP_DIM = FOURIER_WIDTH // FOURIER_GROUPS
IN_COLS = 2 * LRU_WIDTH + FOURIER_WIDTH + 2 * D_MODEL
D_FF = 4 * D_MODEL
N_MOD = 6
EPS = 1e-6
POS_MAX_PERIOD = 10000.0

LANES = 128
SUBLANES = 8
BF16_ROWS = 16
assert BATCH == SUBLANES

TT = 64
ROWS = TT * BATCH
PITCH = TT + SUBLANES
HALO_T = SUBLANES
LEAD = CONV_PAD_LEFT * BATCH
TRAIL = (CONV_WIDTH - 1 - CONV_PAD_LEFT) * BATCH
XROWS = LEAD + ROWS + TRAIL
XROWS_PAD = -(-XROWS // BF16_ROWS) * BF16_ROWS
N_CTX_CHUNKS = CTX_LEN // TT
N_LAT_CHUNKS = SEQ // TT
NCHUNK = 512
FRONT_ROWS = 128
TM = 512
TAIL_WEIGHT_ROWS = (LRU_WIDTH, FOURIER_WIDTH, D_MODEL, D_MODEL, D_FF)
MOD_TN = 1024
VMEM_LIMIT = 58 * 1024 * 1024
LOG2E = math.log2(math.e)
TINY = 1e-30


def _gelu_tanh(x):
    k0 = jnp.full((1, 1), math.sqrt(2.0 / math.pi), F32).astype(x.dtype)
    k1 = jnp.full((1, 1), 0.044715, F32).astype(x.dtype)
    return 0.5 * x * (1.0 + jnp.tanh(k0 * (x + k1 * (x * x * x))))


def _sigmoid(x):
    return 0.5 * jnp.tanh(0.5 * x) + 0.5


def _dot(a, b):
    return jnp.dot(a, b, preferred_element_type=F32)


def _rms(x, g):
    ms = jnp.mean(x * x, axis=-1, keepdims=True)
    return x * lax.rsqrt(ms + EPS) * g


@functools.cache
def _constants():
    half = D_MODEL // 4
    freqs = np.exp(-math.log(POS_MAX_PERIOD) * np.arange(half, dtype=np.float32) / half).astype(np.float32)

    def sincos(n):
        ang = np.arange(n, dtype=np.float32)[:, None] * freqs[None, :]
        return np.concatenate([np.sin(ang), np.cos(ang)], axis=-1).astype(np.float32)

    rows = SEQ // GRID_W
    er, ec = sincos(rows), sincos(GRID_W)
    pos = np.concatenate([
        np.broadcast_to(er[:, None, :], (rows, GRID_W, D_MODEL // 2)),
        np.broadcast_to(ec[None, :, :], (rows, GRID_W, D_MODEL // 2)),
    ], axis=-1).reshape(SEQ, D_MODEL).astype(np.float32)

    def dft(n):
        k = np.arange(n, dtype=np.int64)
        ang = 2.0 * np.pi * ((k[:, None] * k[None, :]) % n).astype(np.float64) / n
        return np.cos(ang) / math.sqrt(n), np.sin(ang) / math.sqrt(n)

    cc, sc = dft(FOURIER_GROUP_DIM)
    chan = np.concatenate([cc, sc], axis=1).astype(np.float32)
    ct, st = dft(SEQ)
    time = np.stack([
        np.concatenate([ct[p::2, :SEQ // 2], -st[p::2, :SEQ // 2]], axis=1) for p in range(2)
    ]).astype(np.float32)
    return pos, chan, time


def _mod_kernel(c_ref, w_ref, b_ref, o_ref):
    c = c_ref[...]
    a = c * _sigmoid(c)
    w = w_ref[...]
    a_hi = a.astype(BF16)
    a_lo = (a - a_hi.astype(F32)).astype(BF16)
    w_hi = w.astype(BF16)
    w_lo = (w - w_hi.astype(F32)).astype(BF16)
    o_ref[...] = _dot(a_hi, w_hi) + _dot(a_hi, w_lo) + _dot(a_lo, w_hi) + b_ref[...]


def _mod_call(cc, w_mod, b_mod):
    n = w_mod.shape[1]
    return pl.pallas_call(
        _mod_kernel,
        grid=(n // MOD_TN,),
        in_specs=[
            pl.BlockSpec((2 * SUBLANES, D_MODEL), lambda i: (0, 0)),
            pl.BlockSpec((D_MODEL, MOD_TN), lambda i: (0, i)),
            pl.BlockSpec((1, MOD_TN), lambda i: (0, i)),
        ],
        out_specs=pl.BlockSpec((2 * SUBLANES, MOD_TN), lambda i: (0, i)),
        out_shape=jax.ShapeDtypeStruct((2 * SUBLANES, n), F32),
        name="mod",
        compiler_params=pltpu.CompilerParams(dimension_semantics=("arbitrary",)),
    )(cc, w_mod, b_mod)


def _mod_spec(row_block, k):
    return pl.BlockSpec((BATCH, D_MODEL), lambda c: (row_block, k))


def _head_kernel(*refs, lat, n_chunks):
    refs = list(refs)
    xm_ref, xp_ref, xn_ref = refs[:3]
    refs = refs[3:]
    if lat:
        pm_ref, pp_ref, pn_ref = refs[:3]
        refs = refs[3:]
    (sh_ref, sc_ref, g_ref, w_ref, cw_ref, cb_ref, wg_ref, ba_ref, bx_ref, lam_ref) = refs[:10]
    refs = refs[10:]
    if lat:
        xt_out, uy_out, uf_out, ug_out, ls_out, pf_out, pb_out = refs[:7]
        refs = refs[7:]
    e_out, slab_ref, hslab_ref, xs_ref, hb_ref, a_ref, b_ref, xh_ref, gate_ref = refs

    c = pl.program_id(0)
    n_slab = D_MODEL // LANES
    mod_rows = slice(None) if lat else slice(0, 1)

    xm = xm_ref[...]
    xp = xp_ref[...]
    xn = xn_ref[...]
    if lat:
        xm = xm + pm_ref[...][None]
        xp = xp + pp_ref[...][None]
        xn = xn + pn_ref[...][None]
    for b in range(BATCH):
        for j in range(n_slab):
            ls = slice(j * LANES, (j + 1) * LANES)
            slab_ref[j, b * PITCH:b * PITCH + TT, :] = xm[b, :, ls]
            hslab_ref[0, j, b * HALO_T:(b + 1) * HALO_T, :] = xp[b, :, ls]
            hslab_ref[1, j, b * HALO_T:(b + 1) * HALO_T, :] = xn[b, :, ls]
    for j in range(n_slab):
        ls = slice(j * LANES, (j + 1) * LANES)
        for k in range(CONV_PAD_LEFT):
            xs_ref[k * BATCH:(k + 1) * BATCH, ls] = (
                hslab_ref[0, j, pl.ds(HALO_T - CONV_PAD_LEFT + k, BATCH, stride=HALO_T), :])
        for t in range(TT):
            xs_ref[LEAD + t * BATCH:LEAD + (t + 1) * BATCH, ls] = slab_ref[j, pl.ds(t, BATCH, stride=PITCH), :]
        for k in range(CONV_WIDTH - 1 - CONV_PAD_LEFT):
            r0 = LEAD + ROWS + k * BATCH
            xs_ref[r0:r0 + BATCH, ls] = hslab_ref[1, j, pl.ds(k, BATCH, stride=HALO_T), :]
    xs_ref[XROWS:XROWS_PAD, :] = jnp.zeros((XROWS_PAD - XROWS, D_MODEL), F32)

    assert LEAD % BF16_ROWS == 0 and FRONT_ROWS % BF16_ROWS == 0
    blocks = ([(0, LEAD)] + [(r, r + FRONT_ROWS) for r in range(LEAD, LEAD + ROWS, FRONT_ROWS)]
              + [(LEAD + ROWS, XROWS_PAD)])
    for r0, r1 in blocks:
        xt = xs_ref[r0:r1, :]
        if lat and LEAD <= r0 < LEAD + ROWS:
            xt_out[r0 - LEAD:r1 - LEAD, :] = xt
        h = _rms(xt, g_ref[...]).reshape((r1 - r0) // BATCH, BATCH, D_MODEL)
        h = h * (1.0 + sc_ref[mod_rows, :][None]) + sh_ref[mod_rows, :][None]
        hb_ref[r0:r1, :] = h.reshape(r1 - r0, D_MODEL).astype(BF16)

    keep_first = jnp.where(c == 0, 0.0, 1.0)
    keep_last = jnp.where(c == n_chunks - 1, 0.0, 1.0)
    for n0 in range(0, LRU_WIDTH, NCHUNK):
        xs_ref[:, n0:n0 + NCHUNK] = _dot(hb_ref[...], w_ref[:, n0:n0 + NCHUNK])
    xs_ref[0:LEAD, :] = xs_ref[0:LEAD, :] * keep_first
    xs_ref[LEAD + ROWS:XROWS, :] = xs_ref[LEAD + ROWS:XROWS, :] * keep_last

    def branch_piece(n0):
        r = _dot(hb_ref[LEAD:LEAD + ROWS, :], w_ref[:, n0:n0 + NCHUNK])
        if n0 < 2 * LRU_WIDTH:
            uy_out[:, n0 - LRU_WIDTH:n0 - LRU_WIDTH + NCHUNK] = r.astype(BF16)
        elif n0 < 2 * LRU_WIDTH + FOURIER_WIDTH:
            for s in range(NCHUNK // LANES):
                slab_ref[s, 0:ROWS, :] = r[:, s * LANES:(s + 1) * LANES]
            for b in range(BATCH):
                for s in range(NCHUNK // LANES):
                    c0 = b * FOURIER_WIDTH + (n0 - 2 * LRU_WIDTH) + s * LANES
                    uf_out[:, c0:c0 + LANES] = slab_ref[s, pl.ds(b, TT, stride=BATCH), :].astype(BF16)
        else:
            c0 = n0 - 2 * LRU_WIDTH - FOURIER_WIDTH
            ug_out[:, c0:c0 + NCHUNK] = r.astype(BF16)

    pieces = list(range(LRU_WIDTH, IN_COLS, NCHUNK)) if lat else []

    lam = lam_ref[...]
    sp = jnp.maximum(-lam, 0.0) + jnp.log1p(jnp.exp(-jnp.abs(lam)))
    c1 = (-0.5 * LRU_C * LOG2E) * sp

    def conv_gates(hd):
        sl = slice(hd * LRU_HEAD_DIM, (hd + 1) * LRU_HEAD_DIM)
        xh = 0.5 * cb_ref[:, sl]
        for k in range(CONV_WIDTH):
            xh = xh + (0.5 * cw_ref[k:k + 1, sl]) * xs_ref[k * BATCH:k * BATCH + ROWS, sl]
        xh_ref[hd % 2] = xh
        xhb = xh.astype(BF16)
        for d in range(2):
            gate_ref[hd % 2, d] = _dot(xhb, wg_ref[d, hd])

    conv_gates(0)
    for hd in range(LRU_HEADS):
        if hd + 1 < LRU_HEADS:
            conv_gates(hd + 1)
        if pieces:
            branch_piece(pieces.pop(0))
        sl = slice(hd * LRU_HEAD_DIM, (hd + 1) * LRU_HEAD_DIM)
        xh = xh_ref[hd % 2]
        for d in range(2):
            g = gate_ref[hd % 2, d]
            ta = jnp.tanh(g[:, :LRU_HEAD_DIM] + 0.5 * ba_ref[d:d + 1, sl])
            ti = jnp.tanh(g[:, LRU_HEAD_DIM:] + 0.5 * bx_ref[d:d + 1, sl])
            a = jnp.exp2(c1[d:d + 1, sl] * ta + c1[d:d + 1, sl])
            v = 1.0 - a * a
            m = v * lax.rsqrt(jnp.maximum(v, TINY))
            a_ref[d, :, sl] = a
            b_ref[d, :, sl] = (m * xh) * (ti + 1.0)

        lf = pf = lb = pb = None
        for t in range(TT):
            rf = slice(t * BATCH, (t + 1) * BATCH)
            rb = slice((TT - 1 - t) * BATCH, (TT - t) * BATCH)
            af, bf = a_ref[0, rf, sl], b_ref[0, rf, sl]
            ab, bb = a_ref[1, rb, sl], b_ref[1, rb, sl]
            if t == 0:
                lf, pf, lb, pb = bf, af, bb, ab
            else:
                lf, pf = af * lf + bf, af * pf
                lb, pb = ab * lb + bb, ab * pb
                b_ref[0, rf, sl] = lf
                a_ref[0, rf, sl] = pf
                b_ref[1, rb, sl] = lb
                a_ref[1, rb, sl] = pb
        e_out[0, :, sl] = lf
        e_out[1, :, sl] = pf
        e_out[2, :, sl] = lb
        e_out[3, :, sl] = pb
        if lat:
            ls_out[:, sl] = (b_ref[0, :, sl] + b_ref[1, :, sl]).astype(BF16)
            pf_out[:, sl] = a_ref[0, :, sl].astype(BF16)
            pb_out[:, sl] = a_ref[1, :, sl].astype(BF16)
    assert not pieces


def _head_call(x, pos, mod, g, w, conv_w, conv_b, wg, b_a, b_x, lam, *, lat):
    seq = x.shape[1]
    n_chunks = seq // TT
    n_halo = seq // HALO_T
    per_chunk = TT // HALO_T

    def const(shape):
        return pl.BlockSpec(shape, lambda c: (0,) * len(shape))

    def prev_idx(c):
        return jnp.maximum(c * per_chunk - 1, 0)

    def next_idx(c):
        return jnp.minimum((c + 1) * per_chunk, n_halo - 1)

    in_specs = [
        pl.BlockSpec((BATCH, TT, D_MODEL), lambda c: (0, c, 0)),
        pl.BlockSpec((BATCH, HALO_T, D_MODEL), lambda c: (0, prev_idx(c), 0)),
        pl.BlockSpec((BATCH, HALO_T, D_MODEL), lambda c: (0, next_idx(c), 0)),
    ]
    args = [x, x, x]
    if lat:
        in_specs += [
            pl.BlockSpec((TT, D_MODEL), lambda c: (c, 0)),
            pl.BlockSpec((HALO_T, D_MODEL), lambda c: (prev_idx(c), 0)),
            pl.BlockSpec((HALO_T, D_MODEL), lambda c: (next_idx(c), 0)),
        ]
        args += [pos, pos, pos]
    in_specs += [
        _mod_spec(0 if lat else 1, 0), _mod_spec(0 if lat else 1, 1), const((1, D_MODEL)),
        const((D_MODEL, IN_COLS if lat else LRU_WIDTH)),
        const((CONV_WIDTH, LRU_WIDTH)), const((1, LRU_WIDTH)),
        const((2, LRU_HEADS, LRU_HEAD_DIM, 2 * LRU_HEAD_DIM)),
        const((2, LRU_WIDTH)), const((2, LRU_WIDTH)), const((2, LRU_WIDTH)),
    ]
    args += [mod, mod, g, w, conv_w, conv_b, wg, b_a, b_x, lam]

    def rows(width):
        return pl.BlockSpec((ROWS, width), lambda c: (c, 0))

    out_specs, out_shape = [], []
    if lat:
        n = n_chunks * ROWS
        out_specs += [rows(D_MODEL), rows(LRU_WIDTH),
                      pl.BlockSpec((TT, BATCH * FOURIER_WIDTH), lambda c: (c, 0)),
                      rows(2 * D_MODEL), rows(LRU_WIDTH), rows(LRU_WIDTH), rows(LRU_WIDTH)]
        out_shape += [
            jax.ShapeDtypeStruct((n, D_MODEL), F32),
            jax.ShapeDtypeStruct((n, LRU_WIDTH), BF16),
            jax.ShapeDtypeStruct((seq, BATCH * FOURIER_WIDTH), BF16),
            jax.ShapeDtypeStruct((n, 2 * D_MODEL), BF16),
            jax.ShapeDtypeStruct((n, LRU_WIDTH), BF16),
            jax.ShapeDtypeStruct((n, LRU_WIDTH), BF16),
            jax.ShapeDtypeStruct((n, LRU_WIDTH), BF16),
        ]
    out_specs.append(pl.BlockSpec((None, 4, BATCH, LRU_WIDTH), lambda c: (c, 0, 0, 0)))
    out_shape.append(jax.ShapeDtypeStruct((n_chunks, 4, BATCH, LRU_WIDTH), F32))

    return pl.pallas_call(
        functools.partial(_head_kernel, lat=lat, n_chunks=n_chunks),
        grid=(n_chunks,),
        in_specs=in_specs,
        out_specs=out_specs,
        out_shape=out_shape,
        scratch_shapes=[
            pltpu.VMEM((D_MODEL // LANES, BATCH * PITCH, LANES), F32),
            pltpu.VMEM((2, D_MODEL // LANES, BATCH * HALO_T, LANES), F32),
            pltpu.VMEM((XROWS_PAD, D_MODEL), F32),
            pltpu.VMEM((XROWS_PAD, D_MODEL), BF16),
            pltpu.VMEM((2, ROWS, LRU_WIDTH), F32),
            pltpu.VMEM((2, ROWS, LRU_WIDTH), F32),
            pltpu.VMEM((2, ROWS, LRU_HEAD_DIM), F32),
            pltpu.VMEM((2, 2, ROWS, 2 * LRU_HEAD_DIM), F32),
        ],
        name="head_lat" if lat else "head_ctx",
        compiler_params=pltpu.CompilerParams(
            dimension_semantics=("arbitrary",), vmem_limit_bytes=VMEM_LIMIT),
    )(*args)


def _carry_kernel(ec_ref, el_ref, hf_ref, hb_ref):
    n_ctx, n_lat = ec_ref.shape[0], el_ref.shape[0]
    h = jnp.zeros((BATCH, LRU_WIDTH), F32)
    for c in range(n_ctx):
        h = ec_ref[c, 1] * h + ec_ref[c, 0]
    for c in range(n_lat):
        hf_ref[c] = h
        h = el_ref[c, 1] * h + el_ref[c, 0]
    h = jnp.zeros((BATCH, LRU_WIDTH), F32)
    for c in reversed(range(n_ctx)):
        h = ec_ref[c, 3] * h + ec_ref[c, 2]
    for c in reversed(range(n_lat)):
        hb_ref[c] = h
        h = el_ref[c, 3] * h + el_ref[c, 2]


def _carry_call(e_ctx, e_lat):
    shape = jax.ShapeDtypeStruct((e_lat.shape[0], BATCH, LRU_WIDTH), F32)
    return pl.pallas_call(_carry_kernel, out_shape=[shape, shape], name="carry")(e_ctx, e_lat)


def _fourier_kernel(x_ref, chan_ref, time_ref, *refs):
    n_w = len(TAIL_WEIGHT_ROWS)
    w_f32_refs, o_ref, w_bf16_refs, z_ref = refs[:n_w], refs[n_w], refs[n_w + 1:2 * n_w + 1], refs[-1]
    for src, dst in zip(w_f32_refs, w_bf16_refs):
        dst[...] = src[...].astype(BF16)

    half = SEQ // 2
    for g in range(FOURIER_GROUPS):
        sl = slice(g * FOURIER_GROUP_DIM, (g + 1) * FOURIER_GROUP_DIM)
        r = _dot(x_ref[:, sl], chan_ref[...])
        lo, hi = r[:half], r[half:]
        for p, v in enumerate((lo + hi, lo - hi)):
            z_ref[p, 0:half, sl] = v[:, :FOURIER_GROUP_DIM].astype(BF16)
            z_ref[p, half:SEQ, sl] = v[:, FOURIER_GROUP_DIM:].astype(BF16)
    for p in range(2):
        for m0 in range(0, half, TM):
            o_ref[p, m0:m0 + TM, :] = _dot(time_ref[p, m0:m0 + TM, :], z_ref[p]).astype(BF16)


def _fourier_call(uf, chan, time, weights):
    def slab(w):
        return pl.BlockSpec((w.shape[0] // BATCH, w.shape[1]), lambda n: (n, 0))

    assert tuple(w.shape[0] for w in weights) == TAIL_WEIGHT_ROWS
    return pl.pallas_call(
        _fourier_kernel,
        grid=(BATCH,),
        in_specs=[
            pl.BlockSpec((SEQ, FOURIER_WIDTH), lambda n: (0, n)),
            pl.BlockSpec((FOURIER_GROUP_DIM, 2 * FOURIER_GROUP_DIM), lambda n: (0, 0)),
            pl.BlockSpec((2, SEQ // 2, SEQ), lambda n: (0, 0, 0)),
        ] + [slab(w) for w in weights],
        out_specs=[pl.BlockSpec((2, SEQ // 2, FOURIER_WIDTH), lambda n: (0, 0, n))]
        + [slab(w) for w in weights],
        out_shape=[jax.ShapeDtypeStruct((2, SEQ // 2, BATCH * FOURIER_WIDTH), BF16)]
        + [jax.ShapeDtypeStruct(w.shape, BF16) for w in weights],
        scratch_shapes=[pltpu.VMEM((2, SEQ, FOURIER_WIDTH), BF16)],
        name="fourier",
        compiler_params=pltpu.CompilerParams(
            dimension_semantics=("arbitrary",), vmem_limit_bytes=VMEM_LIMIT),
    )(uf, chan, time, *weights)


def _tail_kernel(ls_ref, pf_ref, pb_ref, hf_ref, hb_ref, uy_ref, ug_ref, yf_ref, xt_ref,
                 gt1_ref, sh2_ref, sc2_ref, gt2_ref, gm_ref, gf_ref,
                 wl_ref, wf_ref, wo_ref, w1_ref, w2_ref, o_ref, fslab_ref, oslab_ref):
    def tb(v):
        return v.reshape(TT, BATCH, v.shape[-1])

    def flat(v):
        return v.reshape(ROWS, v.shape[-1])

    hf = jnp.tile(hf_ref[...].astype(BF16), (TT, 1))
    hb = jnp.tile(hb_ref[...].astype(BF16), (TT, 1))
    y_lru = ls_ref[...] + pf_ref[...] * hf + pb_ref[...] * hb
    z = y_lru * _gelu_tanh(uy_ref[...])
    y_a = _dot(z, wl_ref[...])

    for b in range(BATCH):
        for p in range(2):
            v = yf_ref[p, :, b * FOURIER_WIDTH:(b + 1) * FOURIER_WIDTH].astype(F32)
            for s in range(FOURIER_WIDTH // LANES):
                fslab_ref[s, pl.ds(p * BATCH + b, TT // 2, stride=2 * BATCH), :] = (
                    v[:, s * LANES:(s + 1) * LANES])
    yf = jnp.concatenate([fslab_ref[s] for s in range(FOURIER_WIDTH // LANES)], axis=-1).astype(BF16)
    y_b = _dot(yf, wf_ref[...])

    g_a = _sigmoid(ug_ref[:, :D_MODEL].astype(F32))
    g_b = _sigmoid(ug_ref[:, D_MODEL:].astype(F32))
    merged = (g_a * y_a + g_b * y_b).astype(BF16)
    x2 = tb(xt_ref[...]) + gt1_ref[...][None] * tb(_dot(merged, wo_ref[...]))

    h = (tb(_rms(flat(x2), gm_ref[...])) * (1.0 + sc2_ref[...][None]) + sh2_ref[...][None])
    h = flat(h).astype(BF16)
    acc = jnp.zeros((ROWS, D_MODEL), F32)
    for k0 in range(0, D_FF, D_MODEL):
        a = jnp.maximum(_dot(h, w1_ref[:, k0:k0 + D_MODEL]), 0.0)
        acc = acc + _dot((a * a).astype(BF16), w2_ref[k0:k0 + D_MODEL, :])
    out = _rms(flat(x2 + gt2_ref[...][None] * tb(acc)), gf_ref[...])

    for j in range(D_MODEL // LANES):
        oslab_ref[j] = out[:, j * LANES:(j + 1) * LANES]
    for b in range(BATCH):
        for j in range(D_MODEL // LANES):
            o_ref[b, :, j * LANES:(j + 1) * LANES] = oslab_ref[j, pl.ds(b, TT, stride=BATCH), :]


def _tail_call(ls, pf, pb, hf, hb, uy, ug, yf, xt, mod, gm, gf, wl, wf, wo, w1, w2):
    def rows(width):
        return pl.BlockSpec((ROWS, width), lambda c: (c, 0))

    def const(shape):
        return pl.BlockSpec(shape, lambda c: (0,) * len(shape))

    state = pl.BlockSpec((None, BATCH, LRU_WIDTH), lambda c: (c, 0, 0))
    vec1 = const((1, D_MODEL))
    return pl.pallas_call(
        _tail_kernel,
        grid=(N_LAT_CHUNKS,),
        in_specs=[
            rows(LRU_WIDTH), rows(LRU_WIDTH), rows(LRU_WIDTH), state, state,
            rows(LRU_WIDTH), rows(2 * D_MODEL),
            pl.BlockSpec((2, TT // 2, BATCH * FOURIER_WIDTH), lambda c: (0, c, 0)),
            rows(D_MODEL),
            _mod_spec(0, 2), _mod_spec(0, 3), _mod_spec(0, 4), _mod_spec(0, 5), vec1, vec1,
            const((LRU_WIDTH, D_MODEL)), const((FOURIER_WIDTH, D_MODEL)), const((D_MODEL, D_MODEL)),
            const((D_MODEL, D_FF)), const((D_FF, D_MODEL)),
        ],
        out_specs=pl.BlockSpec((BATCH, TT, D_MODEL), lambda c: (0, c, 0)),
        out_shape=jax.ShapeDtypeStruct((BATCH, SEQ, D_MODEL), F32),
        scratch_shapes=[
            pltpu.VMEM((FOURIER_WIDTH // LANES, ROWS, LANES), F32),
            pltpu.VMEM((D_MODEL // LANES, ROWS, LANES), F32),
        ],
        name="tail",
        compiler_params=pltpu.CompilerParams(
            dimension_semantics=("arbitrary",), vmem_limit_bytes=VMEM_LIMIT),
    )(ls, pf, pb, hf, hb, uy, ug, yf, xt, mod, mod, mod, mod, gm, gf, wl, wf, wo, w1, w2)


def kernel(x, c, ctx, c_ctx, w_mod, b_mod, g_mix, w_in, conv_w, conv_b, w_a, b_a, w_x, b_x,
           lam, w_lru_out, w_f_out, w_out, g_mlp, w1, w2, g_final):
    pos_np, chan_np, time_np = _constants()
    pos = jnp.asarray(pos_np)
    chan, time = jnp.asarray(chan_np).astype(BF16), jnp.asarray(time_np).astype(BF16)

    cc = jnp.concatenate(
        [c, c_ctx[None], jnp.zeros((2 * SUBLANES - BATCH - 1, D_MODEL), F32)], axis=0)
    mod = _mod_call(cc, w_mod[0], b_mod[0][None])

    w_in_b = w_in[0].astype(BF16)
    g_mix2 = g_mix[0][None]
    wg = jnp.concatenate([w_a[0], w_x[0]], axis=-1).astype(BF16)
    lru = (conv_w[0], conv_b[0][None], wg, b_a[0], b_x[0], lam[0])

    (e_ctx,) = _head_call(ctx, None, mod, g_mix2, w_in_b, *lru, lat=False)
    xt, uy, uf, ug, ls, pf, pb, e_lat = _head_call(x, pos, mod, g_mix2, w_in_b, *lru, lat=True)
    hf, hb = _carry_call(e_ctx, e_lat)
    yf, wl, wf, wo, w1b, w2b = _fourier_call(
        uf, chan, time, (w_lru_out[0], w_f_out[0], w_out[0], w1[0], w2[0]))
    return _tail_call(ls, pf, pb, hf, hb, uy, ug, yf, xt, mod,
                      g_mlp[0][None], g_final[None], wl, wf, wo, w1b, w2b)
```

```python
import functools
import math

import numpy as np
import jax
import jax.numpy as jnp
from jax import lax
from jax.experimental import pallas as pl
from jax.experimental.pallas import tpu as pltpu

F32 = jnp.float32
BF16 = jnp.bfloat16

D_MODEL = 1024
BATCH = 8
SEQ = 2048
CTX_LEN = 256
GRID_W = 64
LRU_WIDTH = 1024
LRU_HEADS = 8
LRU_HEAD_DIM = LRU_WIDTH // LRU_HEADS
LRU_C = 8.0
CONV_WIDTH = 4
CONV_PAD_LEFT = 2
FOURIER_WIDTH = 512
FOURIER_GROUPS = 4
FOURIER_GROUP_DIM = FOURIER_WIDTH // FOURIER_GROUPS
IN_COLS = 2 * LRU_WIDTH + FOURIER_WIDTH + 2 * D_MODEL
D_FF = 4 * D_MODEL
N_MOD = 6
EPS = 1e-6
POS_MAX_PERIOD = 10000.0

LANES = 128
SUBLANES = 8
BF16_ROWS = 16
assert BATCH == SUBLANES

TT = 64
ROWS = TT * BATCH
PITCH = TT + SUBLANES
HALO_T = SUBLANES
LEAD = CONV_PAD_LEFT * BATCH
TRAIL = (CONV_WIDTH - 1 - CONV_PAD_LEFT) * BATCH
XROWS = LEAD + ROWS + TRAIL
XROWS_PAD = -(-XROWS // BF16_ROWS) * BF16_ROWS
N_CTX_CHUNKS = CTX_LEN // TT
N_LAT_CHUNKS = SEQ // TT
NCHUNK = 512
FRONT_ROWS = 128
TM = 512
TAIL_WEIGHT_ROWS = (LRU_WIDTH, FOURIER_WIDTH, D_MODEL, D_MODEL, D_FF)
MOD_TN = 1024
VMEM_LIMIT = 58 * 1024 * 1024
LOG2E = math.log2(math.e)
TINY = 1e-30


def _gelu_tanh(x):
    k0 = jnp.full((1, 1), math.sqrt(2.0 / math.pi), F32).astype(x.dtype)
    k1 = jnp.full((1, 1), 0.044715, F32).astype(x.dtype)
    return 0.5 * x * (1.0 + jnp.tanh(k0 * (x + k1 * (x * x * x))))


def _sigmoid(x):
    return 0.5 * jnp.tanh(0.5 * x) + 0.5


def _dot(a, b):
    return jnp.dot(a, b, preferred_element_type=F32)


def _rms(x, g):
    ms = jnp.mean(x * x, axis=-1, keepdims=True)
    return x * lax.rsqrt(ms + EPS) * g


@functools.cache
def _constants():
    half = D_MODEL // 4
    freqs = np.exp(-math.log(POS_MAX_PERIOD) * np.arange(half, dtype=np.float32) / half).astype(np.float32)

    def sincos(n):
        ang = np.arange(n, dtype=np.float32)[:, None] * freqs[None, :]
        return np.concatenate([np.sin(ang), np.cos(ang)], axis=-1).astype(np.float32)

    rows = SEQ // GRID_W
    er, ec = sincos(rows), sincos(GRID_W)
    pos = np.concatenate([
        np.broadcast_to(er[:, None, :], (rows, GRID_W, D_MODEL // 2)),
        np.broadcast_to(ec[None, :, :], (rows, GRID_W, D_MODEL // 2)),
    ], axis=-1).reshape(SEQ, D_MODEL).astype(np.float32)

    def dft(n):
        k = np.arange(n, dtype=np.int64)
        ang = 2.0 * np.pi * ((k[:, None] * k[None, :]) % n).astype(np.float64) / n
        return np.cos(ang) / math.sqrt(n), np.sin(ang) / math.sqrt(n)

    cc, sc = dft(FOURIER_GROUP_DIM)
    chan = np.concatenate([cc, sc], axis=1).astype(np.float32)
    ct, st = dft(SEQ)
    time = np.stack([
        np.concatenate([ct[p::2, :SEQ // 2], -st[p::2, :SEQ // 2]], axis=1) for p in range(2)
    ]).astype(np.float32)
    return pos, chan, time


def _mod_kernel(c_ref, w_ref, b_ref, o_ref):
    c = c_ref[...]
    a = c * _sigmoid(c)
    w = w_ref[...]
    a_hi = a.astype(BF16)
    a_lo = (a - a_hi.astype(F32)).astype(BF16)
    w_hi = w.astype(BF16)
    w_lo = (w - w_hi.astype(F32)).astype(BF16)
    o_ref[...] = _dot(a_hi, w_hi) + _dot(a_hi, w_lo) + _dot(a_lo, w_hi) + b_ref[...]


def _mod_call(cc, w_mod, b_mod):
    n = w_mod.shape[1]
    return pl.pallas_call(
        _mod_kernel,
        grid=(n // MOD_TN,),
        in_specs=[
            pl.BlockSpec((2 * SUBLANES, D_MODEL), lambda i: (0, 0)),
            pl.BlockSpec((D_MODEL, MOD_TN), lambda i: (0, i)),
            pl.BlockSpec((1, MOD_TN), lambda i: (0, i)),
        ],
        out_specs=pl.BlockSpec((2 * SUBLANES, MOD_TN), lambda i: (0, i)),
        out_shape=jax.ShapeDtypeStruct((2 * SUBLANES, n), F32),
        name="mod",
        compiler_params=pltpu.CompilerParams(dimension_semantics=("arbitrary",)),
    )(cc, w_mod, b_mod)


def _mod_spec(row_block, k):
    return pl.BlockSpec((BATCH, D_MODEL), lambda c: (row_block, k))


def _head_kernel(*refs, lat, n_chunks):
    refs = list(refs)
    xm_ref, xp_ref, xn_ref = refs[:3]
    refs = refs[3:]
    if lat:
        pm_ref, pp_ref, pn_ref = refs[:3]
        refs = refs[3:]
    (sh_ref, sc_ref, g_ref, w_ref, cw_ref, cb_ref, wg_ref, ba_ref, bx_ref, lam_ref) = refs[:10]
    refs = refs[10:]
    if lat:
        xt_out, uy_out, uf_out, ug_out, ls_out, pf_out, pb_out = refs[:7]
        refs = refs[7:]
    else:
        w_slab_ref, w_bf16_out = refs[:2]
        refs = refs[2:]
        w_bf16_out[...] = w_slab_ref[...].astype(BF16)
    e_out, slab_ref, hslab_ref, xs_ref, hb_ref, a_ref, b_ref, xh_ref, gate_ref = refs

    c = pl.program_id(0)
    n_slab = D_MODEL // LANES
    mod_rows = slice(None) if lat else slice(0, 1)

    xm = xm_ref[...]
    xp = xp_ref[...]
    xn = xn_ref[...]
    if lat:
        xm = xm + pm_ref[...][None]
        xp = xp + pp_ref[...][None]
        xn = xn + pn_ref[...][None]
    for b in range(BATCH):
        for j in range(n_slab):
            ls = slice(j * LANES, (j + 1) * LANES)
            slab_ref[j, b * PITCH:b * PITCH + TT, :] = xm[b, :, ls]
            hslab_ref[0, j, b * HALO_T:(b + 1) * HALO_T, :] = xp[b, :, ls]
            hslab_ref[1, j, b * HALO_T:(b + 1) * HALO_T, :] = xn[b, :, ls]
    for j in range(n_slab):
        ls = slice(j * LANES, (j + 1) * LANES)
        for k in range(CONV_PAD_LEFT):
            xs_ref[k * BATCH:(k + 1) * BATCH, ls] = (
                hslab_ref[0, j, pl.ds(HALO_T - CONV_PAD_LEFT + k, BATCH, stride=HALO_T), :])
        for t in range(TT):
            xs_ref[LEAD + t * BATCH:LEAD + (t + 1) * BATCH, ls] = slab_ref[j, pl.ds(t, BATCH, stride=PITCH), :]
        for k in range(CONV_WIDTH - 1 - CONV_PAD_LEFT):
            r0 = LEAD + ROWS + k * BATCH
            xs_ref[r0:r0 + BATCH, ls] = hslab_ref[1, j, pl.ds(k, BATCH, stride=HALO_T), :]
    xs_ref[XROWS:XROWS_PAD, :] = jnp.zeros((XROWS_PAD - XROWS, D_MODEL), F32)

    assert LEAD % BF16_ROWS == 0 and FRONT_ROWS % BF16_ROWS == 0
    blocks = ([(0, LEAD)] + [(r, r + FRONT_ROWS) for r in range(LEAD, LEAD + ROWS, FRONT_ROWS)]
              + [(LEAD + ROWS, XROWS_PAD)])
    for r0, r1 in blocks:
        xt = xs_ref[r0:r1, :]
        if lat and LEAD <= r0 < LEAD + ROWS:
            xt_out[r0 - LEAD:r1 - LEAD, :] = xt
        h = _rms(xt, g_ref[...]).reshape((r1 - r0) // BATCH, BATCH, D_MODEL)
        h = h * (1.0 + sc_ref[mod_rows, :][None]) + sh_ref[mod_rows, :][None]
        hb_ref[r0:r1, :] = h.reshape(r1 - r0, D_MODEL).astype(BF16)

    keep_first = jnp.where(c == 0, 0.0, 1.0)
    keep_last = jnp.where(c == n_chunks - 1, 0.0, 1.0)
    for n0 in range(0, LRU_WIDTH, NCHUNK):
        w_x = w_ref[:, n0:n0 + NCHUNK]
        xs_ref[:, n0:n0 + NCHUNK] = _dot(hb_ref[...], w_x if lat else w_x.astype(BF16))
    xs_ref[0:LEAD, :] = xs_ref[0:LEAD, :] * keep_first
    xs_ref[LEAD + ROWS:XROWS, :] = xs_ref[LEAD + ROWS:XROWS, :] * keep_last

    def branch_piece(n0):
        r = _dot(hb_ref[LEAD:LEAD + ROWS, :], w_ref[:, n0:n0 + NCHUNK])
        if n0 < 2 * LRU_WIDTH:
            uy_out[:, n0 - LRU_WIDTH:n0 - LRU_WIDTH + NCHUNK] = r.astype(BF16)
        elif n0 < 2 * LRU_WIDTH + FOURIER_WIDTH:
            for s in range(NCHUNK // LANES):
                slab_ref[s, 0:ROWS, :] = r[:, s * LANES:(s + 1) * LANES]
            for b in range(BATCH):
                for s in range(NCHUNK // LANES):
                    c0 = b * FOURIER_WIDTH + (n0 - 2 * LRU_WIDTH) + s * LANES
                    uf_out[:, c0:c0 + LANES] = slab_ref[s, pl.ds(b, TT, stride=BATCH), :].astype(BF16)
        else:
            c0 = n0 - 2 * LRU_WIDTH - FOURIER_WIDTH
            ug_out[:, c0:c0 + NCHUNK] = r.astype(BF16)

    pieces = list(range(LRU_WIDTH, IN_COLS, NCHUNK)) if lat else []
    per_head = -(-len(pieces) // LRU_HEADS)

    lam = lam_ref[...]
    sp = jnp.maximum(-lam, 0.0) + jnp.log1p(jnp.exp(-jnp.abs(lam)))
    c1 = (-0.5 * LRU_C * LOG2E) * sp

    def conv_gates(hd):
        sl = slice(hd * LRU_HEAD_DIM, (hd + 1) * LRU_HEAD_DIM)
        xh = 0.5 * cb_ref[:, sl]
        for k in range(CONV_WIDTH):
            xh = xh + (0.5 * cw_ref[k:k + 1, sl]) * xs_ref[k * BATCH:k * BATCH + ROWS, sl]
        xh_ref[hd % 2] = xh
        xhb = xh.astype(BF16)
        for d in range(2):
            gate_ref[hd % 2, d] = _dot(xhb, wg_ref[d, hd])

    conv_gates(0)
    for hd in range(LRU_HEADS):
        for n0 in pieces[hd * per_head:(hd + 1) * per_head]:
            branch_piece(n0)
        if hd + 1 < LRU_HEADS:
            conv_gates(hd + 1)
        sl = slice(hd * LRU_HEAD_DIM, (hd + 1) * LRU_HEAD_DIM)
        xh = xh_ref[hd % 2]
        for d in range(2):
            g = gate_ref[hd % 2, d]
            ta = jnp.tanh(g[:, :LRU_HEAD_DIM] + 0.5 * ba_ref[d:d + 1, sl])
            ti = jnp.tanh(g[:, LRU_HEAD_DIM:] + 0.5 * bx_ref[d:d + 1, sl])
            a = jnp.exp2(c1[d:d + 1, sl] * ta + c1[d:d + 1, sl])
            v = 1.0 - a * a
            m = v * lax.rsqrt(jnp.maximum(v, TINY))
            a_ref[d, :, sl] = a
            b_ref[d, :, sl] = (m * xh) * (ti + 1.0)

        lf = pf = lb = pb = None
        for t in range(TT):
            rf = slice(t * BATCH, (t + 1) * BATCH)
            rb = slice((TT - 1 - t) * BATCH, (TT - t) * BATCH)
            af, bf = a_ref[0, rf, sl], b_ref[0, rf, sl]
            ab, bb = a_ref[1, rb, sl], b_ref[1, rb, sl]
            if t == 0:
                lf, pf, lb, pb = bf, af, bb, ab
            else:
                lf, pf = af * lf + bf, af * pf
                lb, pb = ab * lb + bb, ab * pb
                b_ref[0, rf, sl] = lf
                a_ref[0, rf, sl] = pf
                b_ref[1, rb, sl] = lb
                a_ref[1, rb, sl] = pb
        e_out[0, :, sl] = lf
        e_out[1, :, sl] = pf
        e_out[2, :, sl] = lb
        e_out[3, :, sl] = pb
        if lat:
            ls_out[:, sl] = (b_ref[0, :, sl] + b_ref[1, :, sl]).astype(BF16)
            pf_out[:, sl] = a_ref[0, :, sl].astype(BF16)
            pb_out[:, sl] = a_ref[1, :, sl].astype(BF16)


def _head_call(x, pos, mod, g, w, conv_w, conv_b, wg, b_a, b_x, lam, *, lat):
    seq = x.shape[1]
    n_chunks = seq // TT
    n_halo = seq // HALO_T
    per_chunk = TT // HALO_T

    def const(shape):
        return pl.BlockSpec(shape, lambda c: (0,) * len(shape))

    def prev_idx(c):
        return jnp.maximum(c * per_chunk - 1, 0)

    def next_idx(c):
        return jnp.minimum((c + 1) * per_chunk, n_halo - 1)

    in_specs = [
        pl.BlockSpec((BATCH, TT, D_MODEL), lambda c: (0, c, 0)),
        pl.BlockSpec((BATCH, HALO_T, D_MODEL), lambda c: (0, prev_idx(c), 0)),
        pl.BlockSpec((BATCH, HALO_T, D_MODEL), lambda c: (0, next_idx(c), 0)),
    ]
    args = [x, x, x]
    if lat:
        in_specs += [
            pl.BlockSpec((TT, D_MODEL), lambda c: (c, 0)),
            pl.BlockSpec((HALO_T, D_MODEL), lambda c: (prev_idx(c), 0)),
            pl.BlockSpec((HALO_T, D_MODEL), lambda c: (next_idx(c), 0)),
        ]
        args += [pos, pos, pos]
    in_specs += [
        _mod_spec(0 if lat else 1, 0), _mod_spec(0 if lat else 1, 1), const((1, D_MODEL)),
        const((D_MODEL, IN_COLS if lat else LRU_WIDTH)),
        const((CONV_WIDTH, LRU_WIDTH)), const((1, LRU_WIDTH)),
        const((2, LRU_HEADS, LRU_HEAD_DIM, 2 * LRU_HEAD_DIM)),
        const((2, LRU_WIDTH)), const((2, LRU_WIDTH)), const((2, LRU_WIDTH)),
    ]
    args += [mod, mod, g, w, conv_w, conv_b, wg, b_a, b_x, lam]
    w_slab = pl.BlockSpec((D_MODEL // n_chunks, IN_COLS), lambda c: (c, 0))
    if not lat:
        in_specs.append(w_slab)
        args.append(w)

    def rows(width):
        return pl.BlockSpec((ROWS, width), lambda c: (c, 0))

    out_specs, out_shape = [], []
    if lat:
        n = n_chunks * ROWS
        out_specs += [rows(D_MODEL), rows(LRU_WIDTH),
                      pl.BlockSpec((TT, BATCH * FOURIER_WIDTH), lambda c: (c, 0)),
                      rows(2 * D_MODEL), rows(LRU_WIDTH), rows(LRU_WIDTH), rows(LRU_WIDTH)]
        out_shape += [
            jax.ShapeDtypeStruct((n, D_MODEL), F32),
            jax.ShapeDtypeStruct((n, LRU_WIDTH), BF16),
            jax.ShapeDtypeStruct((seq, BATCH * FOURIER_WIDTH), BF16),
            jax.ShapeDtypeStruct((n, 2 * D_MODEL), BF16),
            jax.ShapeDtypeStruct((n, LRU_WIDTH), BF16),
            jax.ShapeDtypeStruct((n, LRU_WIDTH), BF16),
            jax.ShapeDtypeStruct((n, LRU_WIDTH), BF16),
        ]
    else:
        out_specs.append(w_slab)
        out_shape.append(jax.ShapeDtypeStruct((D_MODEL, IN_COLS), BF16))
    out_specs.append(pl.BlockSpec((None, 4, BATCH, LRU_WIDTH), lambda c: (c, 0, 0, 0)))
    out_shape.append(jax.ShapeDtypeStruct((n_chunks, 4, BATCH, LRU_WIDTH), F32))

    return pl.pallas_call(
        functools.partial(_head_kernel, lat=lat, n_chunks=n_chunks),
        grid=(n_chunks,),
        in_specs=in_specs,
        out_specs=out_specs,
        out_shape=out_shape,
        scratch_shapes=[
            pltpu.VMEM((D_MODEL // LANES, BATCH * PITCH, LANES), F32),
            pltpu.VMEM((2, D_MODEL // LANES, BATCH * HALO_T, LANES), F32),
            pltpu.VMEM((XROWS_PAD, D_MODEL), F32),
            pltpu.VMEM((XROWS_PAD, D_MODEL), BF16),
            pltpu.VMEM((2, ROWS, LRU_WIDTH), F32),
            pltpu.VMEM((2, ROWS, LRU_WIDTH), F32),
            pltpu.VMEM((2, ROWS, LRU_HEAD_DIM), F32),
            pltpu.VMEM((2, 2, ROWS, 2 * LRU_HEAD_DIM), F32),
        ],
        name="head_lat" if lat else "head_ctx",
        compiler_params=pltpu.CompilerParams(
            dimension_semantics=("arbitrary",), vmem_limit_bytes=VMEM_LIMIT),
    )(*args)


def _carry_kernel(ec_ref, el_ref, hf_ref, hb_ref):
    n_ctx, n_lat = ec_ref.shape[0], el_ref.shape[0]
    h = jnp.zeros((BATCH, LRU_WIDTH), F32)
    for c in range(n_ctx):
        h = ec_ref[c, 1] * h + ec_ref[c, 0]
    for c in range(n_lat):
        hf_ref[c] = h
        h = el_ref[c, 1] * h + el_ref[c, 0]
    h = jnp.zeros((BATCH, LRU_WIDTH), F32)
    for c in reversed(range(n_ctx)):
        h = ec_ref[c, 3] * h + ec_ref[c, 2]
    for c in reversed(range(n_lat)):
        hb_ref[c] = h
        h = el_ref[c, 3] * h + el_ref[c, 2]


def _carry_call(e_ctx, e_lat):
    shape = jax.ShapeDtypeStruct((e_lat.shape[0], BATCH, LRU_WIDTH), F32)
    return pl.pallas_call(_carry_kernel, out_shape=[shape, shape], name="carry")(e_ctx, e_lat)


def _fourier_kernel(x_ref, chan_ref, time_ref, *refs):
    n_w = len(TAIL_WEIGHT_ROWS)
    w_f32_refs, o_ref, w_bf16_refs, z_ref = refs[:n_w], refs[n_w], refs[n_w + 1:2 * n_w + 1], refs[-1]
    for src, dst in zip(w_f32_refs, w_bf16_refs):
        dst[...] = src[...].astype(BF16)

    half = SEQ // 2
    for g in range(FOURIER_GROUPS):
        sl = slice(g * FOURIER_GROUP_DIM, (g + 1) * FOURIER_GROUP_DIM)
        r = _dot(x_ref[:, sl], chan_ref[...])
        lo, hi = r[:half], r[half:]
        for p, v in enumerate((lo + hi, lo - hi)):
            z_ref[p, 0:half, sl] = v[:, :FOURIER_GROUP_DIM].astype(BF16)
            z_ref[p, half:SEQ, sl] = v[:, FOURIER_GROUP_DIM:].astype(BF16)
    for p in range(2):
        for m0 in range(0, half, TM):
            o_ref[p, m0:m0 + TM, :] = _dot(time_ref[p, m0:m0 + TM, :], z_ref[p]).astype(BF16)


def _fourier_call(uf, chan, time, weights):
    def slab(w):
        return pl.BlockSpec((w.shape[0] // BATCH, w.shape[1]), lambda n: (n, 0))

    assert tuple(w.shape[0] for w in weights) == TAIL_WEIGHT_ROWS
    return pl.pallas_call(
        _fourier_kernel,
        grid=(BATCH,),
        in_specs=[
            pl.BlockSpec((SEQ, FOURIER_WIDTH), lambda n: (0, n)),
            pl.BlockSpec((FOURIER_GROUP_DIM, 2 * FOURIER_GROUP_DIM), lambda n: (0, 0)),
            pl.BlockSpec((2, SEQ // 2, SEQ), lambda n: (0, 0, 0)),
        ] + [slab(w) for w in weights],
        out_specs=[pl.BlockSpec((2, SEQ // 2, FOURIER_WIDTH), lambda n: (0, 0, n))]
        + [slab(w) for w in weights],
        out_shape=[jax.ShapeDtypeStruct((2, SEQ // 2, BATCH * FOURIER_WIDTH), BF16)]
        + [jax.ShapeDtypeStruct(w.shape, BF16) for w in weights],
        scratch_shapes=[pltpu.VMEM((2, SEQ, FOURIER_WIDTH), BF16)],
        name="fourier",
        compiler_params=pltpu.CompilerParams(
            dimension_semantics=("arbitrary",), vmem_limit_bytes=VMEM_LIMIT),
    )(uf, chan, time, *weights)


def _tail_kernel(ls_ref, pf_ref, pb_ref, hf_ref, hb_ref, uy_ref, ug_ref, yf_ref, xt_ref,
                 gt1_ref, sh2_ref, sc2_ref, gt2_ref, gm_ref, gf_ref,
                 wl_ref, wf_ref, wo_ref, w1_ref, w2_ref, o_ref, fslab_ref, oslab_ref):
    def tb(v):
        return v.reshape(TT, BATCH, v.shape[-1])

    def flat(v):
        return v.reshape(ROWS, v.shape[-1])

    hf = jnp.tile(hf_ref[...].astype(BF16), (TT, 1))
    hb = jnp.tile(hb_ref[...].astype(BF16), (TT, 1))
    y_lru = ls_ref[...] + pf_ref[...] * hf + pb_ref[...] * hb
    z = y_lru * _gelu_tanh(uy_ref[...])
    y_a = _dot(z, wl_ref[...])

    for b in range(BATCH):
        for p in range(2):
            v = yf_ref[p, :, b * FOURIER_WIDTH:(b + 1) * FOURIER_WIDTH].astype(F32)
            for s in range(FOURIER_WIDTH // LANES):
                fslab_ref[s, pl.ds(p * BATCH + b, TT // 2, stride=2 * BATCH), :] = (
                    v[:, s * LANES:(s + 1) * LANES])
    yf = jnp.concatenate([fslab_ref[s] for s in range(FOURIER_WIDTH // LANES)], axis=-1).astype(BF16)
    y_b = _dot(yf, wf_ref[...])

    g_a = _sigmoid(ug_ref[:, :D_MODEL].astype(F32))
    g_b = _sigmoid(ug_ref[:, D_MODEL:].astype(F32))
    merged = (g_a * y_a + g_b * y_b).astype(BF16)
    x2 = tb(xt_ref[...]) + gt1_ref[...][None] * tb(_dot(merged, wo_ref[...]))

    h = (tb(_rms(flat(x2), gm_ref[...])) * (1.0 + sc2_ref[...][None]) + sh2_ref[...][None])
    h = flat(h).astype(BF16)
    acc = jnp.zeros((ROWS, D_MODEL), F32)
    for k0 in range(0, D_FF, D_MODEL):
        a = jnp.maximum(_dot(h, w1_ref[:, k0:k0 + D_MODEL]), 0.0)
        acc = acc + _dot((a * a).astype(BF16), w2_ref[k0:k0 + D_MODEL, :])
    out = _rms(flat(x2 + gt2_ref[...][None] * tb(acc)), gf_ref[...])

    for j in range(D_MODEL // LANES):
        oslab_ref[j] = out[:, j * LANES:(j + 1) * LANES]
    for b in range(BATCH):
        for j in range(D_MODEL // LANES):
            o_ref[b, :, j * LANES:(j + 1) * LANES] = oslab_ref[j, pl.ds(b, TT, stride=BATCH), :]


def _tail_call(ls, pf, pb, hf, hb, uy, ug, yf, xt, mod, gm, gf, wl, wf, wo, w1, w2):
    def rows(width):
        return pl.BlockSpec((ROWS, width), lambda c: (c, 0))

    def const(shape):
        return pl.BlockSpec(shape, lambda c: (0,) * len(shape))

    state = pl.BlockSpec((None, BATCH, LRU_WIDTH), lambda c: (c, 0, 0))
    vec1 = const((1, D_MODEL))
    return pl.pallas_call(
        _tail_kernel,
        grid=(N_LAT_CHUNKS,),
        in_specs=[
            rows(LRU_WIDTH), rows(LRU_WIDTH), rows(LRU_WIDTH), state, state,
            rows(LRU_WIDTH), rows(2 * D_MODEL),
            pl.BlockSpec((2, TT // 2, BATCH * FOURIER_WIDTH), lambda c: (0, c, 0)),
            rows(D_MODEL),
            _mod_spec(0, 2), _mod_spec(0, 3), _mod_spec(0, 4), _mod_spec(0, 5), vec1, vec1,
            const((LRU_WIDTH, D_MODEL)), const((FOURIER_WIDTH, D_MODEL)), const((D_MODEL, D_MODEL)),
            const((D_MODEL, D_FF)), const((D_FF, D_MODEL)),
        ],
        out_specs=pl.BlockSpec((BATCH, TT, D_MODEL), lambda c: (0, c, 0)),
        out_shape=jax.ShapeDtypeStruct((BATCH, SEQ, D_MODEL), F32),
        scratch_shapes=[
            pltpu.VMEM((FOURIER_WIDTH // LANES, ROWS, LANES), F32),
            pltpu.VMEM((D_MODEL // LANES, ROWS, LANES), F32),
        ],
        name="tail",
        compiler_params=pltpu.CompilerParams(
            dimension_semantics=("arbitrary",), vmem_limit_bytes=VMEM_LIMIT),
    )(ls, pf, pb, hf, hb, uy, ug, yf, xt, mod, mod, mod, mod, gm, gf, wl, wf, wo, w1, w2)


def kernel(x, c, ctx, c_ctx, w_mod, b_mod, g_mix, w_in, conv_w, conv_b, w_a, b_a, w_x, b_x,
           lam, w_lru_out, w_f_out, w_out, g_mlp, w1, w2, g_final):
    pos_np, chan_np, time_np = _constants()
    pos = jnp.asarray(pos_np)
    chan, time = jnp.asarray(chan_np).astype(BF16), jnp.asarray(time_np).astype(BF16)

    cc = jnp.concatenate(
        [c, c_ctx[None], jnp.zeros((2 * SUBLANES - BATCH - 1, D_MODEL), F32)], axis=0)
    mod = _mod_call(cc, w_mod[0], b_mod[0][None])

    g_mix2 = g_mix[0][None]
    wg = jnp.concatenate([w_a[0], w_x[0]], axis=-1).astype(BF16)
    lru = (conv_w[0], conv_b[0][None], wg, b_a[0], b_x[0], lam[0])

    w_in_b, e_ctx = _head_call(ctx, None, mod, g_mix2, w_in[0], *lru, lat=False)
    xt, uy, uf, ug, ls, pf, pb, e_lat = _head_call(x, pos, mod, g_mix2, w_in_b, *lru, lat=True)
    hf, hb = _carry_call(e_ctx, e_lat)
    yf, wl, wf, wo, w1b, w2b = _fourier_call(
        uf, chan, time, (w_lru_out[0], w_f_out[0], w_out[0], w1[0], w2[0]))
    return _tail_call(ls, pf, pb, hf, hb, uy, ug, yf, xt, mod,
                      g_mlp[0][None], g_final[None], wl, wf, wo, w1b, w2b)
```

```python
import functools
import math

import numpy as np
import jax
import jax.numpy as jnp
from jax import lax
from jax.experimental import pallas as pl
from jax.experimental.pallas import tpu as pltpu

F32 = jnp.float32
BF16 = jnp.bfloat16

D_MODEL = 1024
BATCH = 8
SEQ = 2048
CTX_LEN = 256
GRID_W = 64
LRU_WIDTH = 1024
LRU_HEADS = 8
LRU_HEAD_DIM = LRU_WIDTH // LRU_HEADS
LRU_C = 8.0
CONV_WIDTH = 4
CONV_PAD_LEFT = 2
FOURIER_WIDTH = 512
FOURIER_GROUPS = 4
FOURIER_GROUP_DIM = FOURIER_WIDTH // FOURIER_GROUPS
IN_COLS = 2 * LRU_WIDTH + FOURIER_WIDTH + 2 * D_MODEL
D_FF = 4 * D_MODEL
N_MOD = 6
EPS = 1e-6
POS_MAX_PERIOD = 10000.0

LANES = 128
SUBLANES = 8
BF16_ROWS = 16
assert BATCH == SUBLANES

TT = 64
ROWS = TT * BATCH
PITCH = TT + SUBLANES
HALO_T = SUBLANES
LEAD = CONV_PAD_LEFT * BATCH
TRAIL = (CONV_WIDTH - 1 - CONV_PAD_LEFT) * BATCH
XROWS = LEAD + ROWS + TRAIL
XROWS_PAD = -(-XROWS // BF16_ROWS) * BF16_ROWS
N_CTX_CHUNKS = CTX_LEN // TT
N_LAT_CHUNKS = SEQ // TT
NCHUNK = 512
FRONT_ROWS = 128
GATE_AHEAD = 1
GATE_SLOTS = GATE_AHEAD + 1
TM = 512
TAIL_WEIGHT_ROWS = (LRU_WIDTH, FOURIER_WIDTH, D_MODEL, D_MODEL, D_FF)
MOD_TN = 1024
VMEM_LIMIT = 58 * 1024 * 1024
LOG2E = math.log2(math.e)
TINY = 1e-30


def _gelu_tanh(x):
    k0 = jnp.full((1, 1), math.sqrt(2.0 / math.pi), F32).astype(x.dtype)
    k1 = jnp.full((1, 1), 0.044715, F32).astype(x.dtype)
    return 0.5 * x * (1.0 + jnp.tanh(k0 * (x + k1 * (x * x * x))))


def _sigmoid(x):
    return 0.5 * jnp.tanh(0.5 * x) + 0.5


def _dot(a, b):
    return jnp.dot(a, b, preferred_element_type=F32)


def _rms(x, g):
    ms = jnp.mean(x * x, axis=-1, keepdims=True)
    return x * lax.rsqrt(ms + EPS) * g


@functools.cache
def _constants():
    half = D_MODEL // 4
    freqs = np.exp(-math.log(POS_MAX_PERIOD) * np.arange(half, dtype=np.float32) / half).astype(np.float32)

    def sincos(n):
        ang = np.arange(n, dtype=np.float32)[:, None] * freqs[None, :]
        return np.concatenate([np.sin(ang), np.cos(ang)], axis=-1).astype(np.float32)

    rows = SEQ // GRID_W
    er, ec = sincos(rows), sincos(GRID_W)
    pos = np.concatenate([
        np.broadcast_to(er[:, None, :], (rows, GRID_W, D_MODEL // 2)),
        np.broadcast_to(ec[None, :, :], (rows, GRID_W, D_MODEL // 2)),
    ], axis=-1).reshape(SEQ, D_MODEL).astype(np.float32)

    def dft(n):
        k = np.arange(n, dtype=np.int64)
        ang = 2.0 * np.pi * ((k[:, None] * k[None, :]) % n).astype(np.float64) / n
        return np.cos(ang) / math.sqrt(n), np.sin(ang) / math.sqrt(n)

    cc, sc = dft(FOURIER_GROUP_DIM)
    chan = np.concatenate([cc, sc], axis=1).astype(np.float32)
    ct, st = dft(SEQ)
    time = np.stack([
        np.concatenate([ct[p::2, :SEQ // 2], -st[p::2, :SEQ // 2]], axis=1) for p in range(2)
    ]).astype(np.float32)
    return pos, chan, time


def _mod_kernel(c_ref, w_ref, b_ref, o_ref):
    c = c_ref[...]
    a = c * _sigmoid(c)
    w = w_ref[...]
    a_hi = a.astype(BF16)
    a_lo = (a - a_hi.astype(F32)).astype(BF16)
    w_hi = w.astype(BF16)
    w_lo = (w - w_hi.astype(F32)).astype(BF16)
    n = a.shape[0]
    both = _dot(jnp.concatenate([a_hi, a_lo], axis=0), w_hi)
    o_ref[...] = both[:n] + _dot(a_hi, w_lo) + both[n:] + b_ref[...]


def _mod_call(cc, w_mod, b_mod):
    n = w_mod.shape[1]
    return pl.pallas_call(
        _mod_kernel,
        grid=(n // MOD_TN,),
        in_specs=[
            pl.BlockSpec((2 * SUBLANES, D_MODEL), lambda i: (0, 0)),
            pl.BlockSpec((D_MODEL, MOD_TN), lambda i: (0, i)),
            pl.BlockSpec((1, MOD_TN), lambda i: (0, i)),
        ],
        out_specs=pl.BlockSpec((2 * SUBLANES, MOD_TN), lambda i: (0, i)),
        out_shape=jax.ShapeDtypeStruct((2 * SUBLANES, n), F32),
        name="mod",
        compiler_params=pltpu.CompilerParams(dimension_semantics=("arbitrary",)),
    )(cc, w_mod, b_mod)


def _mod_spec(row_block, k):
    return pl.BlockSpec((BATCH, D_MODEL), lambda c: (row_block, k))


def _head_kernel(*refs, lat, n_chunks):
    refs = list(refs)
    xm_ref, xp_ref, xn_ref = refs[:3]
    refs = refs[3:]
    if lat:
        pm_ref, pp_ref, pn_ref = refs[:3]
        refs = refs[3:]
    (sh_ref, sc_ref, g_ref, w_ref, cw_ref, cb_ref, wg_ref, ba_ref, bx_ref, lam_ref) = refs[:10]
    refs = refs[10:]
    if lat:
        xt_out, uy_out, uf_out, ug_out, ls_out, pf_out, pb_out = refs[:7]
        refs = refs[7:]
    else:
        w_slab_ref, w_bf16_out = refs[:2]
        refs = refs[2:]
        w_bf16_out[...] = w_slab_ref[...].astype(BF16)
    e_out, slab_ref, hslab_ref, xs_ref, hb_ref, a_ref, b_ref, xh_ref, gate_ref = refs

    c = pl.program_id(0)
    n_slab = D_MODEL // LANES
    mod_rows = slice(None) if lat else slice(0, 1)

    xm = xm_ref[...]
    xp = xp_ref[...]
    xn = xn_ref[...]
    if lat:
        xm = xm + pm_ref[...][None]
        xp = xp + pp_ref[...][None]
        xn = xn + pn_ref[...][None]
    for b in range(BATCH):
        for j in range(n_slab):
            ls = slice(j * LANES, (j + 1) * LANES)
            slab_ref[j, b * PITCH:b * PITCH + TT, :] = xm[b, :, ls]
            hslab_ref[0, j, b * HALO_T:(b + 1) * HALO_T, :] = xp[b, :, ls]
            hslab_ref[1, j, b * HALO_T:(b + 1) * HALO_T, :] = xn[b, :, ls]
    for j in range(n_slab):
        ls = slice(j * LANES, (j + 1) * LANES)
        for k in range(CONV_PAD_LEFT):
            xs_ref[k * BATCH:(k + 1) * BATCH, ls] = (
                hslab_ref[0, j, pl.ds(HALO_T - CONV_PAD_LEFT + k, BATCH, stride=HALO_T), :])
        for t in range(TT):
            xs_ref[LEAD + t * BATCH:LEAD + (t + 1) * BATCH, ls] = slab_ref[j, pl.ds(t, BATCH, stride=PITCH), :]
        for k in range(CONV_WIDTH - 1 - CONV_PAD_LEFT):
            r0 = LEAD + ROWS + k * BATCH
            xs_ref[r0:r0 + BATCH, ls] = hslab_ref[1, j, pl.ds(k, BATCH, stride=HALO_T), :]
    xs_ref[XROWS:XROWS_PAD, :] = jnp.zeros((XROWS_PAD - XROWS, D_MODEL), F32)

    assert LEAD % BF16_ROWS == 0 and FRONT_ROWS % BF16_ROWS == 0
    blocks = ([(0, LEAD)] + [(r, r + FRONT_ROWS) for r in range(LEAD, LEAD + ROWS, FRONT_ROWS)]
              + [(LEAD + ROWS, XROWS_PAD)])
    for r0, r1 in blocks:
        xt = xs_ref[r0:r1, :]
        if lat and LEAD <= r0 < LEAD + ROWS:
            xt_out[r0 - LEAD:r1 - LEAD, :] = xt
        h = _rms(xt, g_ref[...]).reshape((r1 - r0) // BATCH, BATCH, D_MODEL)
        h = h * (1.0 + sc_ref[mod_rows, :][None]) + sh_ref[mod_rows, :][None]
        hb_ref[r0:r1, :] = h.reshape(r1 - r0, D_MODEL).astype(BF16)

    keep_first = jnp.where(c == 0, 0.0, 1.0)
    keep_last = jnp.where(c == n_chunks - 1, 0.0, 1.0)
    for n0 in range(0, LRU_WIDTH, NCHUNK):
        w_x = w_ref[:, n0:n0 + NCHUNK]
        xs_ref[:, n0:n0 + NCHUNK] = _dot(hb_ref[...], w_x if lat else w_x.astype(BF16))
    xs_ref[0:LEAD, :] = xs_ref[0:LEAD, :] * keep_first
    xs_ref[LEAD + ROWS:XROWS, :] = xs_ref[LEAD + ROWS:XROWS, :] * keep_last

    def branch_piece(n0):
        r = _dot(hb_ref[LEAD:LEAD + ROWS, :], w_ref[:, n0:n0 + NCHUNK])
        if n0 < 2 * LRU_WIDTH:
            uy_out[:, n0 - LRU_WIDTH:n0 - LRU_WIDTH + NCHUNK] = r.astype(BF16)
        elif n0 < 2 * LRU_WIDTH + FOURIER_WIDTH:
            for s in range(NCHUNK // LANES):
                slab_ref[s, 0:ROWS, :] = r[:, s * LANES:(s + 1) * LANES]
            for b in range(BATCH):
                for s in range(NCHUNK // LANES):
                    c0 = b * FOURIER_WIDTH + (n0 - 2 * LRU_WIDTH) + s * LANES
                    uf_out[:, c0:c0 + LANES] = slab_ref[s, pl.ds(b, TT, stride=BATCH), :].astype(BF16)
        else:
            c0 = n0 - 2 * LRU_WIDTH - FOURIER_WIDTH
            ug_out[:, c0:c0 + NCHUNK] = r.astype(BF16)

    pieces = list(range(LRU_WIDTH, IN_COLS, NCHUNK)) if lat else []
    per_head = -(-len(pieces) // LRU_HEADS)

    lam = lam_ref[...]
    sp = jnp.maximum(-lam, 0.0) + jnp.log1p(jnp.exp(-jnp.abs(lam)))
    c1 = (-0.5 * LRU_C * LOG2E) * sp

    def conv_gates(hd):
        sl = slice(hd * LRU_HEAD_DIM, (hd + 1) * LRU_HEAD_DIM)
        xh = 0.5 * cb_ref[:, sl]
        for k in range(CONV_WIDTH):
            xh = xh + (0.5 * cw_ref[k:k + 1, sl]) * xs_ref[k * BATCH:k * BATCH + ROWS, sl]
        xh_ref[hd % GATE_SLOTS] = xh
        xhb = xh.astype(BF16)
        for d in range(2):
            gate_ref[hd % GATE_SLOTS, d] = _dot(xhb, wg_ref[d, hd])

    for hd in range(GATE_AHEAD):
        conv_gates(hd)
    for hd in range(LRU_HEADS):
        for n0 in pieces[hd * per_head:(hd + 1) * per_head]:
            branch_piece(n0)
        sl = slice(hd * LRU_HEAD_DIM, (hd + 1) * LRU_HEAD_DIM)
        xh = xh_ref[hd % GATE_SLOTS]
        for d in range(2):
            g = gate_ref[hd % GATE_SLOTS, d]
            ta = jnp.tanh(g[:, :LRU_HEAD_DIM] + 0.5 * ba_ref[d:d + 1, sl])
            ti = jnp.tanh(g[:, LRU_HEAD_DIM:] + 0.5 * bx_ref[d:d + 1, sl])
            a = jnp.exp2(c1[d:d + 1, sl] * ta + c1[d:d + 1, sl])
            v = 1.0 - a * a
            m = v * lax.rsqrt(jnp.maximum(v, TINY))
            a_ref[d, :, sl] = a
            b_ref[d, :, sl] = (m * xh) * (ti + 1.0)
        if hd + GATE_AHEAD < LRU_HEADS:
            conv_gates(hd + GATE_AHEAD)

        lf = pf = lb = pb = None
        for t in range(TT):
            rf = slice(t * BATCH, (t + 1) * BATCH)
            rb = slice((TT - 1 - t) * BATCH, (TT - t) * BATCH)
            af, bf = a_ref[0, rf, sl], b_ref[0, rf, sl]
            ab, bb = a_ref[1, rb, sl], b_ref[1, rb, sl]
            if t == 0:
                lf, pf, lb, pb = bf, af, bb, ab
            else:
                lf, pf = af * lf + bf, af * pf
                lb, pb = ab * lb + bb, ab * pb
                b_ref[0, rf, sl] = lf
                a_ref[0, rf, sl] = pf
                b_ref[1, rb, sl] = lb
                a_ref[1, rb, sl] = pb
        e_out[0, :, sl] = lf
        e_out[1, :, sl] = pf
        e_out[2, :, sl] = lb
        e_out[3, :, sl] = pb
        if lat:
            ls_out[:, sl] = (b_ref[0, :, sl] + b_ref[1, :, sl]).astype(BF16)
            pf_out[:, sl] = a_ref[0, :, sl].astype(BF16)
            pb_out[:, sl] = a_ref[1, :, sl].astype(BF16)


def _head_call(x, pos, mod, g, w, conv_w, conv_b, wg, b_a, b_x, lam, *, lat):
    seq = x.shape[1]
    n_chunks = seq // TT
    n_halo = seq // HALO_T
    per_chunk = TT // HALO_T

    def const(shape):
        return pl.BlockSpec(shape, lambda c: (0,) * len(shape))

    def prev_idx(c):
        return jnp.maximum(c * per_chunk - 1, 0)

    def next_idx(c):
        return jnp.minimum((c + 1) * per_chunk, n_halo - 1)

    in_specs = [
        pl.BlockSpec((BATCH, TT, D_MODEL), lambda c: (0, c, 0)),
        pl.BlockSpec((BATCH, HALO_T, D_MODEL), lambda c: (0, prev_idx(c), 0)),
        pl.BlockSpec((BATCH, HALO_T, D_MODEL), lambda c: (0, next_idx(c), 0)),
    ]
    args = [x, x, x]
    if lat:
        in_specs += [
            pl.BlockSpec((TT, D_MODEL), lambda c: (c, 0)),
            pl.BlockSpec((HALO_T, D_MODEL), lambda c: (prev_idx(c), 0)),
            pl.BlockSpec((HALO_T, D_MODEL), lambda c: (next_idx(c), 0)),
        ]
        args += [pos, pos, pos]
    in_specs += [
        _mod_spec(0 if lat else 1, 0), _mod_spec(0 if lat else 1, 1), const((1, D_MODEL)),
        const((D_MODEL, IN_COLS if lat else LRU_WIDTH)),
        const((CONV_WIDTH, LRU_WIDTH)), const((1, LRU_WIDTH)),
        const((2, LRU_HEADS, LRU_HEAD_DIM, 2 * LRU_HEAD_DIM)),
        const((2, LRU_WIDTH)), const((2, LRU_WIDTH)), const((2, LRU_WIDTH)),
    ]
    args += [mod, mod, g, w, conv_w, conv_b, wg, b_a, b_x, lam]
    w_slab = pl.BlockSpec((D_MODEL // n_chunks, IN_COLS), lambda c: (c, 0))
    if not lat:
        in_specs.append(w_slab)
        args.append(w)

    def rows(width):
        return pl.BlockSpec((ROWS, width), lambda c: (c, 0))

    out_specs, out_shape = [], []
    if lat:
        n = n_chunks * ROWS
        out_specs += [rows(D_MODEL), rows(LRU_WIDTH),
                      pl.BlockSpec((TT, BATCH * FOURIER_WIDTH), lambda c: (c, 0)),
                      rows(2 * D_MODEL), rows(LRU_WIDTH), rows(LRU_WIDTH), rows(LRU_WIDTH)]
        out_shape += [
            jax.ShapeDtypeStruct((n, D_MODEL), F32),
            jax.ShapeDtypeStruct((n, LRU_WIDTH), BF16),
            jax.ShapeDtypeStruct((seq, BATCH * FOURIER_WIDTH), BF16),
            jax.ShapeDtypeStruct((n, 2 * D_MODEL), BF16),
            jax.ShapeDtypeStruct((n, LRU_WIDTH), BF16),
            jax.ShapeDtypeStruct((n, LRU_WIDTH), BF16),
            jax.ShapeDtypeStruct((n, LRU_WIDTH), BF16),
        ]
    else:
        out_specs.append(w_slab)
        out_shape.append(jax.ShapeDtypeStruct((D_MODEL, IN_COLS), BF16))
    out_specs.append(pl.BlockSpec((None, 4, BATCH, LRU_WIDTH), lambda c: (c, 0, 0, 0)))
    out_shape.append(jax.ShapeDtypeStruct((n_chunks, 4, BATCH, LRU_WIDTH), F32))

    return pl.pallas_call(
        functools.partial(_head_kernel, lat=lat, n_chunks=n_chunks),
        grid=(n_chunks,),
        in_specs=in_specs,
        out_specs=out_specs,
        out_shape=out_shape,
        scratch_shapes=[
            pltpu.VMEM((D_MODEL // LANES, BATCH * PITCH, LANES), F32),
            pltpu.VMEM((2, D_MODEL // LANES, BATCH * HALO_T, LANES), F32),
            pltpu.VMEM((XROWS_PAD, D_MODEL), F32),
            pltpu.VMEM((XROWS_PAD, D_MODEL), BF16),
            pltpu.VMEM((2, ROWS, LRU_WIDTH), F32),
            pltpu.VMEM((2, ROWS, LRU_WIDTH), F32),
            pltpu.VMEM((GATE_SLOTS, ROWS, LRU_HEAD_DIM), F32),
            pltpu.VMEM((GATE_SLOTS, 2, ROWS, 2 * LRU_HEAD_DIM), F32),
        ],
        name="head_lat" if lat else "head_ctx",
        compiler_params=pltpu.CompilerParams(
            dimension_semantics=("arbitrary",), vmem_limit_bytes=VMEM_LIMIT),
    )(*args)


def _carry_kernel(ec_ref, el_ref, hf_ref, hb_ref):
    n_ctx, n_lat = ec_ref.shape[0], el_ref.shape[0]
    h = jnp.zeros((BATCH, LRU_WIDTH), F32)
    for c in range(n_ctx):
        h = ec_ref[c, 1] * h + ec_ref[c, 0]
    for c in range(n_lat):
        hf_ref[c] = h
        h = el_ref[c, 1] * h + el_ref[c, 0]
    h = jnp.zeros((BATCH, LRU_WIDTH), F32)
    for c in reversed(range(n_ctx)):
        h = ec_ref[c, 3] * h + ec_ref[c, 2]
    for c in reversed(range(n_lat)):
        hb_ref[c] = h
        h = el_ref[c, 3] * h + el_ref[c, 2]


def _carry_call(e_ctx, e_lat):
    shape = jax.ShapeDtypeStruct((e_lat.shape[0], BATCH, LRU_WIDTH), F32)
    return pl.pallas_call(_carry_kernel, out_shape=[shape, shape], name="carry")(e_ctx, e_lat)


def _fourier_kernel(x_ref, chan_ref, time_ref, *refs):
    n_w = len(TAIL_WEIGHT_ROWS)
    w_f32_refs, o_ref, w_bf16_refs, z_ref = refs[:n_w], refs[n_w], refs[n_w + 1:2 * n_w + 1], refs[-1]
    for src, dst in zip(w_f32_refs, w_bf16_refs):
        dst[...] = src[...].astype(BF16)

    half = SEQ // 2
    for g in range(FOURIER_GROUPS):
        sl = slice(g * FOURIER_GROUP_DIM, (g + 1) * FOURIER_GROUP_DIM)
        r = _dot(x_ref[:, sl], chan_ref[...])
        lo, hi = r[:half], r[half:]
        for p, v in enumerate((lo + hi, lo - hi)):
            z_ref[p, 0:half, sl] = v[:, :FOURIER_GROUP_DIM].astype(BF16)
            z_ref[p, half:SEQ, sl] = v[:, FOURIER_GROUP_DIM:].astype(BF16)
    for p in range(2):
        for m0 in range(0, half, TM):
            o_ref[p, m0:m0 + TM, :] = _dot(time_ref[p, m0:m0 + TM, :], z_ref[p]).astype(BF16)


def _fourier_call(uf, chan, time, weights):
    def slab(w):
        return pl.BlockSpec((w.shape[0] // BATCH, w.shape[1]), lambda n: (n, 0))

    assert tuple(w.shape[0] for w in weights) == TAIL_WEIGHT_ROWS
    return pl.pallas_call(
        _fourier_kernel,
        grid=(BATCH,),
        in_specs=[
            pl.BlockSpec((SEQ, FOURIER_WIDTH), lambda n: (0, n)),
            pl.BlockSpec((FOURIER_GROUP_DIM, 2 * FOURIER_GROUP_DIM), lambda n: (0, 0)),
            pl.BlockSpec((2, SEQ // 2, SEQ), lambda n: (0, 0, 0)),
        ] + [slab(w) for w in weights],
        out_specs=[pl.BlockSpec((2, SEQ // 2, FOURIER_WIDTH), lambda n: (0, 0, n))]
        + [slab(w) for w in weights],
        out_shape=[jax.ShapeDtypeStruct((2, SEQ // 2, BATCH * FOURIER_WIDTH), BF16)]
        + [jax.ShapeDtypeStruct(w.shape, BF16) for w in weights],
        scratch_shapes=[pltpu.VMEM((2, SEQ, FOURIER_WIDTH), BF16)],
        name="fourier",
        compiler_params=pltpu.CompilerParams(
            dimension_semantics=("arbitrary",), vmem_limit_bytes=VMEM_LIMIT),
    )(uf, chan, time, *weights)


def _tail_kernel(ls_ref, pf_ref, pb_ref, hf_ref, hb_ref, uy_ref, ug_ref, yf_ref, xt_ref,
                 gt1_ref, sh2_ref, sc2_ref, gt2_ref, gm_ref, gf_ref,
                 wl_ref, wf_ref, wo_ref, w1_ref, w2_ref, o_ref, fslab_ref, oslab_ref):
    def tb(v):
        return v.reshape(TT, BATCH, v.shape[-1])

    def flat(v):
        return v.reshape(ROWS, v.shape[-1])

    for b in range(BATCH):
        for p in range(2):
            v = yf_ref[p, :, b * FOURIER_WIDTH:(b + 1) * FOURIER_WIDTH].astype(F32)
            for s in range(FOURIER_WIDTH // LANES):
                fslab_ref[s, pl.ds(p * BATCH + b, TT // 2, stride=2 * BATCH), :] = (
                    v[:, s * LANES:(s + 1) * LANES])
    yf = jnp.concatenate([fslab_ref[s] for s in range(FOURIER_WIDTH // LANES)], axis=-1).astype(BF16)
    y_b = _dot(yf, wf_ref[...])

    hf = jnp.tile(hf_ref[...].astype(BF16), (TT, 1))
    hb = jnp.tile(hb_ref[...].astype(BF16), (TT, 1))
    y_lru = ls_ref[...] + pf_ref[...] * hf + pb_ref[...] * hb
    z = y_lru * _gelu_tanh(uy_ref[...])
    y_a = _dot(z, wl_ref[...])

    g_a = _sigmoid(ug_ref[:, :D_MODEL].astype(F32))
    g_b = _sigmoid(ug_ref[:, D_MODEL:].astype(F32))
    merged = (g_a * y_a + g_b * y_b).astype(BF16)
    x2 = tb(xt_ref[...]) + gt1_ref[...][None] * tb(_dot(merged, wo_ref[...]))

    h = (tb(_rms(flat(x2), gm_ref[...])) * (1.0 + sc2_ref[...][None]) + sh2_ref[...][None])
    h = flat(h).astype(BF16)
    acc = jnp.zeros((ROWS, D_MODEL), F32)
    for k0 in range(0, D_FF, D_MODEL):
        a = jnp.maximum(_dot(h, w1_ref[:, k0:k0 + D_MODEL]), 0.0)
        acc = acc + _dot((a * a).astype(BF16), w2_ref[k0:k0 + D_MODEL, :])
    out = _rms(flat(x2 + gt2_ref[...][None] * tb(acc)), gf_ref[...])

    for j in range(D_MODEL // LANES):
        oslab_ref[j] = out[:, j * LANES:(j + 1) * LANES]
    for b in range(BATCH):
        for j in range(D_MODEL // LANES):
            o_ref[b, :, j * LANES:(j + 1) * LANES] = oslab_ref[j, pl.ds(b, TT, stride=BATCH), :]


def _tail_call(ls, pf, pb, hf, hb, uy, ug, yf, xt, mod, gm, gf, wl, wf, wo, w1, w2):
    def rows(width):
        return pl.BlockSpec((ROWS, width), lambda c: (c, 0))

    def const(shape):
        return pl.BlockSpec(shape, lambda c: (0,) * len(shape))

    state = pl.BlockSpec((None, BATCH, LRU_WIDTH), lambda c: (c, 0, 0))
    vec1 = const((1, D_MODEL))
    return pl.pallas_call(
        _tail_kernel,
        grid=(N_LAT_CHUNKS,),
        in_specs=[
            rows(LRU_WIDTH), rows(LRU_WIDTH), rows(LRU_WIDTH), state, state,
            rows(LRU_WIDTH), rows(2 * D_MODEL),
            pl.BlockSpec((2, TT // 2, BATCH * FOURIER_WIDTH), lambda c: (0, c, 0)),
            rows(D_MODEL),
            _mod_spec(0, 2), _mod_spec(0, 3), _mod_spec(0, 4), _mod_spec(0, 5), vec1, vec1,
            const((LRU_WIDTH, D_MODEL)), const((FOURIER_WIDTH, D_MODEL)), const((D_MODEL, D_MODEL)),
            const((D_MODEL, D_FF)), const((D_FF, D_MODEL)),
        ],
        out_specs=pl.BlockSpec((BATCH, TT, D_MODEL), lambda c: (0, c, 0)),
        out_shape=jax.ShapeDtypeStruct((BATCH, SEQ, D_MODEL), F32),
        scratch_shapes=[
            pltpu.VMEM((FOURIER_WIDTH // LANES, ROWS, LANES), F32),
            pltpu.VMEM((D_MODEL // LANES, ROWS, LANES), F32),
        ],
        name="tail",
        compiler_params=pltpu.CompilerParams(
            dimension_semantics=("arbitrary",), vmem_limit_bytes=VMEM_LIMIT),
    )(ls, pf, pb, hf, hb, uy, ug, yf, xt, mod, mod, mod, mod, gm, gf, wl, wf, wo, w1, w2)


def kernel(x, c, ctx, c_ctx, w_mod, b_mod, g_mix, w_in, conv_w, conv_b, w_a, b_a, w_x, b_x,
           lam, w_lru_out, w_f_out, w_out, g_mlp, w1, w2, g_final):
    pos_np, chan_np, time_np = _constants()
    pos = jnp.asarray(pos_np)
    chan, time = jnp.asarray(chan_np).astype(BF16), jnp.asarray(time_np).astype(BF16)

    cc = jnp.concatenate(
        [c, c_ctx[None], jnp.zeros((2 * SUBLANES - BATCH - 1, D_MODEL), F32)], axis=0)
    mod = _mod_call(cc, w_mod[0], b_mod[0][None])

    g_mix2 = g_mix[0][None]
    wg = jnp.concatenate([w_a[0], w_x[0]], axis=-1).astype(BF16)
    lru = (conv_w[0], conv_b[0][None], wg, b_a[0], b_x[0], lam[0])

    w_in_b, e_ctx = _head_call(ctx, None, mod, g_mix2, w_in[0], *lru, lat=False)
    xt, uy, uf, ug, ls, pf, pb, e_lat = _head_call(x, pos, mod, g_mix2, w_in_b, *lru, lat=True)
    hf, hb = _carry_call(e_ctx, e_lat)
    yf, wl, wf, wo, w1b, w2b = _fourier_call(
        uf, chan, time, (w_lru_out[0], w_f_out[0], w_out[0], w1[0], w2[0]))
    return _tail_call(ls, pf, pb, hf, hb, uy, ug, yf, xt, mod,
                      g_mlp[0][None], g_final[None], wl, wf, wo, w1b, w2b)
```

```python
import functools
import math

import numpy as np
import jax
import jax.numpy as jnp
from jax import lax
from jax.experimental import pallas as pl
from jax.experimental.pallas import tpu as pltpu

F32 = jnp.float32
BF16 = jnp.bfloat16

D_MODEL = 1024
BATCH = 8
SEQ = 2048
CTX_LEN = 256
GRID_W = 64
LRU_WIDTH = 1024
LRU_HEADS = 8
LRU_HEAD_DIM = LRU_WIDTH // LRU_HEADS
LRU_C = 8.0
CONV_WIDTH = 4
CONV_PAD_LEFT = 2
FOURIER_WIDTH = 512
FOURIER_GROUPS = 4
FOURIER_GROUP_DIM = FOURIER_WIDTH // FOURIER_GROUPS
IN_COLS = 2 * LRU_WIDTH + FOURIER_WIDTH + 2 * D_MODEL
D_FF = 4 * D_MODEL
N_MOD = 6
EPS = 1e-6
POS_MAX_PERIOD = 10000.0

LANES = 128
SUBLANES = 8
BF16_ROWS = 16
assert BATCH == SUBLANES

TT = 64
ROWS = TT * BATCH
PITCH = TT + SUBLANES
HALO_T = SUBLANES
LEAD = CONV_PAD_LEFT * BATCH
TRAIL = (CONV_WIDTH - 1 - CONV_PAD_LEFT) * BATCH
XROWS = LEAD + ROWS + TRAIL
XROWS_PAD = -(-XROWS // BF16_ROWS) * BF16_ROWS
N_CTX_CHUNKS = CTX_LEN // TT
N_LAT_CHUNKS = SEQ // TT
NCHUNK = 512
FRONT_ROWS = 128
GATE_AHEAD = 1
GATE_SLOTS = GATE_AHEAD + 1
TM = 512
TAIL_WEIGHT_ROWS = (LRU_WIDTH, FOURIER_WIDTH, D_MODEL, D_MODEL, D_FF)
MOD_TN = 1024
V7X_VMEM_BYTES = 64 * 1024 * 1024
VMEM_RESERVE = 6 * 1024 * 1024
VMEM_LIMIT = V7X_VMEM_BYTES - VMEM_RESERVE
LOG2E = math.log2(math.e)
TINY = 1e-30


def _gelu_tanh(x):
    k0 = jnp.full((1, 1), math.sqrt(2.0 / math.pi), F32).astype(x.dtype)
    k1 = jnp.full((1, 1), 0.044715, F32).astype(x.dtype)
    return 0.5 * x * (1.0 + jnp.tanh(k0 * (x + k1 * (x * x * x))))


def _sigmoid(x):
    return 0.5 * jnp.tanh(0.5 * x) + 0.5


def _dot(a, b):
    return jnp.dot(a, b, preferred_element_type=F32)


def _rms(x, g):
    ms = jnp.mean(x * x, axis=-1, keepdims=True)
    return x * lax.rsqrt(ms + EPS) * g


@functools.cache
def _constants():
    half = D_MODEL // 4
    freqs = np.exp(-math.log(POS_MAX_PERIOD) * np.arange(half, dtype=np.float32) / half).astype(np.float32)

    def sincos(n):
        ang = np.arange(n, dtype=np.float32)[:, None] * freqs[None, :]
        return np.concatenate([np.sin(ang), np.cos(ang)], axis=-1).astype(np.float32)

    rows = SEQ // GRID_W
    er, ec = sincos(rows), sincos(GRID_W)
    pos = np.concatenate([
        np.broadcast_to(er[:, None, :], (rows, GRID_W, D_MODEL // 2)),
        np.broadcast_to(ec[None, :, :], (rows, GRID_W, D_MODEL // 2)),
    ], axis=-1).reshape(SEQ, D_MODEL).astype(np.float32)

    def dft(n):
        k = np.arange(n, dtype=np.int64)
        ang = 2.0 * np.pi * ((k[:, None] * k[None, :]) % n).astype(np.float64) / n
        return np.cos(ang) / math.sqrt(n), np.sin(ang) / math.sqrt(n)

    cc, sc = dft(FOURIER_GROUP_DIM)
    chan = np.concatenate([cc, sc], axis=1).astype(np.float32)
    ct, st = dft(SEQ)
    time = np.stack([
        np.concatenate([ct[p::2, :SEQ // 2], -st[p::2, :SEQ // 2]], axis=1) for p in range(2)
    ]).astype(np.float32)
    return pos, chan, time


def _mod_kernel(c_ref, w_ref, b_ref, o_ref):
    c = c_ref[...]
    a = c * _sigmoid(c)
    w = w_ref[...]
    a_hi = a.astype(BF16)
    a_lo = (a - a_hi.astype(F32)).astype(BF16)
    w_hi = w.astype(BF16)
    w_lo = (w - w_hi.astype(F32)).astype(BF16)
    n = a.shape[0]
    both = _dot(jnp.concatenate([a_hi, a_lo], axis=0), w_hi)
    o_ref[...] = both[:n] + _dot(a_hi, w_lo) + both[n:] + b_ref[...]


def _mod_call(cc, w_mod, b_mod):
    n = w_mod.shape[1]
    return pl.pallas_call(
        _mod_kernel,
        grid=(n // MOD_TN,),
        in_specs=[
            pl.BlockSpec((2 * SUBLANES, D_MODEL), lambda i: (0, 0)),
            pl.BlockSpec((D_MODEL, MOD_TN), lambda i: (0, i)),
            pl.BlockSpec((1, MOD_TN), lambda i: (0, i)),
        ],
        out_specs=pl.BlockSpec((2 * SUBLANES, MOD_TN), lambda i: (0, i)),
        out_shape=jax.ShapeDtypeStruct((2 * SUBLANES, n), F32),
        name="mod",
        compiler_params=pltpu.CompilerParams(dimension_semantics=("arbitrary",)),
    )(cc, w_mod, b_mod)


def _mod_spec(row_block, k):
    return pl.BlockSpec((BATCH, D_MODEL), lambda c: (row_block, k))


def _head_kernel(*refs, lat, n_chunks):
    refs = list(refs)
    xm_ref, xp_ref, xn_ref = refs[:3]
    refs = refs[3:]
    if lat:
        pm_ref, pp_ref, pn_ref = refs[:3]
        refs = refs[3:]
    (sh_ref, sc_ref, g_ref, w_ref, cw_ref, cb_ref, wg_ref, ba_ref, bx_ref, lam_ref) = refs[:10]
    refs = refs[10:]
    if lat:
        xt_out, uy_out, uf_out, ug_out, ls_out, pf_out, pb_out = refs[:7]
        refs = refs[7:]
    else:
        w_slab_ref, w_bf16_out = refs[:2]
        refs = refs[2:]
        w_bf16_out[...] = w_slab_ref[...].astype(BF16)
    e_out, slab_ref, hslab_ref, xs_ref, hb_ref, a_ref, b_ref, xh_ref, gate_ref = refs

    c = pl.program_id(0)
    n_slab = D_MODEL // LANES
    mod_rows = slice(None) if lat else slice(0, 1)

    xm = xm_ref[...]
    xp = xp_ref[...]
    xn = xn_ref[...]
    if lat:
        xm = xm + pm_ref[...][None]
        xp = xp + pp_ref[...][None]
        xn = xn + pn_ref[...][None]
    for b in range(BATCH):
        for j in range(n_slab):
            ls = slice(j * LANES, (j + 1) * LANES)
            slab_ref[j, b * PITCH:b * PITCH + TT, :] = xm[b, :, ls]
            hslab_ref[0, j, b * HALO_T:(b + 1) * HALO_T, :] = xp[b, :, ls]
            hslab_ref[1, j, b * HALO_T:(b + 1) * HALO_T, :] = xn[b, :, ls]
    for j in range(n_slab):
        ls = slice(j * LANES, (j + 1) * LANES)
        for k in range(CONV_PAD_LEFT):
            xs_ref[k * BATCH:(k + 1) * BATCH, ls] = (
                hslab_ref[0, j, pl.ds(HALO_T - CONV_PAD_LEFT + k, BATCH, stride=HALO_T), :])
        for t in range(TT):
            xs_ref[LEAD + t * BATCH:LEAD + (t + 1) * BATCH, ls] = slab_ref[j, pl.ds(t, BATCH, stride=PITCH), :]
        for k in range(CONV_WIDTH - 1 - CONV_PAD_LEFT):
            r0 = LEAD + ROWS + k * BATCH
            xs_ref[r0:r0 + BATCH, ls] = hslab_ref[1, j, pl.ds(k, BATCH, stride=HALO_T), :]
    xs_ref[XROWS:XROWS_PAD, :] = jnp.zeros((XROWS_PAD - XROWS, D_MODEL), F32)

    assert LEAD % BF16_ROWS == 0 and FRONT_ROWS % BF16_ROWS == 0
    blocks = ([(0, LEAD)] + [(r, r + FRONT_ROWS) for r in range(LEAD, LEAD + ROWS, FRONT_ROWS)]
              + [(LEAD + ROWS, XROWS_PAD)])
    for r0, r1 in blocks:
        xt = xs_ref[r0:r1, :]
        if lat and LEAD <= r0 < LEAD + ROWS:
            xt_out[r0 - LEAD:r1 - LEAD, :] = xt
        h = _rms(xt, g_ref[...]).reshape((r1 - r0) // BATCH, BATCH, D_MODEL)
        h = h * (1.0 + sc_ref[mod_rows, :][None]) + sh_ref[mod_rows, :][None]
        hb_ref[r0:r1, :] = h.reshape(r1 - r0, D_MODEL).astype(BF16)

    keep_first = jnp.where(c == 0, 0.0, 1.0)
    keep_last = jnp.where(c == n_chunks - 1, 0.0, 1.0)
    for n0 in range(0, LRU_WIDTH, NCHUNK):
        w_x = w_ref[:, n0:n0 + NCHUNK]
        xs_ref[:, n0:n0 + NCHUNK] = _dot(hb_ref[...], w_x if lat else w_x.astype(BF16))
    xs_ref[0:LEAD, :] = xs_ref[0:LEAD, :] * keep_first
    xs_ref[LEAD + ROWS:XROWS, :] = xs_ref[LEAD + ROWS:XROWS, :] * keep_last

    def branch_piece(n0):
        r = _dot(hb_ref[LEAD:LEAD + ROWS, :], w_ref[:, n0:n0 + NCHUNK])
        if n0 < 2 * LRU_WIDTH:
            uy_out[:, n0 - LRU_WIDTH:n0 - LRU_WIDTH + NCHUNK] = r.astype(BF16)
        elif n0 < 2 * LRU_WIDTH + FOURIER_WIDTH:
            for s in range(NCHUNK // LANES):
                slab_ref[s, 0:ROWS, :] = r[:, s * LANES:(s + 1) * LANES]
            for b in range(BATCH):
                for s in range(NCHUNK // LANES):
                    c0 = b * FOURIER_WIDTH + (n0 - 2 * LRU_WIDTH) + s * LANES
                    uf_out[:, c0:c0 + LANES] = slab_ref[s, pl.ds(b, TT, stride=BATCH), :].astype(BF16)
        else:
            c0 = n0 - 2 * LRU_WIDTH - FOURIER_WIDTH
            ug_out[:, c0:c0 + NCHUNK] = r.astype(BF16)

    pieces = list(range(LRU_WIDTH, IN_COLS, NCHUNK)) if lat else []
    per_head = -(-len(pieces) // LRU_HEADS)

    lam = lam_ref[...]
    sp = jnp.maximum(-lam, 0.0) + jnp.log1p(jnp.exp(-jnp.abs(lam)))
    c1 = (-0.5 * LRU_C * LOG2E) * sp

    def conv_gates(hd):
        sl = slice(hd * LRU_HEAD_DIM, (hd + 1) * LRU_HEAD_DIM)
        xh = 0.5 * cb_ref[:, sl]
        for k in range(CONV_WIDTH):
            xh = xh + (0.5 * cw_ref[k:k + 1, sl]) * xs_ref[k * BATCH:k * BATCH + ROWS, sl]
        xh_ref[hd % GATE_SLOTS] = xh
        xhb = xh.astype(BF16)
        for d in range(2):
            gate_ref[hd % GATE_SLOTS, d] = _dot(xhb, wg_ref[d, hd])

    for hd in range(GATE_AHEAD):
        conv_gates(hd)
    for hd in range(LRU_HEADS):
        for n0 in pieces[hd * per_head:(hd + 1) * per_head]:
            branch_piece(n0)
        sl = slice(hd * LRU_HEAD_DIM, (hd + 1) * LRU_HEAD_DIM)
        xh = xh_ref[hd % GATE_SLOTS]
        for d in range(2):
            g = gate_ref[hd % GATE_SLOTS, d]
            ta = jnp.tanh(g[:, :LRU_HEAD_DIM] + 0.5 * ba_ref[d:d + 1, sl])
            ti = jnp.tanh(g[:, LRU_HEAD_DIM:] + 0.5 * bx_ref[d:d + 1, sl])
            a = jnp.exp2(c1[d:d + 1, sl] * ta + c1[d:d + 1, sl])
            v = 1.0 - a * a
            m = v * lax.rsqrt(jnp.maximum(v, TINY))
            a_ref[d, :, sl] = a
            b_ref[d, :, sl] = (m * xh) * (ti + 1.0)
        if hd + GATE_AHEAD < LRU_HEADS:
            conv_gates(hd + GATE_AHEAD)

        lf = pf = lb = pb = None
        for t in range(TT):
            rf = slice(t * BATCH, (t + 1) * BATCH)
            rb = slice((TT - 1 - t) * BATCH, (TT - t) * BATCH)
            af, bf = a_ref[0, rf, sl], b_ref[0, rf, sl]
            ab, bb = a_ref[1, rb, sl], b_ref[1, rb, sl]
            if t == 0:
                lf, pf, lb, pb = bf, af, bb, ab
            else:
                lf, pf = af * lf + bf, af * pf
                lb, pb = ab * lb + bb, ab * pb
                b_ref[0, rf, sl] = lf
                a_ref[0, rf, sl] = pf
                b_ref[1, rb, sl] = lb
                a_ref[1, rb, sl] = pb
        e_out[0, :, sl] = lf
        e_out[1, :, sl] = pf
        e_out[2, :, sl] = lb
        e_out[3, :, sl] = pb
        if lat:
            ls_out[:, sl] = (b_ref[0, :, sl] + b_ref[1, :, sl]).astype(BF16)
            pf_out[:, sl] = a_ref[0, :, sl].astype(BF16)
            pb_out[:, sl] = a_ref[1, :, sl].astype(BF16)


def _head_call(x, pos, mod, g, w, conv_w, conv_b, wg, b_a, b_x, lam, *, lat):
    seq = x.shape[1]
    n_chunks = seq // TT
    n_halo = seq // HALO_T
    per_chunk = TT // HALO_T

    def const(shape):
        return pl.BlockSpec(shape, lambda c: (0,) * len(shape))

    def prev_idx(c):
        return jnp.maximum(c * per_chunk - 1, 0)

    def next_idx(c):
        return jnp.minimum((c + 1) * per_chunk, n_halo - 1)

    in_specs = [
        pl.BlockSpec((BATCH, TT, D_MODEL), lambda c: (0, c, 0)),
        pl.BlockSpec((BATCH, HALO_T, D_MODEL), lambda c: (0, prev_idx(c), 0)),
        pl.BlockSpec((BATCH, HALO_T, D_MODEL), lambda c: (0, next_idx(c), 0)),
    ]
    args = [x, x, x]
    if lat:
        in_specs += [
            pl.BlockSpec((TT, D_MODEL), lambda c: (c, 0)),
            pl.BlockSpec((HALO_T, D_MODEL), lambda c: (prev_idx(c), 0)),
            pl.BlockSpec((HALO_T, D_MODEL), lambda c: (next_idx(c), 0)),
        ]
        args += [pos, pos, pos]
    in_specs += [
        _mod_spec(0 if lat else 1, 0), _mod_spec(0 if lat else 1, 1), const((1, D_MODEL)),
        const((D_MODEL, IN_COLS if lat else LRU_WIDTH)),
        const((CONV_WIDTH, LRU_WIDTH)), const((1, LRU_WIDTH)),
        const((2, LRU_HEADS, LRU_HEAD_DIM, 2 * LRU_HEAD_DIM)),
        const((2, LRU_WIDTH)), const((2, LRU_WIDTH)), const((2, LRU_WIDTH)),
    ]
    args += [mod, mod, g, w, conv_w, conv_b, wg, b_a, b_x, lam]
    w_slab = pl.BlockSpec((D_MODEL // n_chunks, IN_COLS), lambda c: (c, 0))
    if not lat:
        in_specs.append(w_slab)
        args.append(w)

    def rows(width):
        return pl.BlockSpec((ROWS, width), lambda c: (c, 0))

    out_specs, out_shape = [], []
    if lat:
        n = n_chunks * ROWS
        out_specs += [rows(D_MODEL), rows(LRU_WIDTH),
                      pl.BlockSpec((TT, BATCH * FOURIER_WIDTH), lambda c: (c, 0)),
                      rows(2 * D_MODEL), rows(LRU_WIDTH), rows(LRU_WIDTH), rows(LRU_WIDTH)]
        out_shape += [
            jax.ShapeDtypeStruct((n, D_MODEL), F32),
            jax.ShapeDtypeStruct((n, LRU_WIDTH), BF16),
            jax.ShapeDtypeStruct((seq, BATCH * FOURIER_WIDTH), BF16),
            jax.ShapeDtypeStruct((n, 2 * D_MODEL), BF16),
            jax.ShapeDtypeStruct((n, LRU_WIDTH), BF16),
            jax.ShapeDtypeStruct((n, LRU_WIDTH), BF16),
            jax.ShapeDtypeStruct((n, LRU_WIDTH), BF16),
        ]
    else:
        out_specs.append(w_slab)
        out_shape.append(jax.ShapeDtypeStruct((D_MODEL, IN_COLS), BF16))
    out_specs.append(pl.BlockSpec((None, 4, BATCH, LRU_WIDTH), lambda c: (c, 0, 0, 0)))
    out_shape.append(jax.ShapeDtypeStruct((n_chunks, 4, BATCH, LRU_WIDTH), F32))

    return pl.pallas_call(
        functools.partial(_head_kernel, lat=lat, n_chunks=n_chunks),
        grid=(n_chunks,),
        in_specs=in_specs,
        out_specs=out_specs,
        out_shape=out_shape,
        scratch_shapes=[
            pltpu.VMEM((D_MODEL // LANES, BATCH * PITCH, LANES), F32),
            pltpu.VMEM((2, D_MODEL // LANES, BATCH * HALO_T, LANES), F32),
            pltpu.VMEM((XROWS_PAD, D_MODEL), F32),
            pltpu.VMEM((XROWS_PAD, D_MODEL), BF16),
            pltpu.VMEM((2, ROWS, LRU_WIDTH), F32),
            pltpu.VMEM((2, ROWS, LRU_WIDTH), F32),
            pltpu.VMEM((GATE_SLOTS, ROWS, LRU_HEAD_DIM), F32),
            pltpu.VMEM((GATE_SLOTS, 2, ROWS, 2 * LRU_HEAD_DIM), F32),
        ],
        name="head_lat" if lat else "head_ctx",
        compiler_params=pltpu.CompilerParams(
            dimension_semantics=("arbitrary",), vmem_limit_bytes=VMEM_LIMIT),
    )(*args)


def _carry_kernel(ec_ref, el_ref, hf_ref, hb_ref):
    n_ctx, n_lat = ec_ref.shape[0], el_ref.shape[0]
    h = jnp.zeros((BATCH, LRU_WIDTH), F32)
    for c in range(n_ctx):
        h = ec_ref[c, 1] * h + ec_ref[c, 0]
    for c in range(n_lat):
        hf_ref[c] = h
        h = el_ref[c, 1] * h + el_ref[c, 0]
    h = jnp.zeros((BATCH, LRU_WIDTH), F32)
    for c in reversed(range(n_ctx)):
        h = ec_ref[c, 3] * h + ec_ref[c, 2]
    for c in reversed(range(n_lat)):
        hb_ref[c] = h
        h = el_ref[c, 3] * h + el_ref[c, 2]


def _carry_call(e_ctx, e_lat):
    shape = jax.ShapeDtypeStruct((e_lat.shape[0], BATCH, LRU_WIDTH), F32)
    return pl.pallas_call(_carry_kernel, out_shape=[shape, shape], name="carry")(e_ctx, e_lat)


def _fourier_kernel(x_ref, chan_ref, time_ref, *refs):
    n_w = len(TAIL_WEIGHT_ROWS)
    w_f32_refs, o_ref, w_bf16_refs, z_ref = refs[:n_w], refs[n_w], refs[n_w + 1:2 * n_w + 1], refs[-1]
    for src, dst in zip(w_f32_refs, w_bf16_refs):
        dst[...] = src[...].astype(BF16)

    half = SEQ // 2
    for g in range(FOURIER_GROUPS):
        sl = slice(g * FOURIER_GROUP_DIM, (g + 1) * FOURIER_GROUP_DIM)
        r = _dot(x_ref[:, sl], chan_ref[...])
        lo, hi = r[:half], r[half:]
        for p, v in enumerate((lo + hi, lo - hi)):
            z_ref[p, 0:half, sl] = v[:, :FOURIER_GROUP_DIM].astype(BF16)
            z_ref[p, half:SEQ, sl] = v[:, FOURIER_GROUP_DIM:].astype(BF16)
    for p in range(2):
        for m0 in range(0, half, TM):
            o_ref[p, m0:m0 + TM, :] = _dot(time_ref[p, m0:m0 + TM, :], z_ref[p]).astype(BF16)


def _fourier_call(uf, chan, time, weights):
    def slab(w):
        return pl.BlockSpec((w.shape[0] // BATCH, w.shape[1]), lambda n: (n, 0))

    assert tuple(w.shape[0] for w in weights) == TAIL_WEIGHT_ROWS
    return pl.pallas_call(
        _fourier_kernel,
        grid=(BATCH,),
        in_specs=[
            pl.BlockSpec((SEQ, FOURIER_WIDTH), lambda n: (0, n)),
            pl.BlockSpec((FOURIER_GROUP_DIM, 2 * FOURIER_GROUP_DIM), lambda n: (0, 0)),
            pl.BlockSpec((2, SEQ // 2, SEQ), lambda n: (0, 0, 0)),
        ] + [slab(w) for w in weights],
        out_specs=[pl.BlockSpec((2, SEQ // 2, FOURIER_WIDTH), lambda n: (0, 0, n))]
        + [slab(w) for w in weights],
        out_shape=[jax.ShapeDtypeStruct((2, SEQ // 2, BATCH * FOURIER_WIDTH), BF16)]
        + [jax.ShapeDtypeStruct(w.shape, BF16) for w in weights],
        scratch_shapes=[pltpu.VMEM((2, SEQ, FOURIER_WIDTH), BF16)],
        name="fourier",
        compiler_params=pltpu.CompilerParams(
            dimension_semantics=("arbitrary",), vmem_limit_bytes=VMEM_LIMIT),
    )(uf, chan, time, *weights)


def _tail_kernel(ls_ref, pf_ref, pb_ref, hf_ref, hb_ref, uy_ref, ug_ref, yf_ref, xt_ref,
                 gt1_ref, sh2_ref, sc2_ref, gt2_ref, gm_ref, gf_ref,
                 wl_ref, wf_ref, wo_ref, w1_ref, w2_ref, o_ref, fslab_ref, oslab_ref):
    def tb(v):
        return v.reshape(TT, BATCH, v.shape[-1])

    def flat(v):
        return v.reshape(ROWS, v.shape[-1])

    for b in range(BATCH):
        for p in range(2):
            v = yf_ref[p, :, b * FOURIER_WIDTH:(b + 1) * FOURIER_WIDTH].astype(F32)
            for s in range(FOURIER_WIDTH // LANES):
                fslab_ref[s, pl.ds(p * BATCH + b, TT // 2, stride=2 * BATCH), :] = (
                    v[:, s * LANES:(s + 1) * LANES])
    yf = jnp.concatenate([fslab_ref[s] for s in range(FOURIER_WIDTH // LANES)], axis=-1).astype(BF16)
    y_b = _dot(yf, wf_ref[...])

    hf = jnp.tile(hf_ref[...].astype(BF16), (TT, 1))
    hb = jnp.tile(hb_ref[...].astype(BF16), (TT, 1))
    y_lru = ls_ref[...] + pf_ref[...] * hf + pb_ref[...] * hb
    z = y_lru * _gelu_tanh(uy_ref[...])
    y_a = _dot(z, wl_ref[...])

    g_a = _sigmoid(ug_ref[:, :D_MODEL].astype(F32))
    g_b = _sigmoid(ug_ref[:, D_MODEL:].astype(F32))
    merged = (g_a * y_a + g_b * y_b).astype(BF16)
    x2 = tb(xt_ref[...]) + gt1_ref[...][None] * tb(_dot(merged, wo_ref[...]))

    h = (tb(_rms(flat(x2), gm_ref[...])) * (1.0 + sc2_ref[...][None]) + sh2_ref[...][None])
    h = flat(h).astype(BF16)
    acc = jnp.zeros((ROWS, D_MODEL), F32)
    for k0 in range(0, D_FF, D_MODEL):
        a = jnp.maximum(_dot(h, w1_ref[:, k0:k0 + D_MODEL]), 0.0)
        acc = acc + _dot((a * a).astype(BF16), w2_ref[k0:k0 + D_MODEL, :])
    out = _rms(flat(x2 + gt2_ref[...][None] * tb(acc)), gf_ref[...])

    for j in range(D_MODEL // LANES):
        oslab_ref[j] = out[:, j * LANES:(j + 1) * LANES]
    for b in range(BATCH):
        for j in range(D_MODEL // LANES):
            o_ref[b, :, j * LANES:(j + 1) * LANES] = oslab_ref[j, pl.ds(b, TT, stride=BATCH), :]


def _tail_call(ls, pf, pb, hf, hb, uy, ug, yf, xt, mod, gm, gf, wl, wf, wo, w1, w2):
    def rows(width):
        return pl.BlockSpec((ROWS, width), lambda c: (c, 0))

    def const(shape):
        return pl.BlockSpec(shape, lambda c: (0,) * len(shape))

    state = pl.BlockSpec((None, BATCH, LRU_WIDTH), lambda c: (c, 0, 0))
    vec1 = const((1, D_MODEL))
    return pl.pallas_call(
        _tail_kernel,
        grid=(N_LAT_CHUNKS,),
        in_specs=[
            rows(LRU_WIDTH), rows(LRU_WIDTH), rows(LRU_WIDTH), state, state,
            rows(LRU_WIDTH), rows(2 * D_MODEL),
            pl.BlockSpec((2, TT // 2, BATCH * FOURIER_WIDTH), lambda c: (0, c, 0)),
            rows(D_MODEL),
            _mod_spec(0, 2), _mod_spec(0, 3), _mod_spec(0, 4), _mod_spec(0, 5), vec1, vec1,
            const((LRU_WIDTH, D_MODEL)), const((FOURIER_WIDTH, D_MODEL)), const((D_MODEL, D_MODEL)),
            const((D_MODEL, D_FF)), const((D_FF, D_MODEL)),
        ],
        out_specs=pl.BlockSpec((BATCH, TT, D_MODEL), lambda c: (0, c, 0)),
        out_shape=jax.ShapeDtypeStruct((BATCH, SEQ, D_MODEL), F32),
        scratch_shapes=[
            pltpu.VMEM((FOURIER_WIDTH // LANES, ROWS, LANES), F32),
            pltpu.VMEM((D_MODEL // LANES, ROWS, LANES), F32),
        ],
        name="tail",
        compiler_params=pltpu.CompilerParams(
            dimension_semantics=("arbitrary",), vmem_limit_bytes=VMEM_LIMIT),
    )(ls, pf, pb, hf, hb, uy, ug, yf, xt, mod, mod, mod, mod, gm, gf, wl, wf, wo, w1, w2)


def kernel(x, c, ctx, c_ctx, w_mod, b_mod, g_mix, w_in, conv_w, conv_b, w_a, b_a, w_x, b_x,
           lam, w_lru_out, w_f_out, w_out, g_mlp, w1, w2, g_final):
    pos_np, chan_np, time_np = _constants()
    pos = jnp.asarray(pos_np)
    chan, time = jnp.asarray(chan_np).astype(BF16), jnp.asarray(time_np).astype(BF16)

    cc = jnp.concatenate(
        [c, c_ctx[None], jnp.zeros((2 * SUBLANES - BATCH - 1, D_MODEL), F32)], axis=0)
    mod = _mod_call(cc, w_mod[0], b_mod[0][None])

    g_mix2 = g_mix[0][None]
    wg = jnp.concatenate([w_a[0], w_x[0]], axis=-1).astype(BF16)
    lru = (conv_w[0], conv_b[0][None], wg, b_a[0], b_x[0], lam[0])

    w_in_b, e_ctx = _head_call(ctx, None, mod, g_mix2, w_in[0], *lru, lat=False)
    xt, uy, uf, ug, ls, pf, pb, e_lat = _head_call(x, pos, mod, g_mix2, w_in_b, *lru, lat=True)
    hf, hb = _carry_call(e_ctx, e_lat)
    yf, wl, wf, wo, w1b, w2b = _fourier_call(
        uf, chan, time, (w_lru_out[0], w_f_out[0], w_out[0], w1[0], w2[0]))
    return _tail_call(ls, pf, pb, hf, hb, uy, ug, yf, xt, mod,
                      g_mlp[0][None], g_final[None], wl, wf, wo, w1b, w2b)
```

```python
import functools
import math

import numpy as np
import jax
import jax.numpy as jnp
from jax import lax
from jax.experimental import pallas as pl
from jax.experimental.pallas import tpu as pltpu

F32 = jnp.float32
BF16 = jnp.bfloat16

D_MODEL = 1024
BATCH = 8
SEQ = 2048
CTX_LEN = 256
GRID_W = 64
LRU_WIDTH = 1024
LRU_HEADS = 8
LRU_HEAD_DIM = LRU_WIDTH // LRU_HEADS
LRU_C = 8.0
CONV_WIDTH = 4
CONV_PAD_LEFT = 2
FOURIER_WIDTH = 512
FOURIER_GROUPS = 4
FOURIER_GROUP_DIM = FOURIER_WIDTH // FOURIER_GROUPS
IN_COLS = 2 * LRU_WIDTH + FOURIER_WIDTH + 2 * D_MODEL
D_FF = 4 * D_MODEL
N_MOD = 6
EPS = 1e-6
POS_MAX_PERIOD = 10000.0

LANES = 128
SUBLANES = 8
BF16_ROWS = 16
assert BATCH == SUBLANES

TT = 64
ROWS = TT * BATCH
PITCH = TT + SUBLANES
HALO_T = SUBLANES
LEAD = CONV_PAD_LEFT * BATCH
TRAIL = (CONV_WIDTH - 1 - CONV_PAD_LEFT) * BATCH
XROWS = LEAD + ROWS + TRAIL
XROWS_PAD = -(-XROWS // BF16_ROWS) * BF16_ROWS
N_CTX_CHUNKS = CTX_LEN // TT
N_LAT_CHUNKS = SEQ // TT
NCHUNK = 512
FRONT_ROWS = 128
GATE_AHEAD = 1
GATE_SLOTS = GATE_AHEAD + 1
DFT_ROWS = -(-(SEQ // 4 + 1) // BF16_ROWS) * BF16_ROWS
TAIL_WEIGHT_ROWS = (LRU_WIDTH, FOURIER_WIDTH, D_MODEL, D_MODEL, D_FF)
MOD_TN = 1024
V7X_VMEM_BYTES = 64 * 1024 * 1024
VMEM_RESERVE = 6 * 1024 * 1024
VMEM_LIMIT = V7X_VMEM_BYTES - VMEM_RESERVE
LOG2E = math.log2(math.e)
TINY = 1e-30


def _gelu_tanh(x):
    k0 = jnp.full((1, 1), math.sqrt(2.0 / math.pi), F32).astype(x.dtype)
    k1 = jnp.full((1, 1), 0.044715, F32).astype(x.dtype)
    return 0.5 * x * (1.0 + jnp.tanh(k0 * (x + k1 * (x * x * x))))


def _sigmoid(x):
    return 0.5 * jnp.tanh(0.5 * x) + 0.5


def _dot(a, b):
    return jnp.dot(a, b, preferred_element_type=F32)


def _rms(x, g):
    ms = jnp.mean(x * x, axis=-1, keepdims=True)
    return x * lax.rsqrt(ms + EPS) * g


@functools.cache
def _constants():
    half = D_MODEL // 4
    freqs = np.exp(-math.log(POS_MAX_PERIOD) * np.arange(half, dtype=np.float32) / half).astype(np.float32)

    def sincos(n):
        ang = np.arange(n, dtype=np.float32)[:, None] * freqs[None, :]
        return np.concatenate([np.sin(ang), np.cos(ang)], axis=-1).astype(np.float32)

    rows = SEQ // GRID_W
    er, ec = sincos(rows), sincos(GRID_W)
    pos = np.concatenate([
        np.broadcast_to(er[:, None, :], (rows, GRID_W, D_MODEL // 2)),
        np.broadcast_to(ec[None, :, :], (rows, GRID_W, D_MODEL // 2)),
    ], axis=-1).reshape(SEQ, D_MODEL).astype(np.float32)

    def dft(n):
        k = np.arange(n, dtype=np.int64)
        ang = 2.0 * np.pi * ((k[:, None] * k[None, :]) % n).astype(np.float64) / n
        return np.cos(ang) / math.sqrt(n), np.sin(ang) / math.sqrt(n)

    cc, sc = dft(FOURIER_GROUP_DIM)
    chan = np.concatenate([cc, sc], axis=1).astype(np.float32)
    ct, st = dft(SEQ)
    half, quarter = SEQ // 2, SEQ // 4
    time = np.zeros((2, 2, DFT_ROWS, half), np.float64)
    flip = np.zeros((2, quarter, DFT_ROWS), np.float32)
    for p in range(2):
        n_rows = quarter + 1 - p
        rows = 2 * np.arange(n_rows) + p
        time[p, 0, :n_rows] = ct[rows, :half]
        time[p, 1, :n_rows] = st[rows, :half]
        flip[p, np.arange(quarter), quarter - p - np.arange(quarter)] = 1.0
    return pos, chan, time.astype(np.float32), flip


def _mod_kernel(c_ref, w_ref, b_ref, o_ref):
    c = c_ref[...]
    a = c * _sigmoid(c)
    w = w_ref[...]
    a_hi = a.astype(BF16)
    a_lo = (a - a_hi.astype(F32)).astype(BF16)
    w_hi = w.astype(BF16)
    w_lo = (w - w_hi.astype(F32)).astype(BF16)
    n = a.shape[0]
    both = _dot(jnp.concatenate([a_hi, a_lo], axis=0), w_hi)
    o_ref[...] = both[:n] + _dot(a_hi, w_lo) + both[n:] + b_ref[...]


def _mod_call(cc, w_mod, b_mod):
    n = w_mod.shape[1]
    return pl.pallas_call(
        _mod_kernel,
        grid=(n // MOD_TN,),
        in_specs=[
            pl.BlockSpec((2 * SUBLANES, D_MODEL), lambda i: (0, 0)),
            pl.BlockSpec((D_MODEL, MOD_TN), lambda i: (0, i)),
            pl.BlockSpec((1, MOD_TN), lambda i: (0, i)),
        ],
        out_specs=pl.BlockSpec((2 * SUBLANES, MOD_TN), lambda i: (0, i)),
        out_shape=jax.ShapeDtypeStruct((2 * SUBLANES, n), F32),
        name="mod",
        compiler_params=pltpu.CompilerParams(dimension_semantics=("arbitrary",)),
    )(cc, w_mod, b_mod)


def _mod_spec(row_block, k):
    return pl.BlockSpec((BATCH, D_MODEL), lambda c: (row_block, k))


def _head_kernel(*refs, lat, n_chunks):
    refs = list(refs)
    xm_ref, xp_ref, xn_ref = refs[:3]
    refs = refs[3:]
    if lat:
        pm_ref, pp_ref, pn_ref = refs[:3]
        refs = refs[3:]
    (sh_ref, sc_ref, g_ref, w_ref, cw_ref, cb_ref, wg_ref, ba_ref, bx_ref, lam_ref) = refs[:10]
    refs = refs[10:]
    if lat:
        xt_out, uy_out, uf_out, ug_out, ls_out, pf_out, pb_out = refs[:7]
        refs = refs[7:]
    else:
        w_slab_ref, w_bf16_out = refs[:2]
        refs = refs[2:]
        w_bf16_out[...] = w_slab_ref[...].astype(BF16)
    e_out, slab_ref, hslab_ref, xs_ref, hb_ref, a_ref, b_ref, xh_ref, gate_ref = refs

    c = pl.program_id(0)
    n_slab = D_MODEL // LANES
    mod_rows = slice(None) if lat else slice(0, 1)

    xm = xm_ref[...]
    xp = xp_ref[...]
    xn = xn_ref[...]
    if lat:
        xm = xm + pm_ref[...][None]
        xp = xp + pp_ref[...][None]
        xn = xn + pn_ref[...][None]
    for b in range(BATCH):
        for j in range(n_slab):
            ls = slice(j * LANES, (j + 1) * LANES)
            slab_ref[j, b * PITCH:b * PITCH + TT, :] = xm[b, :, ls]
            hslab_ref[0, j, b * HALO_T:(b + 1) * HALO_T, :] = xp[b, :, ls]
            hslab_ref[1, j, b * HALO_T:(b + 1) * HALO_T, :] = xn[b, :, ls]
    for j in range(n_slab):
        ls = slice(j * LANES, (j + 1) * LANES)
        for k in range(CONV_PAD_LEFT):
            xs_ref[k * BATCH:(k + 1) * BATCH, ls] = (
                hslab_ref[0, j, pl.ds(HALO_T - CONV_PAD_LEFT + k, BATCH, stride=HALO_T), :])
        for t in range(TT):
            xs_ref[LEAD + t * BATCH:LEAD + (t + 1) * BATCH, ls] = slab_ref[j, pl.ds(t, BATCH, stride=PITCH), :]
        for k in range(CONV_WIDTH - 1 - CONV_PAD_LEFT):
            r0 = LEAD + ROWS + k * BATCH
            xs_ref[r0:r0 + BATCH, ls] = hslab_ref[1, j, pl.ds(k, BATCH, stride=HALO_T), :]
    xs_ref[XROWS:XROWS_PAD, :] = jnp.zeros((XROWS_PAD - XROWS, D_MODEL), F32)

    assert LEAD % BF16_ROWS == 0 and FRONT_ROWS % BF16_ROWS == 0
    blocks = ([(0, LEAD)] + [(r, r + FRONT_ROWS) for r in range(LEAD, LEAD + ROWS, FRONT_ROWS)]
              + [(LEAD + ROWS, XROWS_PAD)])
    for r0, r1 in blocks:
        xt = xs_ref[r0:r1, :]
        if lat and LEAD <= r0 < LEAD + ROWS:
            xt_out[r0 - LEAD:r1 - LEAD, :] = xt
        h = _rms(xt, g_ref[...]).reshape((r1 - r0) // BATCH, BATCH, D_MODEL)
        h = h * (1.0 + sc_ref[mod_rows, :][None]) + sh_ref[mod_rows, :][None]
        hb_ref[r0:r1, :] = h.reshape(r1 - r0, D_MODEL).astype(BF16)

    keep_first = jnp.where(c == 0, 0.0, 1.0)
    keep_last = jnp.where(c == n_chunks - 1, 0.0, 1.0)
    for n0 in range(0, LRU_WIDTH, NCHUNK):
        w_x = w_ref[:, n0:n0 + NCHUNK]
        xs_ref[:, n0:n0 + NCHUNK] = _dot(hb_ref[...], w_x if lat else w_x.astype(BF16))
    xs_ref[0:LEAD, :] = xs_ref[0:LEAD, :] * keep_first
    xs_ref[LEAD + ROWS:XROWS, :] = xs_ref[LEAD + ROWS:XROWS, :] * keep_last

    def branch_piece(n0):
        r = _dot(hb_ref[LEAD:LEAD + ROWS, :], w_ref[:, n0:n0 + NCHUNK])
        if n0 < 2 * LRU_WIDTH:
            uy_out[:, n0 - LRU_WIDTH:n0 - LRU_WIDTH + NCHUNK] = r.astype(BF16)
        elif n0 < 2 * LRU_WIDTH + FOURIER_WIDTH:
            for s in range(NCHUNK // LANES):
                slab_ref[s, 0:ROWS, :] = r[:, s * LANES:(s + 1) * LANES]
            for b in range(BATCH):
                for s in range(NCHUNK // LANES):
                    c0 = b * FOURIER_WIDTH + (n0 - 2 * LRU_WIDTH) + s * LANES
                    uf_out[:, c0:c0 + LANES] = slab_ref[s, pl.ds(b, TT, stride=BATCH), :].astype(BF16)
        else:
            c0 = n0 - 2 * LRU_WIDTH - FOURIER_WIDTH
            ug_out[:, c0:c0 + NCHUNK] = r.astype(BF16)

    pieces = list(range(LRU_WIDTH, IN_COLS, NCHUNK)) if lat else []
    per_head = -(-len(pieces) // LRU_HEADS)

    lam = lam_ref[...]
    sp = jnp.maximum(-lam, 0.0) + jnp.log1p(jnp.exp(-jnp.abs(lam)))
    c1 = (-0.5 * LRU_C * LOG2E) * sp

    def conv_gates(hd):
        sl = slice(hd * LRU_HEAD_DIM, (hd + 1) * LRU_HEAD_DIM)
        xh = 0.5 * cb_ref[:, sl]
        for k in range(CONV_WIDTH):
            xh = xh + (0.5 * cw_ref[k:k + 1, sl]) * xs_ref[k * BATCH:k * BATCH + ROWS, sl]
        xh_ref[hd % GATE_SLOTS] = xh
        xhb = xh.astype(BF16)
        for d in range(2):
            gate_ref[hd % GATE_SLOTS, d] = _dot(xhb, wg_ref[d, hd])

    for hd in range(GATE_AHEAD):
        conv_gates(hd)
    def coefficients(hd):
        sl = slice(hd * LRU_HEAD_DIM, (hd + 1) * LRU_HEAD_DIM)
        xh = xh_ref[hd % GATE_SLOTS]
        for d in range(2):
            g = gate_ref[hd % GATE_SLOTS, d]
            ta = jnp.tanh(g[:, :LRU_HEAD_DIM] + 0.5 * ba_ref[d:d + 1, sl])
            ti = jnp.tanh(g[:, LRU_HEAD_DIM:] + 0.5 * bx_ref[d:d + 1, sl])
            a = jnp.exp2(c1[d:d + 1, sl] * ta + c1[d:d + 1, sl])
            v = 1.0 - a * a
            m = v * lax.rsqrt(jnp.maximum(v, TINY))
            a_ref[d, :, sl] = a
            b_ref[d, :, sl] = (m * xh) * (ti + 1.0)

    def scans(hd):
        sl = slice(hd * LRU_HEAD_DIM, (hd + 1) * LRU_HEAD_DIM)
        lf = pf = lb = pb = None
        for t in range(TT):
            rf = slice(t * BATCH, (t + 1) * BATCH)
            rb = slice((TT - 1 - t) * BATCH, (TT - t) * BATCH)
            af, bf = a_ref[0, rf, sl], b_ref[0, rf, sl]
            ab, bb = a_ref[1, rb, sl], b_ref[1, rb, sl]
            if t == 0:
                lf, pf, lb, pb = bf, af, bb, ab
            else:
                lf, pf = af * lf + bf, af * pf
                lb, pb = ab * lb + bb, ab * pb
                b_ref[0, rf, sl] = lf
                a_ref[0, rf, sl] = pf
                b_ref[1, rb, sl] = lb
                a_ref[1, rb, sl] = pb
        e_out[0, :, sl] = lf
        e_out[1, :, sl] = pf
        e_out[2, :, sl] = lb
        e_out[3, :, sl] = pb
        if lat:
            ls_out[:, sl] = (b_ref[0, :, sl] + b_ref[1, :, sl]).astype(BF16)
            pf_out[:, sl] = a_ref[0, :, sl].astype(BF16)
            pb_out[:, sl] = a_ref[1, :, sl].astype(BF16)

    for hd in range(LRU_HEADS):
        for n0 in pieces[hd * per_head:(hd + 1) * per_head]:
            branch_piece(n0)
        coefficients(hd)
        if hd + GATE_AHEAD < LRU_HEADS:
            conv_gates(hd + GATE_AHEAD)
        scans(hd)


def _head_call(x, pos, mod, g, w, conv_w, conv_b, wg, b_a, b_x, lam, *, lat):
    seq = x.shape[1]
    n_chunks = seq // TT
    n_halo = seq // HALO_T
    per_chunk = TT // HALO_T

    def const(shape):
        return pl.BlockSpec(shape, lambda c: (0,) * len(shape))

    def prev_idx(c):
        return jnp.maximum(c * per_chunk - 1, 0)

    def next_idx(c):
        return jnp.minimum((c + 1) * per_chunk, n_halo - 1)

    in_specs = [
        pl.BlockSpec((BATCH, TT, D_MODEL), lambda c: (0, c, 0)),
        pl.BlockSpec((BATCH, HALO_T, D_MODEL), lambda c: (0, prev_idx(c), 0)),
        pl.BlockSpec((BATCH, HALO_T, D_MODEL), lambda c: (0, next_idx(c), 0)),
    ]
    args = [x, x, x]
    if lat:
        in_specs += [
            pl.BlockSpec((TT, D_MODEL), lambda c: (c, 0)),
            pl.BlockSpec((HALO_T, D_MODEL), lambda c: (prev_idx(c), 0)),
            pl.BlockSpec((HALO_T, D_MODEL), lambda c: (next_idx(c), 0)),
        ]
        args += [pos, pos, pos]
    in_specs += [
        _mod_spec(0 if lat else 1, 0), _mod_spec(0 if lat else 1, 1), const((1, D_MODEL)),
        const((D_MODEL, IN_COLS if lat else LRU_WIDTH)),
        const((CONV_WIDTH, LRU_WIDTH)), const((1, LRU_WIDTH)),
        const((2, LRU_HEADS, LRU_HEAD_DIM, 2 * LRU_HEAD_DIM)),
        const((2, LRU_WIDTH)), const((2, LRU_WIDTH)), const((2, LRU_WIDTH)),
    ]
    args += [mod, mod, g, w, conv_w, conv_b, wg, b_a, b_x, lam]
    w_slab = pl.BlockSpec((D_MODEL // n_chunks, IN_COLS), lambda c: (c, 0))
    if not lat:
        in_specs.append(w_slab)
        args.append(w)

    def rows(width):
        return pl.BlockSpec((ROWS, width), lambda c: (c, 0))

    out_specs, out_shape = [], []
    if lat:
        n = n_chunks * ROWS
        out_specs += [rows(D_MODEL), rows(LRU_WIDTH),
                      pl.BlockSpec((TT, BATCH * FOURIER_WIDTH), lambda c: (c, 0)),
                      rows(2 * D_MODEL), rows(LRU_WIDTH), rows(LRU_WIDTH), rows(LRU_WIDTH)]
        out_shape += [
            jax.ShapeDtypeStruct((n, D_MODEL), F32),
            jax.ShapeDtypeStruct((n, LRU_WIDTH), BF16),
            jax.ShapeDtypeStruct((seq, BATCH * FOURIER_WIDTH), BF16),
            jax.ShapeDtypeStruct((n, 2 * D_MODEL), BF16),
            jax.ShapeDtypeStruct((n, LRU_WIDTH), BF16),
            jax.ShapeDtypeStruct((n, LRU_WIDTH), BF16),
            jax.ShapeDtypeStruct((n, LRU_WIDTH), BF16),
        ]
    else:
        out_specs.append(w_slab)
        out_shape.append(jax.ShapeDtypeStruct((D_MODEL, IN_COLS), BF16))
    out_specs.append(pl.BlockSpec((None, 4, BATCH, LRU_WIDTH), lambda c: (c, 0, 0, 0)))
    out_shape.append(jax.ShapeDtypeStruct((n_chunks, 4, BATCH, LRU_WIDTH), F32))

    return pl.pallas_call(
        functools.partial(_head_kernel, lat=lat, n_chunks=n_chunks),
        grid=(n_chunks,),
        in_specs=in_specs,
        out_specs=out_specs,
        out_shape=out_shape,
        scratch_shapes=[
            pltpu.VMEM((D_MODEL // LANES, BATCH * PITCH, LANES), F32),
            pltpu.VMEM((2, D_MODEL // LANES, BATCH * HALO_T, LANES), F32),
            pltpu.VMEM((XROWS_PAD, D_MODEL), F32),
            pltpu.VMEM((XROWS_PAD, D_MODEL), BF16),
            pltpu.VMEM((2, ROWS, LRU_WIDTH), F32),
            pltpu.VMEM((2, ROWS, LRU_WIDTH), F32),
            pltpu.VMEM((GATE_SLOTS, ROWS, LRU_HEAD_DIM), F32),
            pltpu.VMEM((GATE_SLOTS, 2, ROWS, 2 * LRU_HEAD_DIM), F32),
        ],
        name="head_lat" if lat else "head_ctx",
        compiler_params=pltpu.CompilerParams(
            dimension_semantics=("arbitrary",), vmem_limit_bytes=VMEM_LIMIT),
    )(*args)


def _carry_kernel(ec_ref, el_ref, hf_ref, hb_ref):
    n_ctx, n_lat = ec_ref.shape[0], el_ref.shape[0]
    h = jnp.zeros((BATCH, LRU_WIDTH), F32)
    for c in range(n_ctx):
        h = ec_ref[c, 1] * h + ec_ref[c, 0]
    for c in range(n_lat):
        hf_ref[c] = h
        h = el_ref[c, 1] * h + el_ref[c, 0]
    h = jnp.zeros((BATCH, LRU_WIDTH), F32)
    for c in reversed(range(n_ctx)):
        h = ec_ref[c, 3] * h + ec_ref[c, 2]
    for c in reversed(range(n_lat)):
        hb_ref[c] = h
        h = el_ref[c, 3] * h + el_ref[c, 2]


def _carry_call(e_ctx, e_lat):
    shape = jax.ShapeDtypeStruct((e_lat.shape[0], BATCH, LRU_WIDTH), F32)
    return pl.pallas_call(_carry_kernel, out_shape=[shape, shape], name="carry")(e_ctx, e_lat)


def _fourier_kernel(x_ref, chan_ref, time_ref, flip_ref, *refs):
    n_w = len(TAIL_WEIGHT_ROWS)
    w_f32_refs, o_ref, w_bf16_refs, z_ref = refs[:n_w], refs[n_w], refs[n_w + 1:2 * n_w + 1], refs[-1]
    for src, dst in zip(w_f32_refs, w_bf16_refs):
        dst[...] = src[...].astype(BF16)

    half, quarter = SEQ // 2, SEQ // 4
    for g in range(FOURIER_GROUPS):
        sl = slice(g * FOURIER_GROUP_DIM, (g + 1) * FOURIER_GROUP_DIM)
        r = _dot(x_ref[:, sl], chan_ref[...])
        lo, hi = r[:half], r[half:]
        for p, v in enumerate((lo + hi, lo - hi)):
            z_ref[p, 0:half, sl] = v[:, :FOURIER_GROUP_DIM].astype(BF16)
            z_ref[p, half:SEQ, sl] = v[:, FOURIER_GROUP_DIM:].astype(BF16)
    for p in range(2):
        cos_part = _dot(time_ref[p, 0], z_ref[p, 0:half, :])
        sin_part = _dot(time_ref[p, 1], z_ref[p, half:SEQ, :])
        o_ref[p, 0:quarter, :] = (cos_part - sin_part)[0:quarter].astype(BF16)
        mirrored = (cos_part + sin_part).astype(BF16)
        o_ref[p, quarter:half, :] = _dot(flip_ref[p], mirrored).astype(BF16)


def _fourier_call(uf, chan, time, flip, weights):
    def slab(w):
        return pl.BlockSpec((w.shape[0] // BATCH, w.shape[1]), lambda n: (n, 0))

    assert tuple(w.shape[0] for w in weights) == TAIL_WEIGHT_ROWS
    return pl.pallas_call(
        _fourier_kernel,
        grid=(BATCH,),
        in_specs=[
            pl.BlockSpec((SEQ, FOURIER_WIDTH), lambda n: (0, n)),
            pl.BlockSpec((FOURIER_GROUP_DIM, 2 * FOURIER_GROUP_DIM), lambda n: (0, 0)),
            pl.BlockSpec((2, 2, DFT_ROWS, SEQ // 2), lambda n: (0, 0, 0, 0)),
            pl.BlockSpec((2, SEQ // 4, DFT_ROWS), lambda n: (0, 0, 0)),
        ] + [slab(w) for w in weights],
        out_specs=[pl.BlockSpec((2, SEQ // 2, FOURIER_WIDTH), lambda n: (0, 0, n))]
        + [slab(w) for w in weights],
        out_shape=[jax.ShapeDtypeStruct((2, SEQ // 2, BATCH * FOURIER_WIDTH), BF16)]
        + [jax.ShapeDtypeStruct(w.shape, BF16) for w in weights],
        scratch_shapes=[pltpu.VMEM((2, SEQ, FOURIER_WIDTH), BF16)],
        name="fourier",
        compiler_params=pltpu.CompilerParams(
            dimension_semantics=("arbitrary",), vmem_limit_bytes=VMEM_LIMIT),
    )(uf, chan, time, flip, *weights)


def _tail_kernel(ls_ref, pf_ref, pb_ref, hf_ref, hb_ref, uy_ref, ug_ref, yf_ref, xt_ref,
                 gt1_ref, sh2_ref, sc2_ref, gt2_ref, gm_ref, gf_ref,
                 wl_ref, wf_ref, wo_ref, w1_ref, w2_ref, o_ref, fslab_ref, oslab_ref):
    def tb(v):
        return v.reshape(TT, BATCH, v.shape[-1])

    def flat(v):
        return v.reshape(ROWS, v.shape[-1])

    for b in range(BATCH):
        for p in range(2):
            v = yf_ref[p, :, b * FOURIER_WIDTH:(b + 1) * FOURIER_WIDTH].astype(F32)
            for s in range(FOURIER_WIDTH // LANES):
                fslab_ref[s, pl.ds(p * BATCH + b, TT // 2, stride=2 * BATCH), :] = (
                    v[:, s * LANES:(s + 1) * LANES])
    yf = jnp.concatenate([fslab_ref[s] for s in range(FOURIER_WIDTH // LANES)], axis=-1).astype(BF16)
    y_b = _dot(yf, wf_ref[...])

    hf = jnp.tile(hf_ref[...].astype(BF16), (TT, 1))
    hb = jnp.tile(hb_ref[...].astype(BF16), (TT, 1))
    y_lru = ls_ref[...] + pf_ref[...] * hf + pb_ref[...] * hb
    z = y_lru * _gelu_tanh(uy_ref[...])
    y_a = _dot(z, wl_ref[...])

    g_a = _sigmoid(ug_ref[:, :D_MODEL].astype(F32))
    g_b = _sigmoid(ug_ref[:, D_MODEL:].astype(F32))
    merged = (g_a * y_a + g_b * y_b).astype(BF16)
    x2 = tb(xt_ref[...]) + gt1_ref[...][None] * tb(_dot(merged, wo_ref[...]))

    h = (tb(_rms(flat(x2), gm_ref[...])) * (1.0 + sc2_ref[...][None]) + sh2_ref[...][None])
    h = flat(h).astype(BF16)
    acc = jnp.zeros((ROWS, D_MODEL), F32)
    for k0 in range(0, D_FF, D_MODEL):
        a = jnp.maximum(_dot(h, w1_ref[:, k0:k0 + D_MODEL]), 0.0)
        acc = acc + _dot((a * a).astype(BF16), w2_ref[k0:k0 + D_MODEL, :])
    out = _rms(flat(x2 + gt2_ref[...][None] * tb(acc)), gf_ref[...])

    for j in range(D_MODEL // LANES):
        oslab_ref[j] = out[:, j * LANES:(j + 1) * LANES]
    for b in range(BATCH):
        for j in range(D_MODEL // LANES):
            o_ref[b, :, j * LANES:(j + 1) * LANES] = oslab_ref[j, pl.ds(b, TT, stride=BATCH), :]


def _tail_call(ls, pf, pb, hf, hb, uy, ug, yf, xt, mod, gm, gf, wl, wf, wo, w1, w2):
    def rows(width):
        return pl.BlockSpec((ROWS, width), lambda c: (c, 0))

    def const(shape):
        return pl.BlockSpec(shape, lambda c: (0,) * len(shape))

    state = pl.BlockSpec((None, BATCH, LRU_WIDTH), lambda c: (c, 0, 0))
    vec1 = const((1, D_MODEL))
    return pl.pallas_call(
        _tail_kernel,
        grid=(N_LAT_CHUNKS,),
        in_specs=[
            rows(LRU_WIDTH), rows(LRU_WIDTH), rows(LRU_WIDTH), state, state,
            rows(LRU_WIDTH), rows(2 * D_MODEL),
            pl.BlockSpec((2, TT // 2, BATCH * FOURIER_WIDTH), lambda c: (0, c, 0)),
            rows(D_MODEL),
            _mod_spec(0, 2), _mod_spec(0, 3), _mod_spec(0, 4), _mod_spec(0, 5), vec1, vec1,
            const((LRU_WIDTH, D_MODEL)), const((FOURIER_WIDTH, D_MODEL)), const((D_MODEL, D_MODEL)),
            const((D_MODEL, D_FF)), const((D_FF, D_MODEL)),
        ],
        out_specs=pl.BlockSpec((BATCH, TT, D_MODEL), lambda c: (0, c, 0)),
        out_shape=jax.ShapeDtypeStruct((BATCH, SEQ, D_MODEL), F32),
        scratch_shapes=[
            pltpu.VMEM((FOURIER_WIDTH // LANES, ROWS, LANES), F32),
            pltpu.VMEM((D_MODEL // LANES, ROWS, LANES), F32),
        ],
        name="tail",
        compiler_params=pltpu.CompilerParams(
            dimension_semantics=("arbitrary",), vmem_limit_bytes=VMEM_LIMIT),
    )(ls, pf, pb, hf, hb, uy, ug, yf, xt, mod, mod, mod, mod, gm, gf, wl, wf, wo, w1, w2)


def kernel(x, c, ctx, c_ctx, w_mod, b_mod, g_mix, w_in, conv_w, conv_b, w_a, b_a, w_x, b_x,
           lam, w_lru_out, w_f_out, w_out, g_mlp, w1, w2, g_final):
    pos_np, chan_np, time_np, flip_np = _constants()
    pos = jnp.asarray(pos_np)
    chan, time, flip = (jnp.asarray(v).astype(BF16) for v in (chan_np, time_np, flip_np))

    cc = jnp.concatenate(
        [c, c_ctx[None], jnp.zeros((2 * SUBLANES - BATCH - 1, D_MODEL), F32)], axis=0)
    mod = _mod_call(cc, w_mod[0], b_mod[0][None])

    g_mix2 = g_mix[0][None]
    wg = jnp.concatenate([w_a[0], w_x[0]], axis=-1).astype(BF16)
    lru = (conv_w[0], conv_b[0][None], wg, b_a[0], b_x[0], lam[0])

    w_in_b, e_ctx = _head_call(ctx, None, mod, g_mix2, w_in[0], *lru, lat=False)
    xt, uy, uf, ug, ls, pf, pb, e_lat = _head_call(x, pos, mod, g_mix2, w_in_b, *lru, lat=True)
    hf, hb = _carry_call(e_ctx, e_lat)
    yf, wl, wf, wo, w1b, w2b = _fourier_call(
        uf, chan, time, flip, (w_lru_out[0], w_f_out[0], w_out[0], w1[0], w2[0]))
    return _tail_call(ls, pf, pb, hf, hb, uy, ug, yf, xt, mod,
                      g_mlp[0][None], g_final[None], wl, wf, wo, w1b, w2b)
```

```python
import functools
import math

import numpy as np
import jax
import jax.numpy as jnp
from jax import lax
from jax.experimental import pallas as pl
from jax.experimental.pallas import tpu as pltpu

F32 = jnp.float32
BF16 = jnp.bfloat16

D_MODEL = 1024
BATCH = 8
SEQ = 2048
CTX_LEN = 256
GRID_W = 64
LRU_WIDTH = 1024
LRU_HEADS = 8
LRU_HEAD_DIM = LRU_WIDTH // LRU_HEADS
LRU_C = 8.0
CONV_WIDTH = 4
CONV_PAD_LEFT = 2
FOURIER_WIDTH = 512
FOURIER_GROUPS = 4
FOURIER_GROUP_DIM = FOURIER_WIDTH // FOURIER_GROUPS
IN_COLS = 2 * LRU_WIDTH + FOURIER_WIDTH + 2 * D_MODEL
D_FF = 4 * D_MODEL
N_MOD = 6
EPS = 1e-6
POS_MAX_PERIOD = 10000.0

SUBLANES = 8
BF16_ROWS = 16
assert BATCH == SUBLANES

TT = 64
ROWS = TT * BATCH
HALO_T = SUBLANES
LEAD = CONV_PAD_LEFT * BATCH
TRAIL = (CONV_WIDTH - 1 - CONV_PAD_LEFT) * BATCH
XROWS = LEAD + ROWS + TRAIL
XROWS_PAD = -(-XROWS // BF16_ROWS) * BF16_ROWS
N_CTX_CHUNKS = CTX_LEN // TT
N_LAT_CHUNKS = SEQ // TT
NCHUNK = 512
FRONT_ROWS = 128
GATE_AHEAD = 1
GATE_SLOTS = GATE_AHEAD + 1
DFT_ROWS = -(-(SEQ // 4 + 1) // BF16_ROWS) * BF16_ROWS
TAIL_WEIGHT_ROWS = (LRU_WIDTH, FOURIER_WIDTH, D_MODEL, D_MODEL, D_FF)
MOD_TN = 1024
V7X_VMEM_BYTES = 64 * 1024 * 1024
VMEM_RESERVE = 6 * 1024 * 1024
VMEM_LIMIT = V7X_VMEM_BYTES - VMEM_RESERVE
LOG2E = math.log2(math.e)
TINY = 1e-30


def _gelu_tanh(x):
    k0 = jnp.full((1, 1), math.sqrt(2.0 / math.pi), F32).astype(x.dtype)
    k1 = jnp.full((1, 1), 0.044715, F32).astype(x.dtype)
    return 0.5 * x * (1.0 + jnp.tanh(k0 * (x + k1 * (x * x * x))))


def _sigmoid(x):
    return 0.5 * jnp.tanh(0.5 * x) + 0.5


def _dot(a, b):
    return jnp.dot(a, b, preferred_element_type=F32)


def _rms(x, g):
    ms = jnp.mean(x * x, axis=-1, keepdims=True)
    return x * lax.rsqrt(ms + EPS) * g


@functools.cache
def _constants():
    half = D_MODEL // 4
    freqs = np.exp(-math.log(POS_MAX_PERIOD) * np.arange(half, dtype=np.float32) / half).astype(np.float32)

    def sincos(n):
        ang = np.arange(n, dtype=np.float32)[:, None] * freqs[None, :]
        return np.concatenate([np.sin(ang), np.cos(ang)], axis=-1).astype(np.float32)

    rows = SEQ // GRID_W
    er, ec = sincos(rows), sincos(GRID_W)
    pos = np.concatenate([
        np.broadcast_to(er[:, None, :], (rows, GRID_W, D_MODEL // 2)),
        np.broadcast_to(ec[None, :, :], (rows, GRID_W, D_MODEL // 2)),
    ], axis=-1).reshape(SEQ, D_MODEL).astype(np.float32)

    def dft(n):
        k = np.arange(n, dtype=np.int64)
        ang = 2.0 * np.pi * ((k[:, None] * k[None, :]) % n).astype(np.float64) / n
        return np.cos(ang) / math.sqrt(n), np.sin(ang) / math.sqrt(n)

    cc, sc = dft(FOURIER_GROUP_DIM)
    chan = np.concatenate([cc, sc], axis=1).astype(np.float32)
    ct, st = dft(SEQ)
    half, quarter = SEQ // 2, SEQ // 4
    time = np.zeros((2, 2, DFT_ROWS, half), np.float64)
    flip = np.zeros((2, quarter, DFT_ROWS), np.float32)
    for p in range(2):
        n_rows = quarter + 1 - p
        rows = 2 * np.arange(n_rows) + p
        time[p, 0, :n_rows] = ct[rows, :half]
        time[p, 1, :n_rows] = st[rows, :half]
        flip[p, np.arange(quarter), quarter - p - np.arange(quarter)] = 1.0
    return pos, chan, time.astype(np.float32), flip


def _mod_kernel(c_ref, w_ref, b_ref, o_ref):
    c = c_ref[...]
    a = c * _sigmoid(c)
    w = w_ref[...]
    a_hi = a.astype(BF16)
    a_lo = (a - a_hi.astype(F32)).astype(BF16)
    w_hi = w.astype(BF16)
    w_lo = (w - w_hi.astype(F32)).astype(BF16)
    n = a.shape[0]
    both = _dot(jnp.concatenate([a_hi, a_lo], axis=0), w_hi)
    o_ref[...] = both[:n] + _dot(a_hi, w_lo) + both[n:] + b_ref[...]


def _mod_call(cc, w_mod, b_mod):
    n = w_mod.shape[1]
    return pl.pallas_call(
        _mod_kernel,
        grid=(n // MOD_TN,),
        in_specs=[
            pl.BlockSpec((2 * SUBLANES, D_MODEL), lambda i: (0, 0)),
            pl.BlockSpec((D_MODEL, MOD_TN), lambda i: (0, i)),
            pl.BlockSpec((1, MOD_TN), lambda i: (0, i)),
        ],
        out_specs=pl.BlockSpec((2 * SUBLANES, MOD_TN), lambda i: (0, i)),
        out_shape=jax.ShapeDtypeStruct((2 * SUBLANES, n), F32),
        name="mod",
        compiler_params=pltpu.CompilerParams(dimension_semantics=("arbitrary",)),
    )(cc, w_mod, b_mod)


def _mod_spec(row_block, k):
    return pl.BlockSpec((BATCH, D_MODEL), lambda c: (row_block, k))


def _head_kernel(*refs, lat, n_chunks):
    refs = list(refs)
    xm_ref, xp_ref, xn_ref = refs[:3]
    refs = refs[3:]
    if lat:
        pm_ref, pp_ref, pn_ref = refs[:3]
        refs = refs[3:]
    (sh_ref, sc_ref, g_ref, w_ref, cw_ref, cb_ref, wg_ref, ba_ref, bx_ref, lam_ref) = refs[:10]
    refs = refs[10:]
    if lat:
        xt_out, uy_out, uf_out, ug_out, ls_out, pf_out, pb_out = refs[:7]
        refs = refs[7:]
    else:
        w_slab_ref, w_bf16_out = refs[:2]
        refs = refs[2:]
        w_bf16_out[...] = w_slab_ref[...].astype(BF16)
    e_out, xs_ref, hb_ref, a_ref, b_ref, xh_ref, gate_ref = refs

    c = pl.program_id(0)
    mod_rows = slice(None) if lat else slice(0, 1)

    xm = xm_ref[...]
    xp = xp_ref[...]
    xn = xn_ref[...]
    if lat:
        xm = xm + pm_ref[...][None]
        xp = xp + pp_ref[...][None]
        xn = xn + pn_ref[...][None]
    xs_ref[LEAD:LEAD + ROWS, :] = pltpu.einshape("btd->tbd", xm).reshape(ROWS, D_MODEL)
    xs_ref[0:LEAD, :] = pltpu.einshape("btd->tbd", xp)[HALO_T - CONV_PAD_LEFT:].reshape(LEAD, D_MODEL)
    xs_ref[LEAD + ROWS:XROWS, :] = (
        pltpu.einshape("btd->tbd", xn)[:CONV_WIDTH - 1 - CONV_PAD_LEFT].reshape(TRAIL, D_MODEL))
    xs_ref[XROWS:XROWS_PAD, :] = jnp.zeros((XROWS_PAD - XROWS, D_MODEL), F32)

    assert LEAD % BF16_ROWS == 0 and FRONT_ROWS % BF16_ROWS == 0
    blocks = ([(0, LEAD)] + [(r, r + FRONT_ROWS) for r in range(LEAD, LEAD + ROWS, FRONT_ROWS)]
              + [(LEAD + ROWS, XROWS_PAD)])
    for r0, r1 in blocks:
        xt = xs_ref[r0:r1, :]
        if lat and LEAD <= r0 < LEAD + ROWS:
            xt_out[r0 - LEAD:r1 - LEAD, :] = xt
        h = _rms(xt, g_ref[...]).reshape((r1 - r0) // BATCH, BATCH, D_MODEL)
        h = h * (1.0 + sc_ref[mod_rows, :][None]) + sh_ref[mod_rows, :][None]
        hb_ref[r0:r1, :] = h.reshape(r1 - r0, D_MODEL).astype(BF16)

    keep_first = jnp.where(c == 0, 0.0, 1.0)
    keep_last = jnp.where(c == n_chunks - 1, 0.0, 1.0)
    for n0 in range(0, LRU_WIDTH, NCHUNK):
        w_x = w_ref[:, n0:n0 + NCHUNK]
        xs_ref[:, n0:n0 + NCHUNK] = _dot(hb_ref[...], w_x if lat else w_x.astype(BF16))
    xs_ref[0:LEAD, :] = xs_ref[0:LEAD, :] * keep_first
    xs_ref[LEAD + ROWS:XROWS, :] = xs_ref[LEAD + ROWS:XROWS, :] * keep_last

    def branch_piece(n0):
        r = _dot(hb_ref[LEAD:LEAD + ROWS, :], w_ref[:, n0:n0 + NCHUNK])
        if n0 < 2 * LRU_WIDTH:
            uy_out[:, n0 - LRU_WIDTH:n0 - LRU_WIDTH + NCHUNK] = r.astype(BF16)
        elif n0 < 2 * LRU_WIDTH + FOURIER_WIDTH:
            assert NCHUNK == FOURIER_WIDTH
            uf_out[...] = pltpu.einshape("tbd->t(bd)", r.reshape(TT, BATCH, NCHUNK)).astype(BF16)
        else:
            c0 = n0 - 2 * LRU_WIDTH - FOURIER_WIDTH
            ug_out[:, c0:c0 + NCHUNK] = r.astype(BF16)

    pieces = list(range(LRU_WIDTH, IN_COLS, NCHUNK)) if lat else []
    per_head = -(-len(pieces) // LRU_HEADS)

    lam = lam_ref[...]
    sp = jnp.maximum(-lam, 0.0) + jnp.log1p(jnp.exp(-jnp.abs(lam)))
    c1 = (-0.5 * LRU_C * LOG2E) * sp

    def conv_gates(hd):
        sl = slice(hd * LRU_HEAD_DIM, (hd + 1) * LRU_HEAD_DIM)
        xh = 0.5 * cb_ref[:, sl]
        for k in range(CONV_WIDTH):
            xh = xh + (0.5 * cw_ref[k:k + 1, sl]) * xs_ref[k * BATCH:k * BATCH + ROWS, sl]
        xh_ref[hd % GATE_SLOTS] = xh
        xhb = xh.astype(BF16)
        for d in range(2):
            gate_ref[hd % GATE_SLOTS, d] = _dot(xhb, wg_ref[d, hd])

    for hd in range(GATE_AHEAD):
        conv_gates(hd)
    def coefficients(hd):
        sl = slice(hd * LRU_HEAD_DIM, (hd + 1) * LRU_HEAD_DIM)
        xh = xh_ref[hd % GATE_SLOTS]
        for d in range(2):
            g = gate_ref[hd % GATE_SLOTS, d]
            ta = jnp.tanh(g[:, :LRU_HEAD_DIM] + 0.5 * ba_ref[d:d + 1, sl])
            ti = jnp.tanh(g[:, LRU_HEAD_DIM:] + 0.5 * bx_ref[d:d + 1, sl])
            a = jnp.exp2(c1[d:d + 1, sl] * ta + c1[d:d + 1, sl])
            v = 1.0 - a * a
            m = v * lax.rsqrt(jnp.maximum(v, TINY))
            a_ref[d, :, sl] = a
            b_ref[d, :, sl] = (m * xh) * (ti + 1.0)

    def scans(hd):
        sl = slice(hd * LRU_HEAD_DIM, (hd + 1) * LRU_HEAD_DIM)
        lf = pf = lb = pb = None
        for t in range(TT):
            rf = slice(t * BATCH, (t + 1) * BATCH)
            rb = slice((TT - 1 - t) * BATCH, (TT - t) * BATCH)
            af, bf = a_ref[0, rf, sl], b_ref[0, rf, sl]
            ab, bb = a_ref[1, rb, sl], b_ref[1, rb, sl]
            if t == 0:
                lf, pf, lb, pb = bf, af, bb, ab
            else:
                lf, pf = af * lf + bf, af * pf
                lb, pb = ab * lb + bb, ab * pb
                b_ref[0, rf, sl] = lf
                a_ref[0, rf, sl] = pf
                b_ref[1, rb, sl] = lb
                a_ref[1, rb, sl] = pb
        e_out[0, :, sl] = lf
        e_out[1, :, sl] = pf
        e_out[2, :, sl] = lb
        e_out[3, :, sl] = pb
        if lat:
            ls_out[:, sl] = (b_ref[0, :, sl] + b_ref[1, :, sl]).astype(BF16)
            pf_out[:, sl] = a_ref[0, :, sl].astype(BF16)
            pb_out[:, sl] = a_ref[1, :, sl].astype(BF16)

    for hd in range(LRU_HEADS):
        for n0 in pieces[hd * per_head:(hd + 1) * per_head]:
            branch_piece(n0)
        coefficients(hd)
        if hd + GATE_AHEAD < LRU_HEADS:
            conv_gates(hd + GATE_AHEAD)
        scans(hd)


def _head_call(x, pos, mod, g, w, conv_w, conv_b, wg, b_a, b_x, lam, *, lat):
    seq = x.shape[1]
    n_chunks = seq // TT
    n_halo = seq // HALO_T
    per_chunk = TT // HALO_T

    def const(shape):
        return pl.BlockSpec(shape, lambda c: (0,) * len(shape))

    def prev_idx(c):
        return jnp.maximum(c * per_chunk - 1, 0)

    def next_idx(c):
        return jnp.minimum((c + 1) * per_chunk, n_halo - 1)

    in_specs = [
        pl.BlockSpec((BATCH, TT, D_MODEL), lambda c: (0, c, 0)),
        pl.BlockSpec((BATCH, HALO_T, D_MODEL), lambda c: (0, prev_idx(c), 0)),
        pl.BlockSpec((BATCH, HALO_T, D_MODEL), lambda c: (0, next_idx(c), 0)),
    ]
    args = [x, x, x]
    if lat:
        in_specs += [
            pl.BlockSpec((TT, D_MODEL), lambda c: (c, 0)),
            pl.BlockSpec((HALO_T, D_MODEL), lambda c: (prev_idx(c), 0)),
            pl.BlockSpec((HALO_T, D_MODEL), lambda c: (next_idx(c), 0)),
        ]
        args += [pos, pos, pos]
    in_specs += [
        _mod_spec(0 if lat else 1, 0), _mod_spec(0 if lat else 1, 1), const((1, D_MODEL)),
        const((D_MODEL, IN_COLS if lat else LRU_WIDTH)),
        const((CONV_WIDTH, LRU_WIDTH)), const((1, LRU_WIDTH)),
        const((2, LRU_HEADS, LRU_HEAD_DIM, 2 * LRU_HEAD_DIM)),
        const((2, LRU_WIDTH)), const((2, LRU_WIDTH)), const((2, LRU_WIDTH)),
    ]
    args += [mod, mod, g, w, conv_w, conv_b, wg, b_a, b_x, lam]
    w_slab = pl.BlockSpec((D_MODEL // n_chunks, IN_COLS), lambda c: (c, 0))
    if not lat:
        in_specs.append(w_slab)
        args.append(w)

    def rows(width):
        return pl.BlockSpec((ROWS, width), lambda c: (c, 0))

    out_specs, out_shape = [], []
    if lat:
        n = n_chunks * ROWS
        out_specs += [rows(D_MODEL), rows(LRU_WIDTH),
                      pl.BlockSpec((TT, BATCH * FOURIER_WIDTH), lambda c: (c, 0)),
                      rows(2 * D_MODEL), rows(LRU_WIDTH), rows(LRU_WIDTH), rows(LRU_WIDTH)]
        out_shape += [
            jax.ShapeDtypeStruct((n, D_MODEL), F32),
            jax.ShapeDtypeStruct((n, LRU_WIDTH), BF16),
            jax.ShapeDtypeStruct((seq, BATCH * FOURIER_WIDTH), BF16),
            jax.ShapeDtypeStruct((n, 2 * D_MODEL), BF16),
            jax.ShapeDtypeStruct((n, LRU_WIDTH), BF16),
            jax.ShapeDtypeStruct((n, LRU_WIDTH), BF16),
            jax.ShapeDtypeStruct((n, LRU_WIDTH), BF16),
        ]
    else:
        out_specs.append(w_slab)
        out_shape.append(jax.ShapeDtypeStruct((D_MODEL, IN_COLS), BF16))
    out_specs.append(pl.BlockSpec((None, 4, BATCH, LRU_WIDTH), lambda c: (c, 0, 0, 0)))
    out_shape.append(jax.ShapeDtypeStruct((n_chunks, 4, BATCH, LRU_WIDTH), F32))

    return pl.pallas_call(
        functools.partial(_head_kernel, lat=lat, n_chunks=n_chunks),
        grid=(n_chunks,),
        in_specs=in_specs,
        out_specs=out_specs,
        out_shape=out_shape,
        scratch_shapes=[
            pltpu.VMEM((XROWS_PAD, D_MODEL), F32),
            pltpu.VMEM((XROWS_PAD, D_MODEL), BF16),
            pltpu.VMEM((2, ROWS, LRU_WIDTH), F32),
            pltpu.VMEM((2, ROWS, LRU_WIDTH), F32),
            pltpu.VMEM((GATE_SLOTS, ROWS, LRU_HEAD_DIM), F32),
            pltpu.VMEM((GATE_SLOTS, 2, ROWS, 2 * LRU_HEAD_DIM), F32),
        ],
        name="head_lat" if lat else "head_ctx",
        compiler_params=pltpu.CompilerParams(
            dimension_semantics=("arbitrary",), vmem_limit_bytes=VMEM_LIMIT),
    )(*args)


def _carry_kernel(ec_ref, el_ref, hf_ref, hb_ref):
    n_ctx, n_lat = ec_ref.shape[0], el_ref.shape[0]
    h = jnp.zeros((BATCH, LRU_WIDTH), F32)
    for c in range(n_ctx):
        h = ec_ref[c, 1] * h + ec_ref[c, 0]
    for c in range(n_lat):
        hf_ref[c] = h
        h = el_ref[c, 1] * h + el_ref[c, 0]
    h = jnp.zeros((BATCH, LRU_WIDTH), F32)
    for c in reversed(range(n_ctx)):
        h = ec_ref[c, 3] * h + ec_ref[c, 2]
    for c in reversed(range(n_lat)):
        hb_ref[c] = h
        h = el_ref[c, 3] * h + el_ref[c, 2]


def _carry_call(e_ctx, e_lat):
    shape = jax.ShapeDtypeStruct((e_lat.shape[0], BATCH, LRU_WIDTH), F32)
    return pl.pallas_call(_carry_kernel, out_shape=[shape, shape], name="carry")(e_ctx, e_lat)


def _fourier_kernel(x_ref, chan_ref, time_ref, flip_ref, *refs):
    n_w = len(TAIL_WEIGHT_ROWS)
    w_f32_refs, o_ref, w_bf16_refs, z_ref = refs[:n_w], refs[n_w], refs[n_w + 1:2 * n_w + 1], refs[-1]
    for src, dst in zip(w_f32_refs, w_bf16_refs):
        dst[...] = src[...].astype(BF16)

    half, quarter = SEQ // 2, SEQ // 4
    for g in range(FOURIER_GROUPS):
        sl = slice(g * FOURIER_GROUP_DIM, (g + 1) * FOURIER_GROUP_DIM)
        r = _dot(x_ref[:, sl], chan_ref[...])
        lo, hi = r[:half], r[half:]
        for p, v in enumerate((lo + hi, lo - hi)):
            z_ref[p, 0:half, sl] = v[:, :FOURIER_GROUP_DIM].astype(BF16)
            z_ref[p, half:SEQ, sl] = v[:, FOURIER_GROUP_DIM:].astype(BF16)
    for p in range(2):
        cos_part = _dot(time_ref[p, 0], z_ref[p, 0:half, :])
        sin_part = _dot(time_ref[p, 1], z_ref[p, half:SEQ, :])
        o_ref[p, 0:quarter, :] = (cos_part - sin_part)[0:quarter].astype(BF16)
        mirrored = (cos_part + sin_part).astype(BF16)
        o_ref[p, quarter:half, :] = _dot(flip_ref[p], mirrored).astype(BF16)


def _fourier_call(uf, chan, time, flip, weights):
    def slab(w):
        return pl.BlockSpec((w.shape[0] // BATCH, w.shape[1]), lambda n: (n, 0))

    assert tuple(w.shape[0] for w in weights) == TAIL_WEIGHT_ROWS
    return pl.pallas_call(
        _fourier_kernel,
        grid=(BATCH,),
        in_specs=[
            pl.BlockSpec((SEQ, FOURIER_WIDTH), lambda n: (0, n)),
            pl.BlockSpec((FOURIER_GROUP_DIM, 2 * FOURIER_GROUP_DIM), lambda n: (0, 0)),
            pl.BlockSpec((2, 2, DFT_ROWS, SEQ // 2), lambda n: (0, 0, 0, 0)),
            pl.BlockSpec((2, SEQ // 4, DFT_ROWS), lambda n: (0, 0, 0)),
        ] + [slab(w) for w in weights],
        out_specs=[pl.BlockSpec((2, SEQ // 2, FOURIER_WIDTH), lambda n: (0, 0, n))]
        + [slab(w) for w in weights],
        out_shape=[jax.ShapeDtypeStruct((2, SEQ // 2, BATCH * FOURIER_WIDTH), BF16)]
        + [jax.ShapeDtypeStruct(w.shape, BF16) for w in weights],
        scratch_shapes=[pltpu.VMEM((2, SEQ, FOURIER_WIDTH), BF16)],
        name="fourier",
        compiler_params=pltpu.CompilerParams(
            dimension_semantics=("arbitrary",), vmem_limit_bytes=VMEM_LIMIT),
    )(uf, chan, time, flip, *weights)


def _tail_kernel(ls_ref, pf_ref, pb_ref, hf_ref, hb_ref, uy_ref, ug_ref, yf_ref, xt_ref,
                 gt1_ref, sh2_ref, sc2_ref, gt2_ref, gm_ref, gf_ref,
                 wl_ref, wf_ref, wo_ref, w1_ref, w2_ref, o_ref):
    def tb(v):
        return v.reshape(TT, BATCH, v.shape[-1])

    def flat(v):
        return v.reshape(ROWS, v.shape[-1])

    yf = pltpu.einshape("pk(bd)->(kpb)d", yf_ref[...].astype(F32), b=BATCH).astype(BF16)
    y_b = _dot(yf, wf_ref[...])

    hf = jnp.tile(hf_ref[...].astype(BF16), (TT, 1))
    hb = jnp.tile(hb_ref[...].astype(BF16), (TT, 1))
    y_lru = ls_ref[...] + pf_ref[...] * hf + pb_ref[...] * hb
    z = y_lru * _gelu_tanh(uy_ref[...])
    y_a = _dot(z, wl_ref[...])

    g_a = _sigmoid(ug_ref[:, :D_MODEL].astype(F32))
    g_b = _sigmoid(ug_ref[:, D_MODEL:].astype(F32))
    merged = (g_a * y_a + g_b * y_b).astype(BF16)
    x2 = tb(xt_ref[...]) + gt1_ref[...][None] * tb(_dot(merged, wo_ref[...]))

    h = (tb(_rms(flat(x2), gm_ref[...])) * (1.0 + sc2_ref[...][None]) + sh2_ref[...][None])
    h = flat(h).astype(BF16)
    acc = jnp.zeros((ROWS, D_MODEL), F32)
    for k0 in range(0, D_FF, D_MODEL):
        a = jnp.maximum(_dot(h, w1_ref[:, k0:k0 + D_MODEL]), 0.0)
        acc = acc + _dot((a * a).astype(BF16), w2_ref[k0:k0 + D_MODEL, :])
    out = _rms(flat(x2 + gt2_ref[...][None] * tb(acc)), gf_ref[...])

    o_ref[...] = pltpu.einshape("tbd->btd", tb(out))


def _tail_call(ls, pf, pb, hf, hb, uy, ug, yf, xt, mod, gm, gf, wl, wf, wo, w1, w2):
    def rows(width):
        return pl.BlockSpec((ROWS, width), lambda c: (c, 0))

    def const(shape):
        return pl.BlockSpec(shape, lambda c: (0,) * len(shape))

    state = pl.BlockSpec((None, BATCH, LRU_WIDTH), lambda c: (c, 0, 0))
    vec1 = const((1, D_MODEL))
    return pl.pallas_call(
        _tail_kernel,
        grid=(N_LAT_CHUNKS,),
        in_specs=[
            rows(LRU_WIDTH), rows(LRU_WIDTH), rows(LRU_WIDTH), state, state,
            rows(LRU_WIDTH), rows(2 * D_MODEL),
            pl.BlockSpec((2, TT // 2, BATCH * FOURIER_WIDTH), lambda c: (0, c, 0)),
            rows(D_MODEL),
            _mod_spec(0, 2), _mod_spec(0, 3), _mod_spec(0, 4), _mod_spec(0, 5), vec1, vec1,
            const((LRU_WIDTH, D_MODEL)), const((FOURIER_WIDTH, D_MODEL)), const((D_MODEL, D_MODEL)),
            const((D_MODEL, D_FF)), const((D_FF, D_MODEL)),
        ],
        out_specs=pl.BlockSpec((BATCH, TT, D_MODEL), lambda c: (0, c, 0)),
        out_shape=jax.ShapeDtypeStruct((BATCH, SEQ, D_MODEL), F32),
        name="tail",
        compiler_params=pltpu.CompilerParams(
            dimension_semantics=("arbitrary",), vmem_limit_bytes=VMEM_LIMIT),
    )(ls, pf, pb, hf, hb, uy, ug, yf, xt, mod, mod, mod, mod, gm, gf, wl, wf, wo, w1, w2)


def kernel(x, c, ctx, c_ctx, w_mod, b_mod, g_mix, w_in, conv_w, conv_b, w_a, b_a, w_x, b_x,
           lam, w_lru_out, w_f_out, w_out, g_mlp, w1, w2, g_final):
    pos_np, chan_np, time_np, flip_np = _constants()
    pos = jnp.asarray(pos_np)
    chan, time, flip = (jnp.asarray(v).astype(BF16) for v in (chan_np, time_np, flip_np))

    cc = jnp.concatenate(
        [c, c_ctx[None], jnp.zeros((2 * SUBLANES - BATCH - 1, D_MODEL), F32)], axis=0)
    mod = _mod_call(cc, w_mod[0], b_mod[0][None])

    g_mix2 = g_mix[0][None]
    wg = jnp.concatenate([w_a[0], w_x[0]], axis=-1).astype(BF16)
    lru = (conv_w[0], conv_b[0][None], wg, b_a[0], b_x[0], lam[0])

    w_in_b, e_ctx = _head_call(ctx, None, mod, g_mix2, w_in[0], *lru, lat=False)
    xt, uy, uf, ug, ls, pf, pb, e_lat = _head_call(x, pos, mod, g_mix2, w_in_b, *lru, lat=True)
    hf, hb = _carry_call(e_ctx, e_lat)
    yf, wl, wf, wo, w1b, w2b = _fourier_call(
        uf, chan, time, flip, (w_lru_out[0], w_f_out[0], w_out[0], w1[0], w2[0]))
    return _tail_call(ls, pf, pb, hf, hb, uy, ug, yf, xt, mod,
                      g_mlp[0][None], g_final[None], wl, wf, wo, w1b, w2b)
```

```python
import functools
import math

import numpy as np
import jax
import jax.numpy as jnp
from jax import lax
from jax.experimental import pallas as pl
from jax.experimental.pallas import tpu as pltpu

F32 = jnp.float32
BF16 = jnp.bfloat16

D_MODEL = 1024
BATCH = 8
SEQ = 2048
CTX_LEN = 256
GRID_W = 64
LRU_WIDTH = 1024
LRU_HEADS = 8
LRU_HEAD_DIM = LRU_WIDTH // LRU_HEADS
LRU_C = 8.0
CONV_WIDTH = 4
CONV_PAD_LEFT = 2
FOURIER_WIDTH = 512
FOURIER_GROUPS = 4
FOURIER_GROUP_DIM = FOURIER_WIDTH // FOURIER_GROUPS
IN_COLS = 2 * LRU_WIDTH + FOURIER_WIDTH + 2 * D_MODEL
D_FF = 4 * D_MODEL
N_MOD = 6
EPS = 1e-6
POS_MAX_PERIOD = 10000.0

SUBLANES = 8
BF16_ROWS = 16
assert BATCH == SUBLANES

TT = 64
ROWS = TT * BATCH
HALO_T = SUBLANES
LEAD = CONV_PAD_LEFT * BATCH
TRAIL = (CONV_WIDTH - 1 - CONV_PAD_LEFT) * BATCH
XROWS = LEAD + ROWS + TRAIL
XROWS_PAD = -(-XROWS // BF16_ROWS) * BF16_ROWS
N_CTX_CHUNKS = CTX_LEN // TT
N_LAT_CHUNKS = SEQ // TT
NCHUNK = 512
FRONT_ROWS = 128
GATE_AHEAD = 1
GATE_SLOTS = GATE_AHEAD + 1
DFT_ROWS = -(-(SEQ // 4 + 1) // BF16_ROWS) * BF16_ROWS
TAIL_WEIGHT_ROWS = (LRU_WIDTH, FOURIER_WIDTH, D_MODEL, D_MODEL, D_FF)
MOD_TN = 1024
V7X_VMEM_BYTES = 64 * 1024 * 1024
VMEM_RESERVE = 6 * 1024 * 1024
VMEM_LIMIT = V7X_VMEM_BYTES - VMEM_RESERVE
LOG2E = math.log2(math.e)
TINY = 1e-30


def _gelu_tanh(x):
    k0 = jnp.full((1, 1), math.sqrt(2.0 / math.pi), F32).astype(x.dtype)
    k1 = jnp.full((1, 1), 0.044715, F32).astype(x.dtype)
    return 0.5 * x * (1.0 + jnp.tanh(k0 * (x + k1 * (x * x * x))))


def _sigmoid(x):
    return 0.5 * jnp.tanh(0.5 * x) + 0.5


def _dot(a, b):
    return jnp.dot(a, b, preferred_element_type=F32)


def _rms(x, g):
    ms = jnp.mean(x * x, axis=-1, keepdims=True)
    return x * lax.rsqrt(ms + EPS) * g


@functools.cache
def _constants():
    half = D_MODEL // 4
    freqs = np.exp(-math.log(POS_MAX_PERIOD) * np.arange(half, dtype=np.float32) / half).astype(np.float32)

    def sincos(n):
        ang = np.arange(n, dtype=np.float32)[:, None] * freqs[None, :]
        return np.concatenate([np.sin(ang), np.cos(ang)], axis=-1).astype(np.float32)

    rows = SEQ // GRID_W
    er, ec = sincos(rows), sincos(GRID_W)
    pos = np.concatenate([
        np.broadcast_to(er[:, None, :], (rows, GRID_W, D_MODEL // 2)),
        np.broadcast_to(ec[None, :, :], (rows, GRID_W, D_MODEL // 2)),
    ], axis=-1).reshape(SEQ, D_MODEL).astype(np.float32)

    def dft(n):
        k = np.arange(n, dtype=np.int64)
        ang = 2.0 * np.pi * ((k[:, None] * k[None, :]) % n).astype(np.float64) / n
        return np.cos(ang) / math.sqrt(n), np.sin(ang) / math.sqrt(n)

    cc, sc = dft(FOURIER_GROUP_DIM)
    chan = np.concatenate([cc, sc], axis=1).astype(np.float32)
    ct, st = dft(SEQ)
    half, quarter = SEQ // 2, SEQ // 4
    time = np.zeros((2, 2, DFT_ROWS, half), np.float64)
    flip = np.zeros((2, quarter, DFT_ROWS), np.float32)
    for p in range(2):
        n_rows = quarter + 1 - p
        rows = 2 * np.arange(n_rows) + p
        time[p, 0, :n_rows] = ct[rows, :half]
        time[p, 1, :n_rows] = st[rows, :half]
        flip[p, np.arange(quarter), quarter - p - np.arange(quarter)] = 1.0
    return pos, chan, time.astype(np.float32), flip


def _mod_kernel(c_ref, w_ref, b_ref, o_ref):
    c = c_ref[...]
    a = c * _sigmoid(c)
    w = w_ref[...]
    a_hi = a.astype(BF16)
    a_lo = (a - a_hi.astype(F32)).astype(BF16)
    w_hi = w.astype(BF16)
    w_lo = (w - w_hi.astype(F32)).astype(BF16)
    n = a.shape[0]
    both = _dot(jnp.concatenate([a_hi, a_lo], axis=0), w_hi)
    o_ref[...] = both[:n] + _dot(a_hi, w_lo) + both[n:] + b_ref[...]


def _mod_call(cc, w_mod, b_mod):
    n = w_mod.shape[1]
    return pl.pallas_call(
        _mod_kernel,
        grid=(n // MOD_TN,),
        in_specs=[
            pl.BlockSpec((2 * SUBLANES, D_MODEL), lambda i: (0, 0)),
            pl.BlockSpec((D_MODEL, MOD_TN), lambda i: (0, i)),
            pl.BlockSpec((1, MOD_TN), lambda i: (0, i)),
        ],
        out_specs=pl.BlockSpec((2 * SUBLANES, MOD_TN), lambda i: (0, i)),
        out_shape=jax.ShapeDtypeStruct((2 * SUBLANES, n), F32),
        name="mod",
        compiler_params=pltpu.CompilerParams(dimension_semantics=("arbitrary",)),
    )(cc, w_mod, b_mod)


def _mod_spec(row_block, k):
    return pl.BlockSpec((BATCH, D_MODEL), lambda c: (row_block, k))


def _head_kernel(*refs, lat, n_chunks):
    refs = list(refs)
    xm_ref, xp_ref, xn_ref = refs[:3]
    refs = refs[3:]
    if lat:
        pm_ref, pp_ref, pn_ref = refs[:3]
        refs = refs[3:]
    (sh_ref, sc_ref, g_ref, w_ref, cw_ref, cb_ref, wg_ref, ba_ref, bx_ref, lam_ref) = refs[:10]
    refs = refs[10:]
    if lat:
        xt_out, uy_out, uf_out, ug_out, ls_out, pf_out, pb_out = refs[:7]
        refs = refs[7:]
    else:
        w_slab_ref, w_bf16_out = refs[:2]
        refs = refs[2:]
        w_bf16_out[...] = w_slab_ref[...].astype(BF16)
    e_out, xs_ref, hb_ref, a_ref, b_ref, xh_ref, gate_ref = refs

    c = pl.program_id(0)
    mod_rows = slice(None) if lat else slice(0, 1)

    xm = xm_ref[...]
    xp = xp_ref[...]
    xn = xn_ref[...]
    if lat:
        xm = xm + pm_ref[...][None]
        xp = xp + pp_ref[...][None]
        xn = xn + pn_ref[...][None]
    xs_ref[LEAD:LEAD + ROWS, :] = jnp.swapaxes(xm, 0, 1).reshape(ROWS, D_MODEL)
    xs_ref[0:LEAD, :] = jnp.swapaxes(xp, 0, 1)[HALO_T - CONV_PAD_LEFT:].reshape(LEAD, D_MODEL)
    xs_ref[LEAD + ROWS:XROWS, :] = (
        jnp.swapaxes(xn, 0, 1)[:CONV_WIDTH - 1 - CONV_PAD_LEFT].reshape(TRAIL, D_MODEL))
    xs_ref[XROWS:XROWS_PAD, :] = jnp.zeros((XROWS_PAD - XROWS, D_MODEL), F32)

    assert LEAD % BF16_ROWS == 0 and FRONT_ROWS % BF16_ROWS == 0
    blocks = ([(0, LEAD)] + [(r, r + FRONT_ROWS) for r in range(LEAD, LEAD + ROWS, FRONT_ROWS)]
              + [(LEAD + ROWS, XROWS_PAD)])
    for r0, r1 in blocks:
        xt = xs_ref[r0:r1, :]
        if lat and LEAD <= r0 < LEAD + ROWS:
            xt_out[r0 - LEAD:r1 - LEAD, :] = xt
        h = _rms(xt, g_ref[...]).reshape((r1 - r0) // BATCH, BATCH, D_MODEL)
        h = h * (1.0 + sc_ref[mod_rows, :][None]) + sh_ref[mod_rows, :][None]
        hb_ref[r0:r1, :] = h.reshape(r1 - r0, D_MODEL).astype(BF16)

    keep_first = jnp.where(c == 0, 0.0, 1.0)
    keep_last = jnp.where(c == n_chunks - 1, 0.0, 1.0)
    for n0 in range(0, LRU_WIDTH, NCHUNK):
        w_x = w_ref[:, n0:n0 + NCHUNK]
        xs_ref[:, n0:n0 + NCHUNK] = _dot(hb_ref[...], w_x if lat else w_x.astype(BF16))
    xs_ref[0:LEAD, :] = xs_ref[0:LEAD, :] * keep_first
    xs_ref[LEAD + ROWS:XROWS, :] = xs_ref[LEAD + ROWS:XROWS, :] * keep_last

    def branch_piece(n0):
        r = _dot(hb_ref[LEAD:LEAD + ROWS, :], w_ref[:, n0:n0 + NCHUNK])
        if n0 < 2 * LRU_WIDTH:
            uy_out[:, n0 - LRU_WIDTH:n0 - LRU_WIDTH + NCHUNK] = r.astype(BF16)
        elif n0 < 2 * LRU_WIDTH + FOURIER_WIDTH:
            assert NCHUNK == FOURIER_WIDTH
            uf_out[...] = r.reshape(TT, BATCH * NCHUNK).astype(BF16)
        else:
            c0 = n0 - 2 * LRU_WIDTH - FOURIER_WIDTH
            ug_out[:, c0:c0 + NCHUNK] = r.astype(BF16)

    pieces = list(range(LRU_WIDTH, IN_COLS, NCHUNK)) if lat else []
    per_head = -(-len(pieces) // LRU_HEADS)

    lam = lam_ref[...]
    sp = jnp.maximum(-lam, 0.0) + jnp.log1p(jnp.exp(-jnp.abs(lam)))
    c1 = (-0.5 * LRU_C * LOG2E) * sp

    def conv_gates(hd):
        sl = slice(hd * LRU_HEAD_DIM, (hd + 1) * LRU_HEAD_DIM)
        xh = 0.5 * cb_ref[:, sl]
        for k in range(CONV_WIDTH):
            xh = xh + (0.5 * cw_ref[k:k + 1, sl]) * xs_ref[k * BATCH:k * BATCH + ROWS, sl]
        xh_ref[hd % GATE_SLOTS] = xh
        xhb = xh.astype(BF16)
        for d in range(2):
            gate_ref[hd % GATE_SLOTS, d] = _dot(xhb, wg_ref[d, hd])

    for hd in range(GATE_AHEAD):
        conv_gates(hd)
    def coefficients(hd):
        sl = slice(hd * LRU_HEAD_DIM, (hd + 1) * LRU_HEAD_DIM)
        xh = xh_ref[hd % GATE_SLOTS]
        for d in range(2):
            g = gate_ref[hd % GATE_SLOTS, d]
            ta = jnp.tanh(g[:, :LRU_HEAD_DIM] + 0.5 * ba_ref[d:d + 1, sl])
            ti = jnp.tanh(g[:, LRU_HEAD_DIM:] + 0.5 * bx_ref[d:d + 1, sl])
            a = jnp.exp2(c1[d:d + 1, sl] * ta + c1[d:d + 1, sl])
            v = 1.0 - a * a
            m = v * lax.rsqrt(jnp.maximum(v, TINY))
            a_ref[d, :, sl] = a
            b_ref[d, :, sl] = (m * xh) * (ti + 1.0)

    def scans(hd):
        sl = slice(hd * LRU_HEAD_DIM, (hd + 1) * LRU_HEAD_DIM)
        lf = pf = lb = pb = None
        for t in range(TT):
            rf = slice(t * BATCH, (t + 1) * BATCH)
            rb = slice((TT - 1 - t) * BATCH, (TT - t) * BATCH)
            af, bf = a_ref[0, rf, sl], b_ref[0, rf, sl]
            ab, bb = a_ref[1, rb, sl], b_ref[1, rb, sl]
            if t == 0:
                lf, pf, lb, pb = bf, af, bb, ab
            else:
                lf, pf = af * lf + bf, af * pf
                lb, pb = ab * lb + bb, ab * pb
                b_ref[0, rf, sl] = lf
                a_ref[0, rf, sl] = pf
                b_ref[1, rb, sl] = lb
                a_ref[1, rb, sl] = pb
        e_out[0, :, sl] = lf
        e_out[1, :, sl] = pf
        e_out[2, :, sl] = lb
        e_out[3, :, sl] = pb
        if lat:
            ls_out[:, sl] = (b_ref[0, :, sl] + b_ref[1, :, sl]).astype(BF16)
            pf_out[:, sl] = a_ref[0, :, sl].astype(BF16)
            pb_out[:, sl] = a_ref[1, :, sl].astype(BF16)

    for hd in range(LRU_HEADS):
        for n0 in pieces[hd * per_head:(hd + 1) * per_head]:
            branch_piece(n0)
        coefficients(hd)
        if hd + GATE_AHEAD < LRU_HEADS:
            conv_gates(hd + GATE_AHEAD)
        scans(hd)


def _head_call(x, pos, mod, g, w, conv_w, conv_b, wg, b_a, b_x, lam, *, lat):
    seq = x.shape[1]
    n_chunks = seq // TT
    n_halo = seq // HALO_T
    per_chunk = TT // HALO_T

    def const(shape):
        return pl.BlockSpec(shape, lambda c: (0,) * len(shape))

    def prev_idx(c):
        return jnp.maximum(c * per_chunk - 1, 0)

    def next_idx(c):
        return jnp.minimum((c + 1) * per_chunk, n_halo - 1)

    in_specs = [
        pl.BlockSpec((BATCH, TT, D_MODEL), lambda c: (0, c, 0)),
        pl.BlockSpec((BATCH, HALO_T, D_MODEL), lambda c: (0, prev_idx(c), 0)),
        pl.BlockSpec((BATCH, HALO_T, D_MODEL), lambda c: (0, next_idx(c), 0)),
    ]
    args = [x, x, x]
    if lat:
        in_specs += [
            pl.BlockSpec((TT, D_MODEL), lambda c: (c, 0)),
            pl.BlockSpec((HALO_T, D_MODEL), lambda c: (prev_idx(c), 0)),
            pl.BlockSpec((HALO_T, D_MODEL), lambda c: (next_idx(c), 0)),
        ]
        args += [pos, pos, pos]
    in_specs += [
        _mod_spec(0 if lat else 1, 0), _mod_spec(0 if lat else 1, 1), const((1, D_MODEL)),
        const((D_MODEL, IN_COLS if lat else LRU_WIDTH)),
        const((CONV_WIDTH, LRU_WIDTH)), const((1, LRU_WIDTH)),
        const((2, LRU_HEADS, LRU_HEAD_DIM, 2 * LRU_HEAD_DIM)),
        const((2, LRU_WIDTH)), const((2, LRU_WIDTH)), const((2, LRU_WIDTH)),
    ]
    args += [mod, mod, g, w, conv_w, conv_b, wg, b_a, b_x, lam]
    w_slab = pl.BlockSpec((D_MODEL // n_chunks, IN_COLS), lambda c: (c, 0))
    if not lat:
        in_specs.append(w_slab)
        args.append(w)

    def rows(width):
        return pl.BlockSpec((ROWS, width), lambda c: (c, 0))

    out_specs, out_shape = [], []
    if lat:
        n = n_chunks * ROWS
        out_specs += [rows(D_MODEL), rows(LRU_WIDTH),
                      pl.BlockSpec((TT, BATCH * FOURIER_WIDTH), lambda c: (c, 0)),
                      rows(2 * D_MODEL), rows(LRU_WIDTH), rows(LRU_WIDTH), rows(LRU_WIDTH)]
        out_shape += [
            jax.ShapeDtypeStruct((n, D_MODEL), F32),
            jax.ShapeDtypeStruct((n, LRU_WIDTH), BF16),
            jax.ShapeDtypeStruct((seq, BATCH * FOURIER_WIDTH), BF16),
            jax.ShapeDtypeStruct((n, 2 * D_MODEL), BF16),
            jax.ShapeDtypeStruct((n, LRU_WIDTH), BF16),
            jax.ShapeDtypeStruct((n, LRU_WIDTH), BF16),
            jax.ShapeDtypeStruct((n, LRU_WIDTH), BF16),
        ]
    else:
        out_specs.append(w_slab)
        out_shape.append(jax.ShapeDtypeStruct((D_MODEL, IN_COLS), BF16))
    out_specs.append(pl.BlockSpec((None, 4, BATCH, LRU_WIDTH), lambda c: (c, 0, 0, 0)))
    out_shape.append(jax.ShapeDtypeStruct((n_chunks, 4, BATCH, LRU_WIDTH), F32))

    return pl.pallas_call(
        functools.partial(_head_kernel, lat=lat, n_chunks=n_chunks),
        grid=(n_chunks,),
        in_specs=in_specs,
        out_specs=out_specs,
        out_shape=out_shape,
        scratch_shapes=[
            pltpu.VMEM((XROWS_PAD, D_MODEL), F32),
            pltpu.VMEM((XROWS_PAD, D_MODEL), BF16),
            pltpu.VMEM((2, ROWS, LRU_WIDTH), F32),
            pltpu.VMEM((2, ROWS, LRU_WIDTH), F32),
            pltpu.VMEM((GATE_SLOTS, ROWS, LRU_HEAD_DIM), F32),
            pltpu.VMEM((GATE_SLOTS, 2, ROWS, 2 * LRU_HEAD_DIM), F32),
        ],
        name="head_lat" if lat else "head_ctx",
        compiler_params=pltpu.CompilerParams(
            dimension_semantics=("arbitrary",), vmem_limit_bytes=VMEM_LIMIT),
    )(*args)


def _carry_kernel(ec_ref, el_ref, hf_ref, hb_ref):
    n_ctx, n_lat = ec_ref.shape[0], el_ref.shape[0]
    h = jnp.zeros((BATCH, LRU_WIDTH), F32)
    for c in range(n_ctx):
        h = ec_ref[c, 1] * h + ec_ref[c, 0]
    for c in range(n_lat):
        hf_ref[c] = h
        h = el_ref[c, 1] * h + el_ref[c, 0]
    h = jnp.zeros((BATCH, LRU_WIDTH), F32)
    for c in reversed(range(n_ctx)):
        h = ec_ref[c, 3] * h + ec_ref[c, 2]
    for c in reversed(range(n_lat)):
        hb_ref[c] = h
        h = el_ref[c, 3] * h + el_ref[c, 2]


def _carry_call(e_ctx, e_lat):
    shape = jax.ShapeDtypeStruct((e_lat.shape[0], BATCH, LRU_WIDTH), F32)
    return pl.pallas_call(_carry_kernel, out_shape=[shape, shape], name="carry")(e_ctx, e_lat)


def _fourier_kernel(x_ref, chan_ref, time_ref, flip_ref, *refs):
    n_w = len(TAIL_WEIGHT_ROWS)
    w_f32_refs, o_ref, w_bf16_refs, z_ref = refs[:n_w], refs[n_w], refs[n_w + 1:2 * n_w + 1], refs[-1]
    for src, dst in zip(w_f32_refs, w_bf16_refs):
        dst[...] = src[...].astype(BF16)

    half, quarter = SEQ // 2, SEQ // 4
    for g in range(FOURIER_GROUPS):
        sl = slice(g * FOURIER_GROUP_DIM, (g + 1) * FOURIER_GROUP_DIM)
        r = _dot(x_ref[:, sl], chan_ref[...])
        lo, hi = r[:half], r[half:]
        for p, v in enumerate((lo + hi, lo - hi)):
            z_ref[p, 0:half, sl] = v[:, :FOURIER_GROUP_DIM].astype(BF16)
            z_ref[p, half:SEQ, sl] = v[:, FOURIER_GROUP_DIM:].astype(BF16)
    for p in range(2):
        cos_part = _dot(time_ref[p, 0], z_ref[p, 0:half, :])
        sin_part = _dot(time_ref[p, 1], z_ref[p, half:SEQ, :])
        o_ref[p, 0:quarter, :] = (cos_part - sin_part)[0:quarter].astype(BF16)
        mirrored = (cos_part + sin_part).astype(BF16)
        o_ref[p, quarter:half, :] = _dot(flip_ref[p], mirrored).astype(BF16)


def _fourier_call(uf, chan, time, flip, weights):
    def slab(w):
        return pl.BlockSpec((w.shape[0] // BATCH, w.shape[1]), lambda n: (n, 0))

    assert tuple(w.shape[0] for w in weights) == TAIL_WEIGHT_ROWS
    return pl.pallas_call(
        _fourier_kernel,
        grid=(BATCH,),
        in_specs=[
            pl.BlockSpec((SEQ, FOURIER_WIDTH), lambda n: (0, n)),
            pl.BlockSpec((FOURIER_GROUP_DIM, 2 * FOURIER_GROUP_DIM), lambda n: (0, 0)),
            pl.BlockSpec((2, 2, DFT_ROWS, SEQ // 2), lambda n: (0, 0, 0, 0)),
            pl.BlockSpec((2, SEQ // 4, DFT_ROWS), lambda n: (0, 0, 0)),
        ] + [slab(w) for w in weights],
        out_specs=[pl.BlockSpec((2, SEQ // 2, FOURIER_WIDTH), lambda n: (0, 0, n))]
        + [slab(w) for w in weights],
        out_shape=[jax.ShapeDtypeStruct((2, SEQ // 2, BATCH * FOURIER_WIDTH), BF16)]
        + [jax.ShapeDtypeStruct(w.shape, BF16) for w in weights],
        scratch_shapes=[pltpu.VMEM((2, SEQ, FOURIER_WIDTH), BF16)],
        name="fourier",
        compiler_params=pltpu.CompilerParams(
            dimension_semantics=("arbitrary",), vmem_limit_bytes=VMEM_LIMIT),
    )(uf, chan, time, flip, *weights)


def _tail_kernel(ls_ref, pf_ref, pb_ref, hf_ref, hb_ref, uy_ref, ug_ref, yf_ref, xt_ref,
                 gt1_ref, sh2_ref, sc2_ref, gt2_ref, gm_ref, gf_ref,
                 wl_ref, wf_ref, wo_ref, w1_ref, w2_ref, o_ref):
    def tb(v):
        return v.reshape(TT, BATCH, v.shape[-1])

    def flat(v):
        return v.reshape(ROWS, v.shape[-1])

    yf = yf_ref[...].astype(F32).reshape(2, TT // 2, BATCH, FOURIER_WIDTH)
    yf = jnp.swapaxes(yf, 0, 1).reshape(ROWS, FOURIER_WIDTH).astype(BF16)
    y_b = _dot(yf, wf_ref[...])

    hf = jnp.tile(hf_ref[...].astype(BF16), (TT, 1))
    hb = jnp.tile(hb_ref[...].astype(BF16), (TT, 1))
    y_lru = ls_ref[...] + pf_ref[...] * hf + pb_ref[...] * hb
    z = y_lru * _gelu_tanh(uy_ref[...])
    y_a = _dot(z, wl_ref[...])

    g_a = _sigmoid(ug_ref[:, :D_MODEL].astype(F32))
    g_b = _sigmoid(ug_ref[:, D_MODEL:].astype(F32))
    merged = (g_a * y_a + g_b * y_b).astype(BF16)
    x2 = tb(xt_ref[...]) + gt1_ref[...][None] * tb(_dot(merged, wo_ref[...]))

    h = (tb(_rms(flat(x2), gm_ref[...])) * (1.0 + sc2_ref[...][None]) + sh2_ref[...][None])
    h = flat(h).astype(BF16)
    acc = jnp.zeros((ROWS, D_MODEL), F32)
    for k0 in range(0, D_FF, D_MODEL):
        a = jnp.maximum(_dot(h, w1_ref[:, k0:k0 + D_MODEL]), 0.0)
        acc = acc + _dot((a * a).astype(BF16), w2_ref[k0:k0 + D_MODEL, :])
    out = _rms(flat(x2 + gt2_ref[...][None] * tb(acc)), gf_ref[...])

    o_ref[...] = jnp.swapaxes(tb(out), 0, 1)


def _tail_call(ls, pf, pb, hf, hb, uy, ug, yf, xt, mod, gm, gf, wl, wf, wo, w1, w2):
    def rows(width):
        return pl.BlockSpec((ROWS, width), lambda c: (c, 0))

    def const(shape):
        return pl.BlockSpec(shape, lambda c: (0,) * len(shape))

    state = pl.BlockSpec((None, BATCH, LRU_WIDTH), lambda c: (c, 0, 0))
    vec1 = const((1, D_MODEL))
    return pl.pallas_call(
        _tail_kernel,
        grid=(N_LAT_CHUNKS,),
        in_specs=[
            rows(LRU_WIDTH), rows(LRU_WIDTH), rows(LRU_WIDTH), state, state,
            rows(LRU_WIDTH), rows(2 * D_MODEL),
            pl.BlockSpec((2, TT // 2, BATCH * FOURIER_WIDTH), lambda c: (0, c, 0)),
            rows(D_MODEL),
            _mod_spec(0, 2), _mod_spec(0, 3), _mod_spec(0, 4), _mod_spec(0, 5), vec1, vec1,
            const((LRU_WIDTH, D_MODEL)), const((FOURIER_WIDTH, D_MODEL)), const((D_MODEL, D_MODEL)),
            const((D_MODEL, D_FF)), const((D_FF, D_MODEL)),
        ],
        out_specs=pl.BlockSpec((BATCH, TT, D_MODEL), lambda c: (0, c, 0)),
        out_shape=jax.ShapeDtypeStruct((BATCH, SEQ, D_MODEL), F32),
        name="tail",
        compiler_params=pltpu.CompilerParams(
            dimension_semantics=("arbitrary",), vmem_limit_bytes=VMEM_LIMIT),
    )(ls, pf, pb, hf, hb, uy, ug, yf, xt, mod, mod, mod, mod, gm, gf, wl, wf, wo, w1, w2)


def kernel(x, c, ctx, c_ctx, w_mod, b_mod, g_mix, w_in, conv_w, conv_b, w_a, b_a, w_x, b_x,
           lam, w_lru_out, w_f_out, w_out, g_mlp, w1, w2, g_final):
    pos_np, chan_np, time_np, flip_np = _constants()
    pos = jnp.asarray(pos_np)
    chan, time, flip = (jnp.asarray(v).astype(BF16) for v in (chan_np, time_np, flip_np))

    cc = jnp.concatenate(
        [c, c_ctx[None], jnp.zeros((2 * SUBLANES - BATCH - 1, D_MODEL), F32)], axis=0)
    mod = _mod_call(cc, w_mod[0], b_mod[0][None])

    g_mix2 = g_mix[0][None]
    wg = jnp.concatenate([w_a[0], w_x[0]], axis=-1).astype(BF16)
    lru = (conv_w[0], conv_b[0][None], wg, b_a[0], b_x[0], lam[0])

    w_in_b, e_ctx = _head_call(ctx, None, mod, g_mix2, w_in[0], *lru, lat=False)
    xt, uy, uf, ug, ls, pf, pb, e_lat = _head_call(x, pos, mod, g_mix2, w_in_b, *lru, lat=True)
    hf, hb = _carry_call(e_ctx, e_lat)
    yf, wl, wf, wo, w1b, w2b = _fourier_call(
        uf, chan, time, flip, (w_lru_out[0], w_f_out[0], w_out[0], w1[0], w2[0]))
    return _tail_call(ls, pf, pb, hf, hb, uy, ug, yf, xt, mod,
                      g_mlp[0][None], g_final[None], wl, wf, wo, w1b, w2b)
```

```python
import functools
import math

import numpy as np
import jax
import jax.numpy as jnp
from jax import lax
from jax.experimental import pallas as pl
from jax.experimental.pallas import tpu as pltpu

F32 = jnp.float32
BF16 = jnp.bfloat16

D_MODEL = 1024
BATCH = 8
SEQ = 2048
CTX_LEN = 256
GRID_W = 64
LRU_WIDTH = 1024
LRU_HEADS = 8
LRU_HEAD_DIM = LRU_WIDTH // LRU_HEADS
LRU_C = 8.0
CONV_WIDTH = 4
CONV_PAD_LEFT = 2
FOURIER_WIDTH = 512
FOURIER_GROUPS = 4
FOURIER_GROUP_DIM = FOURIER_WIDTH // FOURIER_GROUPS
IN_COLS = 2 * LRU_WIDTH + FOURIER_WIDTH + 2 * D_MODEL
D_FF = 4 * D_MODEL
N_MOD = 6
EPS = 1e-6
POS_MAX_PERIOD = 10000.0

SUBLANES = 8
BF16_ROWS = 16
assert BATCH == SUBLANES

TT = 64
ROWS = TT * BATCH
HALO_T = SUBLANES
LEAD = CONV_PAD_LEFT * BATCH
TRAIL = (CONV_WIDTH - 1 - CONV_PAD_LEFT) * BATCH
XROWS = LEAD + ROWS + TRAIL
XROWS_PAD = -(-XROWS // BF16_ROWS) * BF16_ROWS
N_CTX_CHUNKS = CTX_LEN // TT
N_LAT_CHUNKS = SEQ // TT
NCHUNK = 512
FRONT_ROWS = 128
GATE_AHEAD = 1
GATE_SLOTS = GATE_AHEAD + 1
DFT_ROWS = -(-(SEQ // 4 + 1) // BF16_ROWS) * BF16_ROWS
TAIL_WEIGHT_ROWS = (LRU_WIDTH, FOURIER_WIDTH, D_MODEL, D_MODEL, D_FF)
MOD_TN = 1024
V7X_VMEM_BYTES = 64 * 1024 * 1024
VMEM_RESERVE = 6 * 1024 * 1024
VMEM_LIMIT = V7X_VMEM_BYTES - VMEM_RESERVE
LOG2E = math.log2(math.e)
TINY = 1e-30


def _gelu_tanh(x):
    k0 = jnp.full((1, 1), math.sqrt(2.0 / math.pi), F32).astype(x.dtype)
    k1 = jnp.full((1, 1), 0.044715, F32).astype(x.dtype)
    return 0.5 * x * (1.0 + jnp.tanh(k0 * (x + k1 * (x * x * x))))


def _sigmoid(x):
    return 0.5 * jnp.tanh(0.5 * x) + 0.5


def _dot(a, b):
    return jnp.dot(a, b, preferred_element_type=F32)


def _rms(x, g):
    ms = jnp.mean(x * x, axis=-1, keepdims=True)
    return x * lax.rsqrt(ms + EPS) * g


@functools.cache
def _constants():
    half = D_MODEL // 4
    freqs = np.exp(-math.log(POS_MAX_PERIOD) * np.arange(half, dtype=np.float32) / half).astype(np.float32)

    def sincos(n):
        ang = np.arange(n, dtype=np.float32)[:, None] * freqs[None, :]
        return np.concatenate([np.sin(ang), np.cos(ang)], axis=-1).astype(np.float32)

    rows = SEQ // GRID_W
    er, ec = sincos(rows), sincos(GRID_W)
    pos = np.concatenate([
        np.broadcast_to(er[:, None, :], (rows, GRID_W, D_MODEL // 2)),
        np.broadcast_to(ec[None, :, :], (rows, GRID_W, D_MODEL // 2)),
    ], axis=-1).reshape(SEQ, D_MODEL).astype(np.float32)

    def dft(n):
        k = np.arange(n, dtype=np.int64)
        ang = 2.0 * np.pi * ((k[:, None] * k[None, :]) % n).astype(np.float64) / n
        return np.cos(ang) / math.sqrt(n), np.sin(ang) / math.sqrt(n)

    cc, sc = dft(FOURIER_GROUP_DIM)
    chan = np.concatenate([cc, sc], axis=1).astype(np.float32)
    ct, st = dft(SEQ)
    half, quarter = SEQ // 2, SEQ // 4
    time = np.zeros((2, 2, DFT_ROWS, half), np.float64)
    flip = np.zeros((2, quarter, DFT_ROWS), np.float32)
    for p in range(2):
        n_rows = quarter + 1 - p
        rows = 2 * np.arange(n_rows) + p
        time[p, 0, :n_rows] = ct[rows, :half]
        time[p, 1, :n_rows] = st[rows, :half]
        flip[p, np.arange(quarter), quarter - p - np.arange(quarter)] = 1.0
    return pos, chan, time.astype(np.float32), flip


def _mod_kernel(c_ref, w_ref, b_ref, o_ref):
    c = c_ref[...]
    a = c * _sigmoid(c)
    w = w_ref[...]
    a_hi = a.astype(BF16)
    a_lo = (a - a_hi.astype(F32)).astype(BF16)
    w_hi = w.astype(BF16)
    w_lo = (w - w_hi.astype(F32)).astype(BF16)
    n = a.shape[0]
    both = _dot(jnp.concatenate([a_hi, a_lo], axis=0), w_hi)
    o_ref[...] = both[:n] + _dot(a_hi, w_lo) + both[n:] + b_ref[...]


def _mod_call(cc, w_mod, b_mod):
    n = w_mod.shape[1]
    return pl.pallas_call(
        _mod_kernel,
        grid=(n // MOD_TN,),
        in_specs=[
            pl.BlockSpec((2 * SUBLANES, D_MODEL), lambda i: (0, 0)),
            pl.BlockSpec((D_MODEL, MOD_TN), lambda i: (0, i)),
            pl.BlockSpec((1, MOD_TN), lambda i: (0, i)),
        ],
        out_specs=pl.BlockSpec((2 * SUBLANES, MOD_TN), lambda i: (0, i)),
        out_shape=jax.ShapeDtypeStruct((2 * SUBLANES, n), F32),
        name="mod",
        compiler_params=pltpu.CompilerParams(dimension_semantics=("arbitrary",)),
    )(cc, w_mod, b_mod)


def _mod_spec(row_block, k):
    return pl.BlockSpec((BATCH, D_MODEL), lambda c: (row_block, k))


def _head_kernel(*refs, lat, n_chunks):
    refs = list(refs)
    xm_ref, xp_ref, xn_ref = refs[:3]
    refs = refs[3:]
    if lat:
        pm_ref, pp_ref, pn_ref = refs[:3]
        refs = refs[3:]
    (sh_ref, sc_ref, g_ref, w_ref, cw_ref, cb_ref, wg_ref, ba_ref, bx_ref, lam_ref) = refs[:10]
    refs = refs[10:]
    if lat:
        xt_out, uy_out, uf_out, ug_out, ls_out, pf_out, pb_out = refs[:7]
        refs = refs[7:]
    else:
        w_slab_ref, w_bf16_out = refs[:2]
        refs = refs[2:]
        w_bf16_out[...] = w_slab_ref[...].astype(BF16)
    e_out, xs_ref, hb_ref, a_ref, b_ref, xh_ref, gate_ref = refs

    c = pl.program_id(0)
    mod_rows = slice(None) if lat else slice(0, 1)

    xm = xm_ref[...]
    xp = xp_ref[...]
    xn = xn_ref[...]
    if lat:
        xm = xm + pm_ref[...][None]
        xp = xp + pp_ref[...][None]
        xn = xn + pn_ref[...][None]
    xs_ref[LEAD:LEAD + ROWS, :] = jnp.swapaxes(xm, 0, 1).reshape(ROWS, D_MODEL)
    xs_ref[0:LEAD, :] = jnp.swapaxes(xp, 0, 1)[HALO_T - CONV_PAD_LEFT:].reshape(LEAD, D_MODEL)
    xs_ref[LEAD + ROWS:XROWS, :] = (
        jnp.swapaxes(xn, 0, 1)[:CONV_WIDTH - 1 - CONV_PAD_LEFT].reshape(TRAIL, D_MODEL))
    xs_ref[XROWS:XROWS_PAD, :] = jnp.zeros((XROWS_PAD - XROWS, D_MODEL), F32)

    assert LEAD % BF16_ROWS == 0 and FRONT_ROWS % BF16_ROWS == 0
    blocks = ([(0, LEAD)] + [(r, r + FRONT_ROWS) for r in range(LEAD, LEAD + ROWS, FRONT_ROWS)]
              + [(LEAD + ROWS, XROWS_PAD)])
    for r0, r1 in blocks:
        xt = xs_ref[r0:r1, :]
        if lat and LEAD <= r0 < LEAD + ROWS:
            xt_out[r0 - LEAD:r1 - LEAD, :] = xt
        h = _rms(xt, g_ref[...]).reshape((r1 - r0) // BATCH, BATCH, D_MODEL)
        h = h * (1.0 + sc_ref[mod_rows, :][None]) + sh_ref[mod_rows, :][None]
        hb_ref[r0:r1, :] = h.reshape(r1 - r0, D_MODEL).astype(BF16)

    keep_first = jnp.where(c == 0, 0.0, 1.0)
    keep_last = jnp.where(c == n_chunks - 1, 0.0, 1.0)
    for n0 in range(0, LRU_WIDTH, NCHUNK):
        w_x = w_ref[:, n0:n0 + NCHUNK]
        xs_ref[:, n0:n0 + NCHUNK] = _dot(hb_ref[...], w_x if lat else w_x.astype(BF16))
    xs_ref[0:LEAD, :] = xs_ref[0:LEAD, :] * keep_first
    xs_ref[LEAD + ROWS:XROWS, :] = xs_ref[LEAD + ROWS:XROWS, :] * keep_last

    def branch_piece(n0):
        r = _dot(hb_ref[LEAD:LEAD + ROWS, :], w_ref[:, n0:n0 + NCHUNK])
        if n0 < 2 * LRU_WIDTH:
            uy_out[:, n0 - LRU_WIDTH:n0 - LRU_WIDTH + NCHUNK] = r.astype(BF16)
        elif n0 < 2 * LRU_WIDTH + FOURIER_WIDTH:
            assert NCHUNK == FOURIER_WIDTH
            uf_out[...] = r.reshape(TT, BATCH * NCHUNK).astype(BF16)
        else:
            c0 = n0 - 2 * LRU_WIDTH - FOURIER_WIDTH
            ug_out[:, c0:c0 + NCHUNK] = r.astype(BF16)

    pieces = list(range(LRU_WIDTH, IN_COLS, NCHUNK)) if lat else []
    per_head = -(-len(pieces) // LRU_HEADS)

    lam = lam_ref[...]
    sp = jnp.maximum(-lam, 0.0) + jnp.log1p(jnp.exp(-jnp.abs(lam)))
    c1 = (-0.5 * LRU_C * LOG2E) * sp

    def conv_gates(hd):
        sl = slice(hd * LRU_HEAD_DIM, (hd + 1) * LRU_HEAD_DIM)
        xh = 0.5 * cb_ref[:, sl]
        for k in range(CONV_WIDTH):
            xh = xh + (0.5 * cw_ref[k:k + 1, sl]) * xs_ref[k * BATCH:k * BATCH + ROWS, sl]
        xh_ref[hd % GATE_SLOTS] = xh
        xhb = xh.astype(BF16)
        for d in range(2):
            gate_ref[hd % GATE_SLOTS, d] = _dot(xhb, wg_ref[d, hd])

    for hd in range(GATE_AHEAD):
        conv_gates(hd)
    def coefficients(hd):
        sl = slice(hd * LRU_HEAD_DIM, (hd + 1) * LRU_HEAD_DIM)
        xh = xh_ref[hd % GATE_SLOTS]
        for d in range(2):
            g = gate_ref[hd % GATE_SLOTS, d]
            ta = jnp.tanh(g[:, :LRU_HEAD_DIM] + 0.5 * ba_ref[d:d + 1, sl])
            ti = jnp.tanh(g[:, LRU_HEAD_DIM:] + 0.5 * bx_ref[d:d + 1, sl])
            a = jnp.exp2(c1[d:d + 1, sl] * ta + c1[d:d + 1, sl])
            v = 1.0 - a * a
            m = v * lax.rsqrt(jnp.maximum(v, TINY))
            a_ref[d, :, sl] = a
            b_ref[d, :, sl] = (m * xh) * (ti + 1.0)

    def scans(hd):
        sl = slice(hd * LRU_HEAD_DIM, (hd + 1) * LRU_HEAD_DIM)
        lf = pf = lb = pb = None
        for t in range(TT):
            rf = slice(t * BATCH, (t + 1) * BATCH)
            rb = slice((TT - 1 - t) * BATCH, (TT - t) * BATCH)
            af, bf = a_ref[0, rf, sl], b_ref[0, rf, sl]
            ab, bb = a_ref[1, rb, sl], b_ref[1, rb, sl]
            if t == 0:
                lf, pf, lb, pb = bf, af, bb, ab
            else:
                lf, pf = af * lf + bf, af * pf
                lb, pb = ab * lb + bb, ab * pb
                b_ref[0, rf, sl] = lf
                a_ref[0, rf, sl] = pf
                b_ref[1, rb, sl] = lb
                a_ref[1, rb, sl] = pb
        e_out[0, :, sl] = lf
        e_out[1, :, sl] = pf
        e_out[2, :, sl] = lb
        e_out[3, :, sl] = pb
        if lat:
            ls_out[:, sl] = (b_ref[0, :, sl] + b_ref[1, :, sl]).astype(BF16)
            pf_out[:, sl] = a_ref[0, :, sl].astype(BF16)
            pb_out[:, sl] = a_ref[1, :, sl].astype(BF16)

    for hd in range(LRU_HEADS):
        for n0 in pieces[hd * per_head:(hd + 1) * per_head]:
            branch_piece(n0)
        coefficients(hd)
        if hd + GATE_AHEAD < LRU_HEADS:
            conv_gates(hd + GATE_AHEAD)
        scans(hd)


def _head_call(x, pos, mod, g, w, conv_w, conv_b, wg, b_a, b_x, lam, *, lat):
    seq = x.shape[1]
    n_chunks = seq // TT
    n_halo = seq // HALO_T
    per_chunk = TT // HALO_T

    def const(shape):
        return pl.BlockSpec(shape, lambda c: (0,) * len(shape))

    def prev_idx(c):
        return jnp.maximum(c * per_chunk - 1, 0)

    def next_idx(c):
        return jnp.minimum((c + 1) * per_chunk, n_halo - 1)

    in_specs = [
        pl.BlockSpec((BATCH, TT, D_MODEL), lambda c: (0, c, 0)),
        pl.BlockSpec((BATCH, HALO_T, D_MODEL), lambda c: (0, prev_idx(c), 0)),
        pl.BlockSpec((BATCH, HALO_T, D_MODEL), lambda c: (0, next_idx(c), 0)),
    ]
    args = [x, x, x]
    if lat:
        in_specs += [
            pl.BlockSpec((TT, D_MODEL), lambda c: (c, 0)),
            pl.BlockSpec((HALO_T, D_MODEL), lambda c: (prev_idx(c), 0)),
            pl.BlockSpec((HALO_T, D_MODEL), lambda c: (next_idx(c), 0)),
        ]
        args += [pos, pos, pos]
    in_specs += [
        _mod_spec(0 if lat else 1, 0), _mod_spec(0 if lat else 1, 1), const((1, D_MODEL)),
        const((D_MODEL, IN_COLS if lat else LRU_WIDTH)),
        const((CONV_WIDTH, LRU_WIDTH)), const((1, LRU_WIDTH)),
        const((2, LRU_HEADS, LRU_HEAD_DIM, 2 * LRU_HEAD_DIM)),
        const((2, LRU_WIDTH)), const((2, LRU_WIDTH)), const((2, LRU_WIDTH)),
    ]
    args += [mod, mod, g, w, conv_w, conv_b, wg, b_a, b_x, lam]
    w_slab = pl.BlockSpec((D_MODEL // n_chunks, IN_COLS), lambda c: (c, 0))
    if not lat:
        in_specs.append(w_slab)
        args.append(w)

    def rows(width):
        return pl.BlockSpec((ROWS, width), lambda c: (c, 0))

    out_specs, out_shape = [], []
    if lat:
        n = n_chunks * ROWS
        out_specs += [rows(D_MODEL), rows(LRU_WIDTH),
                      pl.BlockSpec((TT, BATCH * FOURIER_WIDTH), lambda c: (c, 0)),
                      rows(2 * D_MODEL), rows(LRU_WIDTH), rows(LRU_WIDTH), rows(LRU_WIDTH)]
        out_shape += [
            jax.ShapeDtypeStruct((n, D_MODEL), F32),
            jax.ShapeDtypeStruct((n, LRU_WIDTH), BF16),
            jax.ShapeDtypeStruct((seq, BATCH * FOURIER_WIDTH), BF16),
            jax.ShapeDtypeStruct((n, 2 * D_MODEL), BF16),
            jax.ShapeDtypeStruct((n, LRU_WIDTH), BF16),
            jax.ShapeDtypeStruct((n, LRU_WIDTH), BF16),
            jax.ShapeDtypeStruct((n, LRU_WIDTH), BF16),
        ]
    else:
        out_specs.append(w_slab)
        out_shape.append(jax.ShapeDtypeStruct((D_MODEL, IN_COLS), BF16))
    out_specs.append(pl.BlockSpec((None, 4, BATCH, LRU_WIDTH), lambda c: (c, 0, 0, 0)))
    out_shape.append(jax.ShapeDtypeStruct((n_chunks, 4, BATCH, LRU_WIDTH), F32))

    return pl.pallas_call(
        functools.partial(_head_kernel, lat=lat, n_chunks=n_chunks),
        grid=(n_chunks,),
        in_specs=in_specs,
        out_specs=out_specs,
        out_shape=out_shape,
        scratch_shapes=[
            pltpu.VMEM((XROWS_PAD, D_MODEL), F32),
            pltpu.VMEM((XROWS_PAD, D_MODEL), BF16),
            pltpu.VMEM((2, ROWS, LRU_WIDTH), F32),
            pltpu.VMEM((2, ROWS, LRU_WIDTH), F32),
            pltpu.VMEM((GATE_SLOTS, ROWS, LRU_HEAD_DIM), F32),
            pltpu.VMEM((GATE_SLOTS, 2, ROWS, 2 * LRU_HEAD_DIM), F32),
        ],
        name="head_lat" if lat else "head_ctx",
        compiler_params=pltpu.CompilerParams(
            dimension_semantics=("arbitrary",), vmem_limit_bytes=VMEM_LIMIT),
    )(*args)


def _carry_kernel(ec_ref, el_ref, hf_ref, hb_ref):
    n_ctx, n_lat = ec_ref.shape[0], el_ref.shape[0]
    h = jnp.zeros((BATCH, LRU_WIDTH), F32)
    for c in range(n_ctx):
        h = ec_ref[c, 1] * h + ec_ref[c, 0]
    for c in range(n_lat):
        hf_ref[c] = h
        h = el_ref[c, 1] * h + el_ref[c, 0]
    h = jnp.zeros((BATCH, LRU_WIDTH), F32)
    for c in reversed(range(n_ctx)):
        h = ec_ref[c, 3] * h + ec_ref[c, 2]
    for c in reversed(range(n_lat)):
        hb_ref[c] = h
        h = el_ref[c, 3] * h + el_ref[c, 2]


def _carry_call(e_ctx, e_lat):
    shape = jax.ShapeDtypeStruct((e_lat.shape[0], BATCH, LRU_WIDTH), F32)
    return pl.pallas_call(_carry_kernel, out_shape=[shape, shape], name="carry")(e_ctx, e_lat)


def _fourier_kernel(x_ref, chan_ref, time_ref, flip_ref, *refs):
    n_w = len(TAIL_WEIGHT_ROWS)
    w_f32_refs, o_ref, w_bf16_refs, z_ref = refs[:n_w], refs[n_w], refs[n_w + 1:2 * n_w + 1], refs[-1]
    for src, dst in zip(w_f32_refs, w_bf16_refs):
        dst[...] = src[...].astype(BF16)

    half, quarter = SEQ // 2, SEQ // 4
    for g in range(FOURIER_GROUPS):
        sl = slice(g * FOURIER_GROUP_DIM, (g + 1) * FOURIER_GROUP_DIM)
        r = _dot(x_ref[:, sl], chan_ref[...])
        lo, hi = r[:half], r[half:]
        for p, v in enumerate((lo + hi, lo - hi)):
            z_ref[p, 0:half, sl] = v[:, :FOURIER_GROUP_DIM].astype(BF16)
            z_ref[p, half:SEQ, sl] = v[:, FOURIER_GROUP_DIM:].astype(BF16)
    for p in range(2):
        cos_part = _dot(time_ref[p, 0], z_ref[p, 0:half, :])
        sin_part = _dot(time_ref[p, 1], z_ref[p, half:SEQ, :])
        o_ref[p, 0:quarter, :] = (cos_part - sin_part)[0:quarter].astype(BF16)
        mirrored = (cos_part + sin_part).astype(BF16)
        o_ref[p, quarter:half, :] = _dot(flip_ref[p], mirrored).astype(BF16)


def _fourier_call(uf, chan, time, flip, weights):
    def slab(w):
        return pl.BlockSpec((w.shape[0] // BATCH, w.shape[1]), lambda n: (n, 0))

    assert tuple(w.shape[0] for w in weights) == TAIL_WEIGHT_ROWS
    return pl.pallas_call(
        _fourier_kernel,
        grid=(BATCH,),
        in_specs=[
            pl.BlockSpec((SEQ, FOURIER_WIDTH), lambda n: (0, n)),
            pl.BlockSpec((FOURIER_GROUP_DIM, 2 * FOURIER_GROUP_DIM), lambda n: (0, 0)),
            pl.BlockSpec((2, 2, DFT_ROWS, SEQ // 2), lambda n: (0, 0, 0, 0)),
            pl.BlockSpec((2, SEQ // 4, DFT_ROWS), lambda n: (0, 0, 0)),
        ] + [slab(w) for w in weights],
        out_specs=[pl.BlockSpec((2, SEQ // 2, FOURIER_WIDTH), lambda n: (0, 0, n))]
        + [slab(w) for w in weights],
        out_shape=[jax.ShapeDtypeStruct((2, SEQ // 2, BATCH * FOURIER_WIDTH), BF16)]
        + [jax.ShapeDtypeStruct(w.shape, BF16) for w in weights],
        scratch_shapes=[pltpu.VMEM((2, SEQ, FOURIER_WIDTH), BF16)],
        name="fourier",
        compiler_params=pltpu.CompilerParams(
            dimension_semantics=("arbitrary",), vmem_limit_bytes=VMEM_LIMIT),
    )(uf, chan, time, flip, *weights)


def _tail_kernel(ls_ref, pf_ref, pb_ref, hf_ref, hb_ref, uy_ref, ug_ref, yf_ref, xt_ref,
                 gt1_ref, sh2_ref, sc2_ref, gt2_ref, gm_ref, gf_ref,
                 wl_ref, wf_ref, wo_ref, w1_ref, w2_ref, o_hbm, obuf_ref, osem):
    def tb(v):
        return v.reshape(TT, BATCH, v.shape[-1])

    def flat(v):
        return v.reshape(ROWS, v.shape[-1])

    yf = yf_ref[...].astype(F32).reshape(2, TT // 2, BATCH, FOURIER_WIDTH)
    yf = jnp.swapaxes(yf, 0, 1).reshape(ROWS, FOURIER_WIDTH).astype(BF16)
    y_b = _dot(yf, wf_ref[...])

    hf = jnp.tile(hf_ref[...].astype(BF16), (TT, 1))
    hb = jnp.tile(hb_ref[...].astype(BF16), (TT, 1))
    y_lru = ls_ref[...] + pf_ref[...] * hf + pb_ref[...] * hb
    z = y_lru * _gelu_tanh(uy_ref[...])
    y_a = _dot(z, wl_ref[...])

    g_a = _sigmoid(ug_ref[:, :D_MODEL].astype(F32))
    g_b = _sigmoid(ug_ref[:, D_MODEL:].astype(F32))
    merged = (g_a * y_a + g_b * y_b).astype(BF16)
    x2 = tb(xt_ref[...]) + gt1_ref[...][None] * tb(_dot(merged, wo_ref[...]))

    h = (tb(_rms(flat(x2), gm_ref[...])) * (1.0 + sc2_ref[...][None]) + sh2_ref[...][None])
    h = flat(h).astype(BF16)
    acc = jnp.zeros((ROWS, D_MODEL), F32)
    for k0 in range(0, D_FF, D_MODEL):
        a = jnp.maximum(_dot(h, w1_ref[:, k0:k0 + D_MODEL]), 0.0)
        acc = acc + _dot((a * a).astype(BF16), w2_ref[k0:k0 + D_MODEL, :])
    out = _rms(flat(x2 + gt2_ref[...][None] * tb(acc)), gf_ref[...])

    c = pl.program_id(0)
    slot = c % 2

    def copies(step, slot_):
        return [pltpu.make_async_copy(obuf_ref.at[slot_, :, b, :],
                                      o_hbm.at[b, pl.ds(step * TT, TT), :], osem.at[slot_, b])
                for b in range(BATCH)]

    @pl.when(c >= 2)
    def _():
        for cp in copies(c - 2, slot):
            cp.wait()

    obuf_ref[slot] = tb(out)
    for cp in copies(c, slot):
        cp.start()

    @pl.when(c == N_LAT_CHUNKS - 1)
    def _():
        for cp in copies(c - 1, 1 - slot) + copies(c, slot):
            cp.wait()


def _tail_call(ls, pf, pb, hf, hb, uy, ug, yf, xt, mod, gm, gf, wl, wf, wo, w1, w2):
    def rows(width):
        return pl.BlockSpec((ROWS, width), lambda c: (c, 0))

    def const(shape):
        return pl.BlockSpec(shape, lambda c: (0,) * len(shape))

    state = pl.BlockSpec((None, BATCH, LRU_WIDTH), lambda c: (c, 0, 0))
    vec1 = const((1, D_MODEL))
    return pl.pallas_call(
        _tail_kernel,
        grid=(N_LAT_CHUNKS,),
        in_specs=[
            rows(LRU_WIDTH), rows(LRU_WIDTH), rows(LRU_WIDTH), state, state,
            rows(LRU_WIDTH), rows(2 * D_MODEL),
            pl.BlockSpec((2, TT // 2, BATCH * FOURIER_WIDTH), lambda c: (0, c, 0)),
            rows(D_MODEL),
            _mod_spec(0, 2), _mod_spec(0, 3), _mod_spec(0, 4), _mod_spec(0, 5), vec1, vec1,
            const((LRU_WIDTH, D_MODEL)), const((FOURIER_WIDTH, D_MODEL)), const((D_MODEL, D_MODEL)),
            const((D_MODEL, D_FF)), const((D_FF, D_MODEL)),
        ],
        out_specs=pl.BlockSpec(memory_space=pl.ANY),
        out_shape=jax.ShapeDtypeStruct((BATCH, SEQ, D_MODEL), F32),
        scratch_shapes=[pltpu.VMEM((2, TT, BATCH, D_MODEL), F32), pltpu.SemaphoreType.DMA((2, BATCH))],
        name="tail",
        compiler_params=pltpu.CompilerParams(
            dimension_semantics=("arbitrary",), vmem_limit_bytes=VMEM_LIMIT),
    )(ls, pf, pb, hf, hb, uy, ug, yf, xt, mod, mod, mod, mod, gm, gf, wl, wf, wo, w1, w2)


def kernel(x, c, ctx, c_ctx, w_mod, b_mod, g_mix, w_in, conv_w, conv_b, w_a, b_a, w_x, b_x,
           lam, w_lru_out, w_f_out, w_out, g_mlp, w1, w2, g_final):
    pos_np, chan_np, time_np, flip_np = _constants()
    pos = jnp.asarray(pos_np)
    chan, time, flip = (jnp.asarray(v).astype(BF16) for v in (chan_np, time_np, flip_np))

    cc = jnp.concatenate(
        [c, c_ctx[None], jnp.zeros((2 * SUBLANES - BATCH - 1, D_MODEL), F32)], axis=0)
    mod = _mod_call(cc, w_mod[0], b_mod[0][None])

    g_mix2 = g_mix[0][None]
    wg = jnp.concatenate([w_a[0], w_x[0]], axis=-1).astype(BF16)
    lru = (conv_w[0], conv_b[0][None], wg, b_a[0], b_x[0], lam[0])

    w_in_b, e_ctx = _head_call(ctx, None, mod, g_mix2, w_in[0], *lru, lat=False)
    xt, uy, uf, ug, ls, pf, pb, e_lat = _head_call(x, pos, mod, g_mix2, w_in_b, *lru, lat=True)
    hf, hb = _carry_call(e_ctx, e_lat)
    yf, wl, wf, wo, w1b, w2b = _fourier_call(
        uf, chan, time, flip, (w_lru_out[0], w_f_out[0], w_out[0], w1[0], w2[0]))
    return _tail_call(ls, pf, pb, hf, hb, uy, ug, yf, xt, mod,
                      g_mlp[0][None], g_final[None], wl, wf, wo, w1b, w2b)
```

```python
import functools
import math

import numpy as np
import jax
import jax.numpy as jnp
from jax import lax
from jax.experimental import pallas as pl
from jax.experimental.pallas import tpu as pltpu

F32 = jnp.float32
BF16 = jnp.bfloat16

D_MODEL = 1024
BATCH = 8
SEQ = 2048
CTX_LEN = 256
GRID_W = 64
LRU_WIDTH = 1024
LRU_HEADS = 8
LRU_HEAD_DIM = LRU_WIDTH // LRU_HEADS
LRU_C = 8.0
CONV_WIDTH = 4
CONV_PAD_LEFT = 2
FOURIER_WIDTH = 512
FOURIER_GROUPS = 4
FOURIER_GROUP_DIM = FOURIER_WIDTH // FOURIER_GROUPS
IN_COLS = 2 * LRU_WIDTH + FOURIER_WIDTH + 2 * D_MODEL
D_FF = 4 * D_MODEL
N_MOD = 6
EPS = 1e-6
POS_MAX_PERIOD = 10000.0

SUBLANES = 8
BF16_ROWS = 16
assert BATCH == SUBLANES

TT = 64
ROWS = TT * BATCH
HALO_T = SUBLANES
LEAD = CONV_PAD_LEFT * BATCH
TRAIL = (CONV_WIDTH - 1 - CONV_PAD_LEFT) * BATCH
XROWS = LEAD + ROWS + TRAIL
XROWS_PAD = -(-XROWS // BF16_ROWS) * BF16_ROWS
N_CTX_CHUNKS = CTX_LEN // TT
N_LAT_CHUNKS = SEQ // TT
NCHUNK = 512
FRONT_ROWS = 128
GATE_AHEAD = 1
GATE_SLOTS = GATE_AHEAD + 1
DFT_ROWS = -(-(SEQ // 4 + 1) // BF16_ROWS) * BF16_ROWS
TAIL_WEIGHT_ROWS = (LRU_WIDTH, FOURIER_WIDTH, D_MODEL, D_MODEL, D_FF)
MOD_TN = 1024
V7X_VMEM_BYTES = 64 * 1024 * 1024
VMEM_RESERVE = 6 * 1024 * 1024
VMEM_LIMIT = V7X_VMEM_BYTES - VMEM_RESERVE
LOG2E = math.log2(math.e)
TINY = 1e-30


def _gelu_tanh(x):
    k0 = jnp.full((1, 1), math.sqrt(2.0 / math.pi), F32).astype(x.dtype)
    k1 = jnp.full((1, 1), 0.044715, F32).astype(x.dtype)
    return 0.5 * x * (1.0 + jnp.tanh(k0 * (x + k1 * (x * x * x))))


def _sigmoid(x):
    return 0.5 * jnp.tanh(0.5 * x) + 0.5


def _dot(a, b):
    return jnp.dot(a, b, preferred_element_type=F32)


def _rms(x, g):
    ms = jnp.mean(x * x, axis=-1, keepdims=True)
    return x * lax.rsqrt(ms + EPS) * g


@functools.cache
def _constants():
    half = D_MODEL // 4
    freqs = np.exp(-math.log(POS_MAX_PERIOD) * np.arange(half, dtype=np.float32) / half).astype(np.float32)

    def sincos(n):
        ang = np.arange(n, dtype=np.float32)[:, None] * freqs[None, :]
        return np.concatenate([np.sin(ang), np.cos(ang)], axis=-1).astype(np.float32)

    rows = SEQ // GRID_W
    er, ec = sincos(rows), sincos(GRID_W)
    pos = np.concatenate([
        np.broadcast_to(er[:, None, :], (rows, GRID_W, D_MODEL // 2)),
        np.broadcast_to(ec[None, :, :], (rows, GRID_W, D_MODEL // 2)),
    ], axis=-1).reshape(SEQ, D_MODEL).astype(np.float32)

    def dft(n):
        k = np.arange(n, dtype=np.int64)
        ang = 2.0 * np.pi * ((k[:, None] * k[None, :]) % n).astype(np.float64) / n
        return np.cos(ang) / math.sqrt(n), np.sin(ang) / math.sqrt(n)

    cc, sc = dft(FOURIER_GROUP_DIM)
    chan = np.concatenate([cc, sc], axis=1).astype(np.float32)
    ct, st = dft(SEQ)
    half, quarter = SEQ // 2, SEQ // 4
    time = np.zeros((2, 2, DFT_ROWS, half), np.float64)
    flip = np.zeros((2, quarter, DFT_ROWS), np.float32)
    for p in range(2):
        n_rows = quarter + 1 - p
        rows = 2 * np.arange(n_rows) + p
        time[p, 0, :n_rows] = ct[rows, :half]
        time[p, 1, :n_rows] = st[rows, :half]
        flip[p, np.arange(quarter), quarter - p - np.arange(quarter)] = 1.0
    return pos, chan, time.astype(np.float32), flip


def _mod_kernel(c_ref, w_ref, b_ref, o_ref):
    c = c_ref[...]
    a = c * _sigmoid(c)
    w = w_ref[...]
    a_hi = a.astype(BF16)
    a_lo = (a - a_hi.astype(F32)).astype(BF16)
    w_hi = w.astype(BF16)
    w_lo = (w - w_hi.astype(F32)).astype(BF16)
    n = a.shape[0]
    both = _dot(jnp.concatenate([a_hi, a_lo], axis=0), w_hi)
    o_ref[...] = both[:n] + _dot(a_hi, w_lo) + both[n:] + b_ref[...]


def _mod_call(cc, w_mod, b_mod):
    n = w_mod.shape[1]
    return pl.pallas_call(
        _mod_kernel,
        grid=(n // MOD_TN,),
        in_specs=[
            pl.BlockSpec((2 * SUBLANES, D_MODEL), lambda i: (0, 0)),
            pl.BlockSpec((D_MODEL, MOD_TN), lambda i: (0, i)),
            pl.BlockSpec((1, MOD_TN), lambda i: (0, i)),
        ],
        out_specs=pl.BlockSpec((2 * SUBLANES, MOD_TN), lambda i: (0, i)),
        out_shape=jax.ShapeDtypeStruct((2 * SUBLANES, n), F32),
        name="mod",
        compiler_params=pltpu.CompilerParams(dimension_semantics=("arbitrary",)),
    )(cc, w_mod, b_mod)


def _mod_spec(row_block, k):
    return pl.BlockSpec((BATCH, D_MODEL), lambda c: (row_block, k))


def _head_kernel(*refs, lat, n_chunks):
    refs = list(refs)
    x_hbm, xp_ref, xn_ref = refs[:3]
    refs = refs[3:]
    if lat:
        pm_ref, pp_ref, pn_ref = refs[:3]
        refs = refs[3:]
    (sh_ref, sc_ref, g_ref, w_ref, cw_ref, cb_ref, wg_ref, ba_ref, bx_ref, lam_ref) = refs[:10]
    refs = refs[10:]
    if lat:
        xt_out, uy_out, uf_out, ug_out, ls_out, pf_out, pb_out = refs[:7]
        refs = refs[7:]
    else:
        w_slab_ref, w_bf16_out = refs[:2]
        refs = refs[2:]
        w_bf16_out[...] = w_slab_ref[...].astype(BF16)
    e_out, xs_ref, hb_ref, a_ref, b_ref, xh_ref, gate_ref, xbuf_ref, xsem = refs

    c = pl.program_id(0)
    mod_rows = slice(None) if lat else slice(0, 1)

    def x_copies(chunk, slot_):
        return [pltpu.make_async_copy(x_hbm.at[b, pl.ds(pl.multiple_of(chunk * TT, TT), TT), :],
                                      xbuf_ref.at[slot_, :, b, :], xsem.at[slot_, b])
                for b in range(BATCH)]

    slot = c % 2

    @pl.when(c == 0)
    def _():
        for cp in x_copies(c, slot):
            cp.start()

    @pl.when(c + 1 < n_chunks)
    def _():
        for cp in x_copies(c + 1, 1 - slot):
            cp.start()

    for cp in x_copies(c, slot):
        cp.wait()

    xm = xbuf_ref[slot]
    xp = xp_ref[...]
    xn = xn_ref[...]
    if lat:
        xm = xm + pm_ref[...][:, None, :]
        xp = xp + pp_ref[...][None]
        xn = xn + pn_ref[...][None]
    xs_ref[LEAD:LEAD + ROWS, :] = xm.reshape(ROWS, D_MODEL)
    xs_ref[0:LEAD, :] = jnp.swapaxes(xp, 0, 1)[HALO_T - CONV_PAD_LEFT:].reshape(LEAD, D_MODEL)
    xs_ref[LEAD + ROWS:XROWS, :] = (
        jnp.swapaxes(xn, 0, 1)[:CONV_WIDTH - 1 - CONV_PAD_LEFT].reshape(TRAIL, D_MODEL))
    xs_ref[XROWS:XROWS_PAD, :] = jnp.zeros((XROWS_PAD - XROWS, D_MODEL), F32)

    assert LEAD % BF16_ROWS == 0 and FRONT_ROWS % BF16_ROWS == 0
    blocks = ([(0, LEAD)] + [(r, r + FRONT_ROWS) for r in range(LEAD, LEAD + ROWS, FRONT_ROWS)]
              + [(LEAD + ROWS, XROWS_PAD)])
    for r0, r1 in blocks:
        xt = xs_ref[r0:r1, :]
        if lat and LEAD <= r0 < LEAD + ROWS:
            xt_out[r0 - LEAD:r1 - LEAD, :] = xt
        h = _rms(xt, g_ref[...]).reshape((r1 - r0) // BATCH, BATCH, D_MODEL)
        h = h * (1.0 + sc_ref[mod_rows, :][None]) + sh_ref[mod_rows, :][None]
        hb_ref[r0:r1, :] = h.reshape(r1 - r0, D_MODEL).astype(BF16)

    keep_first = jnp.where(c == 0, 0.0, 1.0)
    keep_last = jnp.where(c == n_chunks - 1, 0.0, 1.0)
    for n0 in range(0, LRU_WIDTH, NCHUNK):
        w_x = w_ref[:, n0:n0 + NCHUNK]
        xs_ref[:, n0:n0 + NCHUNK] = _dot(hb_ref[...], w_x if lat else w_x.astype(BF16))
    xs_ref[0:LEAD, :] = xs_ref[0:LEAD, :] * keep_first
    xs_ref[LEAD + ROWS:XROWS, :] = xs_ref[LEAD + ROWS:XROWS, :] * keep_last

    def branch_piece(n0):
        r = _dot(hb_ref[LEAD:LEAD + ROWS, :], w_ref[:, n0:n0 + NCHUNK])
        if n0 < 2 * LRU_WIDTH:
            uy_out[:, n0 - LRU_WIDTH:n0 - LRU_WIDTH + NCHUNK] = r.astype(BF16)
        elif n0 < 2 * LRU_WIDTH + FOURIER_WIDTH:
            assert NCHUNK == FOURIER_WIDTH
            uf_out[...] = r.reshape(TT, BATCH * NCHUNK).astype(BF16)
        else:
            c0 = n0 - 2 * LRU_WIDTH - FOURIER_WIDTH
            ug_out[:, c0:c0 + NCHUNK] = r.astype(BF16)

    pieces = list(range(LRU_WIDTH, IN_COLS, NCHUNK)) if lat else []
    per_head = -(-len(pieces) // LRU_HEADS)

    lam = lam_ref[...]
    sp = jnp.maximum(-lam, 0.0) + jnp.log1p(jnp.exp(-jnp.abs(lam)))
    c1 = (-0.5 * LRU_C * LOG2E) * sp

    def conv_gates(hd):
        sl = slice(hd * LRU_HEAD_DIM, (hd + 1) * LRU_HEAD_DIM)
        xh = 0.5 * cb_ref[:, sl]
        for k in range(CONV_WIDTH):
            xh = xh + (0.5 * cw_ref[k:k + 1, sl]) * xs_ref[k * BATCH:k * BATCH + ROWS, sl]
        xh_ref[hd % GATE_SLOTS] = xh
        xhb = xh.astype(BF16)
        for d in range(2):
            gate_ref[hd % GATE_SLOTS, d] = _dot(xhb, wg_ref[d, hd])

    for hd in range(GATE_AHEAD):
        conv_gates(hd)
    def coefficients(hd):
        sl = slice(hd * LRU_HEAD_DIM, (hd + 1) * LRU_HEAD_DIM)
        xh = xh_ref[hd % GATE_SLOTS]
        for d in range(2):
            g = gate_ref[hd % GATE_SLOTS, d]
            ta = jnp.tanh(g[:, :LRU_HEAD_DIM] + 0.5 * ba_ref[d:d + 1, sl])
            ti = jnp.tanh(g[:, LRU_HEAD_DIM:] + 0.5 * bx_ref[d:d + 1, sl])
            a = jnp.exp2(c1[d:d + 1, sl] * ta + c1[d:d + 1, sl])
            v = 1.0 - a * a
            m = v * lax.rsqrt(jnp.maximum(v, TINY))
            a_ref[d, :, sl] = a
            b_ref[d, :, sl] = (m * xh) * (ti + 1.0)

    def scans(hd):
        sl = slice(hd * LRU_HEAD_DIM, (hd + 1) * LRU_HEAD_DIM)
        lf = pf = lb = pb = None
        for t in range(TT):
            rf = slice(t * BATCH, (t + 1) * BATCH)
            rb = slice((TT - 1 - t) * BATCH, (TT - t) * BATCH)
            af, bf = a_ref[0, rf, sl], b_ref[0, rf, sl]
            ab, bb = a_ref[1, rb, sl], b_ref[1, rb, sl]
            if t == 0:
                lf, pf, lb, pb = bf, af, bb, ab
            else:
                lf, pf = af * lf + bf, af * pf
                lb, pb = ab * lb + bb, ab * pb
                b_ref[0, rf, sl] = lf
                a_ref[0, rf, sl] = pf
                b_ref[1, rb, sl] = lb
                a_ref[1, rb, sl] = pb
        e_out[0, :, sl] = lf
        e_out[1, :, sl] = pf
        e_out[2, :, sl] = lb
        e_out[3, :, sl] = pb
        if lat:
            ls_out[:, sl] = (b_ref[0, :, sl] + b_ref[1, :, sl]).astype(BF16)
            pf_out[:, sl] = a_ref[0, :, sl].astype(BF16)
            pb_out[:, sl] = a_ref[1, :, sl].astype(BF16)

    for hd in range(LRU_HEADS):
        for n0 in pieces[hd * per_head:(hd + 1) * per_head]:
            branch_piece(n0)
        coefficients(hd)
        if hd + GATE_AHEAD < LRU_HEADS:
            conv_gates(hd + GATE_AHEAD)
        scans(hd)


def _head_call(x, pos, mod, g, w, conv_w, conv_b, wg, b_a, b_x, lam, *, lat):
    seq = x.shape[1]
    n_chunks = seq // TT
    n_halo = seq // HALO_T
    per_chunk = TT // HALO_T

    def const(shape):
        return pl.BlockSpec(shape, lambda c: (0,) * len(shape))

    def prev_idx(c):
        return jnp.maximum(c * per_chunk - 1, 0)

    def next_idx(c):
        return jnp.minimum((c + 1) * per_chunk, n_halo - 1)

    in_specs = [
        pl.BlockSpec(memory_space=pl.ANY),
        pl.BlockSpec((BATCH, HALO_T, D_MODEL), lambda c: (0, prev_idx(c), 0)),
        pl.BlockSpec((BATCH, HALO_T, D_MODEL), lambda c: (0, next_idx(c), 0)),
    ]
    args = [x, x, x]
    if lat:
        in_specs += [
            pl.BlockSpec((TT, D_MODEL), lambda c: (c, 0)),
            pl.BlockSpec((HALO_T, D_MODEL), lambda c: (prev_idx(c), 0)),
            pl.BlockSpec((HALO_T, D_MODEL), lambda c: (next_idx(c), 0)),
        ]
        args += [pos, pos, pos]
    in_specs += [
        _mod_spec(0 if lat else 1, 0), _mod_spec(0 if lat else 1, 1), const((1, D_MODEL)),
        const((D_MODEL, IN_COLS if lat else LRU_WIDTH)),
        const((CONV_WIDTH, LRU_WIDTH)), const((1, LRU_WIDTH)),
        const((2, LRU_HEADS, LRU_HEAD_DIM, 2 * LRU_HEAD_DIM)),
        const((2, LRU_WIDTH)), const((2, LRU_WIDTH)), const((2, LRU_WIDTH)),
    ]
    args += [mod, mod, g, w, conv_w, conv_b, wg, b_a, b_x, lam]
    w_slab = pl.BlockSpec((D_MODEL // n_chunks, IN_COLS), lambda c: (c, 0))
    if not lat:
        in_specs.append(w_slab)
        args.append(w)

    def rows(width):
        return pl.BlockSpec((ROWS, width), lambda c: (c, 0))

    out_specs, out_shape = [], []
    if lat:
        n = n_chunks * ROWS
        out_specs += [rows(D_MODEL), rows(LRU_WIDTH),
                      pl.BlockSpec((TT, BATCH * FOURIER_WIDTH), lambda c: (c, 0)),
                      rows(2 * D_MODEL), rows(LRU_WIDTH), rows(LRU_WIDTH), rows(LRU_WIDTH)]
        out_shape += [
            jax.ShapeDtypeStruct((n, D_MODEL), F32),
            jax.ShapeDtypeStruct((n, LRU_WIDTH), BF16),
            jax.ShapeDtypeStruct((seq, BATCH * FOURIER_WIDTH), BF16),
            jax.ShapeDtypeStruct((n, 2 * D_MODEL), BF16),
            jax.ShapeDtypeStruct((n, LRU_WIDTH), BF16),
            jax.ShapeDtypeStruct((n, LRU_WIDTH), BF16),
            jax.ShapeDtypeStruct((n, LRU_WIDTH), BF16),
        ]
    else:
        out_specs.append(w_slab)
        out_shape.append(jax.ShapeDtypeStruct((D_MODEL, IN_COLS), BF16))
    out_specs.append(pl.BlockSpec((None, 4, BATCH, LRU_WIDTH), lambda c: (c, 0, 0, 0)))
    out_shape.append(jax.ShapeDtypeStruct((n_chunks, 4, BATCH, LRU_WIDTH), F32))

    return pl.pallas_call(
        functools.partial(_head_kernel, lat=lat, n_chunks=n_chunks),
        grid=(n_chunks,),
        in_specs=in_specs,
        out_specs=out_specs,
        out_shape=out_shape,
        scratch_shapes=[
            pltpu.VMEM((XROWS_PAD, D_MODEL), F32),
            pltpu.VMEM((XROWS_PAD, D_MODEL), BF16),
            pltpu.VMEM((2, ROWS, LRU_WIDTH), F32),
            pltpu.VMEM((2, ROWS, LRU_WIDTH), F32),
            pltpu.VMEM((GATE_SLOTS, ROWS, LRU_HEAD_DIM), F32),
            pltpu.VMEM((GATE_SLOTS, 2, ROWS, 2 * LRU_HEAD_DIM), F32),
            pltpu.VMEM((2, TT, BATCH, D_MODEL), F32),
            pltpu.SemaphoreType.DMA((2, BATCH)),
        ],
        name="head_lat" if lat else "head_ctx",
        compiler_params=pltpu.CompilerParams(
            dimension_semantics=("arbitrary",), vmem_limit_bytes=VMEM_LIMIT),
    )(*args)


def _carry_kernel(ec_ref, el_ref, hf_ref, hb_ref):
    n_ctx, n_lat = ec_ref.shape[0], el_ref.shape[0]
    h = jnp.zeros((BATCH, LRU_WIDTH), F32)
    for c in range(n_ctx):
        h = ec_ref[c, 1] * h + ec_ref[c, 0]
    for c in range(n_lat):
        hf_ref[c] = h
        h = el_ref[c, 1] * h + el_ref[c, 0]
    h = jnp.zeros((BATCH, LRU_WIDTH), F32)
    for c in reversed(range(n_ctx)):
        h = ec_ref[c, 3] * h + ec_ref[c, 2]
    for c in reversed(range(n_lat)):
        hb_ref[c] = h
        h = el_ref[c, 3] * h + el_ref[c, 2]


def _carry_call(e_ctx, e_lat):
    shape = jax.ShapeDtypeStruct((e_lat.shape[0], BATCH, LRU_WIDTH), F32)
    return pl.pallas_call(_carry_kernel, out_shape=[shape, shape], name="carry")(e_ctx, e_lat)


def _fourier_kernel(x_ref, chan_ref, time_ref, flip_ref, *refs):
    n_w = len(TAIL_WEIGHT_ROWS)
    w_f32_refs, o_ref, w_bf16_refs, z_ref = refs[:n_w], refs[n_w], refs[n_w + 1:2 * n_w + 1], refs[-1]
    for src, dst in zip(w_f32_refs, w_bf16_refs):
        dst[...] = src[...].astype(BF16)

    half, quarter = SEQ // 2, SEQ // 4
    for g in range(FOURIER_GROUPS):
        sl = slice(g * FOURIER_GROUP_DIM, (g + 1) * FOURIER_GROUP_DIM)
        r = _dot(x_ref[:, sl], chan_ref[...])
        lo, hi = r[:half], r[half:]
        for p, v in enumerate((lo + hi, lo - hi)):
            z_ref[p, 0:half, sl] = v[:, :FOURIER_GROUP_DIM].astype(BF16)
            z_ref[p, half:SEQ, sl] = v[:, FOURIER_GROUP_DIM:].astype(BF16)
    for p in range(2):
        cos_part = _dot(time_ref[p, 0], z_ref[p, 0:half, :])
        sin_part = _dot(time_ref[p, 1], z_ref[p, half:SEQ, :])
        o_ref[p, 0:quarter, :] = (cos_part - sin_part)[0:quarter].astype(BF16)
        mirrored = (cos_part + sin_part).astype(BF16)
        o_ref[p, quarter:half, :] = _dot(flip_ref[p], mirrored).astype(BF16)


def _fourier_call(uf, chan, time, flip, weights):
    def slab(w):
        return pl.BlockSpec((w.shape[0] // BATCH, w.shape[1]), lambda n: (n, 0))

    assert tuple(w.shape[0] for w in weights) == TAIL_WEIGHT_ROWS
    return pl.pallas_call(
        _fourier_kernel,
        grid=(BATCH,),
        in_specs=[
            pl.BlockSpec((SEQ, FOURIER_WIDTH), lambda n: (0, n)),
            pl.BlockSpec((FOURIER_GROUP_DIM, 2 * FOURIER_GROUP_DIM), lambda n: (0, 0)),
            pl.BlockSpec((2, 2, DFT_ROWS, SEQ // 2), lambda n: (0, 0, 0, 0)),
            pl.BlockSpec((2, SEQ // 4, DFT_ROWS), lambda n: (0, 0, 0)),
        ] + [slab(w) for w in weights],
        out_specs=[pl.BlockSpec((2, SEQ // 2, FOURIER_WIDTH), lambda n: (0, 0, n))]
        + [slab(w) for w in weights],
        out_shape=[jax.ShapeDtypeStruct((2, SEQ // 2, BATCH * FOURIER_WIDTH), BF16)]
        + [jax.ShapeDtypeStruct(w.shape, BF16) for w in weights],
        scratch_shapes=[pltpu.VMEM((2, SEQ, FOURIER_WIDTH), BF16)],
        name="fourier",
        compiler_params=pltpu.CompilerParams(
            dimension_semantics=("arbitrary",), vmem_limit_bytes=VMEM_LIMIT),
    )(uf, chan, time, flip, *weights)


def _tail_kernel(ls_ref, pf_ref, pb_ref, hf_ref, hb_ref, uy_ref, ug_ref, yf_ref, xt_ref,
                 gt1_ref, sh2_ref, sc2_ref, gt2_ref, gm_ref, gf_ref,
                 wl_ref, wf_ref, wo_ref, w1_ref, w2_ref, o_hbm, obuf_ref, osem):
    def tb(v):
        return v.reshape(TT, BATCH, v.shape[-1])

    def flat(v):
        return v.reshape(ROWS, v.shape[-1])

    yf = yf_ref[...].astype(F32).reshape(2, TT // 2, BATCH, FOURIER_WIDTH)
    yf = jnp.swapaxes(yf, 0, 1).reshape(ROWS, FOURIER_WIDTH).astype(BF16)
    y_b = _dot(yf, wf_ref[...])

    hf = jnp.tile(hf_ref[...].astype(BF16), (TT, 1))
    hb = jnp.tile(hb_ref[...].astype(BF16), (TT, 1))
    y_lru = ls_ref[...] + pf_ref[...] * hf + pb_ref[...] * hb
    z = y_lru * _gelu_tanh(uy_ref[...])
    y_a = _dot(z, wl_ref[...])

    g_a = _sigmoid(ug_ref[:, :D_MODEL].astype(F32))
    g_b = _sigmoid(ug_ref[:, D_MODEL:].astype(F32))
    merged = (g_a * y_a + g_b * y_b).astype(BF16)
    x2 = tb(xt_ref[...]) + gt1_ref[...][None] * tb(_dot(merged, wo_ref[...]))

    h = (tb(_rms(flat(x2), gm_ref[...])) * (1.0 + sc2_ref[...][None]) + sh2_ref[...][None])
    h = flat(h).astype(BF16)
    acc = jnp.zeros((ROWS, D_MODEL), F32)
    for k0 in range(0, D_FF, D_MODEL):
        a = jnp.maximum(_dot(h, w1_ref[:, k0:k0 + D_MODEL]), 0.0)
        acc = acc + _dot((a * a).astype(BF16), w2_ref[k0:k0 + D_MODEL, :])
    out = _rms(flat(x2 + gt2_ref[...][None] * tb(acc)), gf_ref[...])

    c = pl.program_id(0)
    slot = c % 2

    def copies(step, slot_):
        return [pltpu.make_async_copy(obuf_ref.at[slot_, :, b, :],
                                      o_hbm.at[b, pl.ds(step * TT, TT), :], osem.at[slot_, b])
                for b in range(BATCH)]

    @pl.when(c >= 2)
    def _():
        for cp in copies(c - 2, slot):
            cp.wait()

    obuf_ref[slot] = tb(out)
    for cp in copies(c, slot):
        cp.start()

    @pl.when(c == N_LAT_CHUNKS - 1)
    def _():
        for cp in copies(c - 1, 1 - slot) + copies(c, slot):
            cp.wait()


def _tail_call(ls, pf, pb, hf, hb, uy, ug, yf, xt, mod, gm, gf, wl, wf, wo, w1, w2):
    def rows(width):
        return pl.BlockSpec((ROWS, width), lambda c: (c, 0))

    def const(shape):
        return pl.BlockSpec(shape, lambda c: (0,) * len(shape))

    state = pl.BlockSpec((None, BATCH, LRU_WIDTH), lambda c: (c, 0, 0))
    vec1 = const((1, D_MODEL))
    return pl.pallas_call(
        _tail_kernel,
        grid=(N_LAT_CHUNKS,),
        in_specs=[
            rows(LRU_WIDTH), rows(LRU_WIDTH), rows(LRU_WIDTH), state, state,
            rows(LRU_WIDTH), rows(2 * D_MODEL),
            pl.BlockSpec((2, TT // 2, BATCH * FOURIER_WIDTH), lambda c: (0, c, 0)),
            rows(D_MODEL),
            _mod_spec(0, 2), _mod_spec(0, 3), _mod_spec(0, 4), _mod_spec(0, 5), vec1, vec1,
            const((LRU_WIDTH, D_MODEL)), const((FOURIER_WIDTH, D_MODEL)), const((D_MODEL, D_MODEL)),
            const((D_MODEL, D_FF)), const((D_FF, D_MODEL)),
        ],
        out_specs=pl.BlockSpec(memory_space=pl.ANY),
        out_shape=jax.ShapeDtypeStruct((BATCH, SEQ, D_MODEL), F32),
        scratch_shapes=[pltpu.VMEM((2, TT, BATCH, D_MODEL), F32), pltpu.SemaphoreType.DMA((2, BATCH))],
        name="tail",
        compiler_params=pltpu.CompilerParams(
            dimension_semantics=("arbitrary",), vmem_limit_bytes=VMEM_LIMIT),
    )(ls, pf, pb, hf, hb, uy, ug, yf, xt, mod, mod, mod, mod, gm, gf, wl, wf, wo, w1, w2)


def kernel(x, c, ctx, c_ctx, w_mod, b_mod, g_mix, w_in, conv_w, conv_b, w_a, b_a, w_x, b_x,
           lam, w_lru_out, w_f_out, w_out, g_mlp, w1, w2, g_final):
    pos_np, chan_np, time_np, flip_np = _constants()
    pos = jnp.asarray(pos_np)
    chan, time, flip = (jnp.asarray(v).astype(BF16) for v in (chan_np, time_np, flip_np))

    cc = jnp.concatenate(
        [c, c_ctx[None], jnp.zeros((2 * SUBLANES - BATCH - 1, D_MODEL), F32)], axis=0)
    mod = _mod_call(cc, w_mod[0], b_mod[0][None])

    g_mix2 = g_mix[0][None]
    wg = jnp.concatenate([w_a[0], w_x[0]], axis=-1).astype(BF16)
    lru = (conv_w[0], conv_b[0][None], wg, b_a[0], b_x[0], lam[0])

    w_in_b, e_ctx = _head_call(ctx, None, mod, g_mix2, w_in[0], *lru, lat=False)
    xt, uy, uf, ug, ls, pf, pb, e_lat = _head_call(x, pos, mod, g_mix2, w_in_b, *lru, lat=True)
    hf, hb = _carry_call(e_ctx, e_lat)
    yf, wl, wf, wo, w1b, w2b = _fourier_call(
        uf, chan, time, flip, (w_lru_out[0], w_f_out[0], w_out[0], w1[0], w2[0]))
    return _tail_call(ls, pf, pb, hf, hb, uy, ug, yf, xt, mod,
                      g_mlp[0][None], g_final[None], wl, wf, wo, w1b, w2b)
```

```python
import functools
import math

import numpy as np
import jax
import jax.numpy as jnp
from jax import lax
from jax.experimental import pallas as pl
from jax.experimental.pallas import tpu as pltpu

F32 = jnp.float32
BF16 = jnp.bfloat16

D_MODEL = 1024
BATCH = 8
SEQ = 2048
CTX_LEN = 256
GRID_W = 64
LRU_WIDTH = 1024
LRU_HEADS = 8
LRU_HEAD_DIM = LRU_WIDTH // LRU_HEADS
LRU_C = 8.0
CONV_WIDTH = 4
CONV_PAD_LEFT = 2
FOURIER_WIDTH = 512
FOURIER_GROUPS = 4
FOURIER_GROUP_DIM = FOURIER_WIDTH // FOURIER_GROUPS
IN_COLS = 2 * LRU_WIDTH + FOURIER_WIDTH + 2 * D_MODEL
D_FF = 4 * D_MODEL
N_MOD = 6
EPS = 1e-6
POS_MAX_PERIOD = 10000.0

SUBLANES = 8
BF16_ROWS = 16
assert BATCH == SUBLANES

TT = 64
ROWS = TT * BATCH
HALO_T = SUBLANES
LEAD = CONV_PAD_LEFT * BATCH
TRAIL = (CONV_WIDTH - 1 - CONV_PAD_LEFT) * BATCH
XROWS = LEAD + ROWS + TRAIL
XROWS_PAD = -(-XROWS // BF16_ROWS) * BF16_ROWS
N_CTX_CHUNKS = CTX_LEN // TT
N_LAT_CHUNKS = SEQ // TT
NCHUNK = 512
FRONT_ROWS = 128
GATE_AHEAD = 1
GATE_SLOTS = GATE_AHEAD + 1
DFT_ROWS = -(-(SEQ // 4 + 1) // BF16_ROWS) * BF16_ROWS
MLP_T = 2 * TT
TAIL_WEIGHT_ROWS = (LRU_WIDTH, FOURIER_WIDTH, D_MODEL, D_MODEL, D_FF)
MOD_TN = 1024
V7X_VMEM_BYTES = 64 * 1024 * 1024
VMEM_RESERVE = 6 * 1024 * 1024
VMEM_LIMIT = V7X_VMEM_BYTES - VMEM_RESERVE
LOG2E = math.log2(math.e)
TINY = 1e-30


def _gelu_tanh(x):
    k0 = jnp.full((1, 1), math.sqrt(2.0 / math.pi), F32).astype(x.dtype)
    k1 = jnp.full((1, 1), 0.044715, F32).astype(x.dtype)
    return 0.5 * x * (1.0 + jnp.tanh(k0 * (x + k1 * (x * x * x))))


def _sigmoid(x):
    return 0.5 * jnp.tanh(0.5 * x) + 0.5


def _dot(a, b):
    return jnp.dot(a, b, preferred_element_type=F32)


def _rms(x, g):
    ms = jnp.mean(x * x, axis=-1, keepdims=True)
    return x * lax.rsqrt(ms + EPS) * g


@functools.cache
def _constants():
    half = D_MODEL // 4
    freqs = np.exp(-math.log(POS_MAX_PERIOD) * np.arange(half, dtype=np.float32) / half).astype(np.float32)

    def sincos(n):
        ang = np.arange(n, dtype=np.float32)[:, None] * freqs[None, :]
        return np.concatenate([np.sin(ang), np.cos(ang)], axis=-1).astype(np.float32)

    rows = SEQ // GRID_W
    er, ec = sincos(rows), sincos(GRID_W)
    pos = np.concatenate([
        np.broadcast_to(er[:, None, :], (rows, GRID_W, D_MODEL // 2)),
        np.broadcast_to(ec[None, :, :], (rows, GRID_W, D_MODEL // 2)),
    ], axis=-1).reshape(SEQ, D_MODEL).astype(np.float32)

    def dft(n):
        k = np.arange(n, dtype=np.int64)
        ang = 2.0 * np.pi * ((k[:, None] * k[None, :]) % n).astype(np.float64) / n
        return np.cos(ang) / math.sqrt(n), np.sin(ang) / math.sqrt(n)

    cc, sc = dft(FOURIER_GROUP_DIM)
    chan = np.concatenate([cc, sc], axis=1).astype(np.float32)
    ct, st = dft(SEQ)
    half, quarter = SEQ // 2, SEQ // 4
    time = np.zeros((2, 2, DFT_ROWS, half), np.float64)
    flip = np.zeros((2, quarter, DFT_ROWS), np.float32)
    for p in range(2):
        n_rows = quarter + 1 - p
        rows = 2 * np.arange(n_rows) + p
        time[p, 0, :n_rows] = ct[rows, :half]
        time[p, 1, :n_rows] = st[rows, :half]
        flip[p, np.arange(quarter), quarter - p - np.arange(quarter)] = 1.0
    return pos, chan, time.astype(np.float32), flip


def _mod_kernel(c_ref, w_ref, b_ref, o_ref):
    c = c_ref[...]
    a = c * _sigmoid(c)
    w = w_ref[...]
    a_hi = a.astype(BF16)
    a_lo = (a - a_hi.astype(F32)).astype(BF16)
    w_hi = w.astype(BF16)
    w_lo = (w - w_hi.astype(F32)).astype(BF16)
    n = a.shape[0]
    both = _dot(jnp.concatenate([a_hi, a_lo], axis=0), w_hi)
    o_ref[...] = both[:n] + _dot(a_hi, w_lo) + both[n:] + b_ref[...]


def _mod_call(cc, w_mod, b_mod):
    n = w_mod.shape[1]
    return pl.pallas_call(
        _mod_kernel,
        grid=(n // MOD_TN,),
        in_specs=[
            pl.BlockSpec((2 * SUBLANES, D_MODEL), lambda i: (0, 0)),
            pl.BlockSpec((D_MODEL, MOD_TN), lambda i: (0, i)),
            pl.BlockSpec((1, MOD_TN), lambda i: (0, i)),
        ],
        out_specs=pl.BlockSpec((2 * SUBLANES, MOD_TN), lambda i: (0, i)),
        out_shape=jax.ShapeDtypeStruct((2 * SUBLANES, n), F32),
        name="mod",
        compiler_params=pltpu.CompilerParams(dimension_semantics=("arbitrary",)),
    )(cc, w_mod, b_mod)


def _mod_spec(row_block, k):
    return pl.BlockSpec((BATCH, D_MODEL), lambda c: (row_block, k))


def _head_kernel(*refs, lat, n_chunks):
    refs = list(refs)
    xm_ref, xp_ref, xn_ref = refs[:3]
    refs = refs[3:]
    if lat:
        pm_ref, pp_ref, pn_ref = refs[:3]
        refs = refs[3:]
    (sh_ref, sc_ref, g_ref, w_ref, cw_ref, cb_ref, wg_ref, ba_ref, bx_ref, lam_ref) = refs[:10]
    refs = refs[10:]
    if lat:
        xt_out, uy_out, uf_out, ug_out, ls_out, pf_out, pb_out = refs[:7]
        refs = refs[7:]
    else:
        w_slab_ref, w_bf16_out = refs[:2]
        refs = refs[2:]
        w_bf16_out[...] = w_slab_ref[...].astype(BF16)
    e_out, xs_ref, hb_ref, a_ref, b_ref, xh_ref, gate_ref = refs

    c = pl.program_id(0)
    mod_rows = slice(None) if lat else slice(0, 1)

    xm = xm_ref[...]
    xp = xp_ref[...]
    xn = xn_ref[...]
    if lat:
        xm = xm + pm_ref[...][None]
        xp = xp + pp_ref[...][None]
        xn = xn + pn_ref[...][None]
    xs_ref[LEAD:LEAD + ROWS, :] = jnp.swapaxes(xm, 0, 1).reshape(ROWS, D_MODEL)
    xs_ref[0:LEAD, :] = jnp.swapaxes(xp, 0, 1)[HALO_T - CONV_PAD_LEFT:].reshape(LEAD, D_MODEL)
    xs_ref[LEAD + ROWS:XROWS, :] = (
        jnp.swapaxes(xn, 0, 1)[:CONV_WIDTH - 1 - CONV_PAD_LEFT].reshape(TRAIL, D_MODEL))
    xs_ref[XROWS:XROWS_PAD, :] = jnp.zeros((XROWS_PAD - XROWS, D_MODEL), F32)

    assert LEAD % BF16_ROWS == 0 and FRONT_ROWS % BF16_ROWS == 0
    blocks = ([(0, LEAD)] + [(r, r + FRONT_ROWS) for r in range(LEAD, LEAD + ROWS, FRONT_ROWS)]
              + [(LEAD + ROWS, XROWS_PAD)])
    for r0, r1 in blocks:
        xt = xs_ref[r0:r1, :]
        if lat and LEAD <= r0 < LEAD + ROWS:
            xt_out[r0 - LEAD:r1 - LEAD, :] = xt
        h = _rms(xt, g_ref[...]).reshape((r1 - r0) // BATCH, BATCH, D_MODEL)
        h = h * (1.0 + sc_ref[mod_rows, :][None]) + sh_ref[mod_rows, :][None]
        hb_ref[r0:r1, :] = h.reshape(r1 - r0, D_MODEL).astype(BF16)

    keep_first = jnp.where(c == 0, 0.0, 1.0)
    keep_last = jnp.where(c == n_chunks - 1, 0.0, 1.0)
    for n0 in range(0, LRU_WIDTH, NCHUNK):
        w_x = w_ref[:, n0:n0 + NCHUNK]
        xs_ref[:, n0:n0 + NCHUNK] = _dot(hb_ref[...], w_x if lat else w_x.astype(BF16))
    xs_ref[0:LEAD, :] = xs_ref[0:LEAD, :] * keep_first
    xs_ref[LEAD + ROWS:XROWS, :] = xs_ref[LEAD + ROWS:XROWS, :] * keep_last

    def branch_piece(n0):
        r = _dot(hb_ref[LEAD:LEAD + ROWS, :], w_ref[:, n0:n0 + NCHUNK])
        if n0 < 2 * LRU_WIDTH:
            uy_out[:, n0 - LRU_WIDTH:n0 - LRU_WIDTH + NCHUNK] = r.astype(BF16)
        elif n0 < 2 * LRU_WIDTH + FOURIER_WIDTH:
            assert NCHUNK == FOURIER_WIDTH
            uf_out[...] = r.reshape(TT, BATCH * NCHUNK).astype(BF16)
        else:
            c0 = n0 - 2 * LRU_WIDTH - FOURIER_WIDTH
            ug_out[:, c0:c0 + NCHUNK] = r.astype(BF16)

    pieces = list(range(LRU_WIDTH, IN_COLS, NCHUNK)) if lat else []
    per_head = -(-len(pieces) // LRU_HEADS)

    lam = lam_ref[...]
    sp = jnp.maximum(-lam, 0.0) + jnp.log1p(jnp.exp(-jnp.abs(lam)))
    c1 = (-0.5 * LRU_C * LOG2E) * sp

    def conv_gates(hd):
        sl = slice(hd * LRU_HEAD_DIM, (hd + 1) * LRU_HEAD_DIM)
        xh = 0.5 * cb_ref[:, sl]
        for k in range(CONV_WIDTH):
            xh = xh + (0.5 * cw_ref[k:k + 1, sl]) * xs_ref[k * BATCH:k * BATCH + ROWS, sl]
        xh_ref[hd % GATE_SLOTS] = xh
        xhb = xh.astype(BF16)
        for d in range(2):
            gate_ref[hd % GATE_SLOTS, d] = _dot(xhb, wg_ref[d, hd])

    for hd in range(GATE_AHEAD):
        conv_gates(hd)
    def coefficients(hd):
        sl = slice(hd * LRU_HEAD_DIM, (hd + 1) * LRU_HEAD_DIM)
        xh = xh_ref[hd % GATE_SLOTS]
        for d in range(2):
            g = gate_ref[hd % GATE_SLOTS, d]
            ta = jnp.tanh(g[:, :LRU_HEAD_DIM] + 0.5 * ba_ref[d:d + 1, sl])
            ti = jnp.tanh(g[:, LRU_HEAD_DIM:] + 0.5 * bx_ref[d:d + 1, sl])
            a = jnp.exp2(c1[d:d + 1, sl] * ta + c1[d:d + 1, sl])
            v = 1.0 - a * a
            m = v * lax.rsqrt(jnp.maximum(v, TINY))
            a_ref[d, :, sl] = a
            b_ref[d, :, sl] = (m * xh) * (ti + 1.0)

    def scans(hd):
        sl = slice(hd * LRU_HEAD_DIM, (hd + 1) * LRU_HEAD_DIM)
        lf = pf = lb = pb = None
        for t in range(TT):
            rf = slice(t * BATCH, (t + 1) * BATCH)
            rb = slice((TT - 1 - t) * BATCH, (TT - t) * BATCH)
            af, bf = a_ref[0, rf, sl], b_ref[0, rf, sl]
            ab, bb = a_ref[1, rb, sl], b_ref[1, rb, sl]
            if t == 0:
                lf, pf, lb, pb = bf, af, bb, ab
            else:
                lf, pf = af * lf + bf, af * pf
                lb, pb = ab * lb + bb, ab * pb
                b_ref[0, rf, sl] = lf
                a_ref[0, rf, sl] = pf
                b_ref[1, rb, sl] = lb
                a_ref[1, rb, sl] = pb
        e_out[0, :, sl] = lf
        e_out[1, :, sl] = pf
        e_out[2, :, sl] = lb
        e_out[3, :, sl] = pb
        if lat:
            ls_out[:, sl] = (b_ref[0, :, sl] + b_ref[1, :, sl]).astype(BF16)
            pf_out[:, sl] = a_ref[0, :, sl].astype(BF16)
            pb_out[:, sl] = a_ref[1, :, sl].astype(BF16)

    for hd in range(LRU_HEADS):
        for n0 in pieces[hd * per_head:(hd + 1) * per_head]:
            branch_piece(n0)
        coefficients(hd)
        if hd + GATE_AHEAD < LRU_HEADS:
            conv_gates(hd + GATE_AHEAD)
        scans(hd)


def _head_call(x, pos, mod, g, w, conv_w, conv_b, wg, b_a, b_x, lam, *, lat):
    seq = x.shape[1]
    n_chunks = seq // TT
    n_halo = seq // HALO_T
    per_chunk = TT // HALO_T

    def const(shape):
        return pl.BlockSpec(shape, lambda c: (0,) * len(shape))

    def prev_idx(c):
        return jnp.maximum(c * per_chunk - 1, 0)

    def next_idx(c):
        return jnp.minimum((c + 1) * per_chunk, n_halo - 1)

    in_specs = [
        pl.BlockSpec((BATCH, TT, D_MODEL), lambda c: (0, c, 0)),
        pl.BlockSpec((BATCH, HALO_T, D_MODEL), lambda c: (0, prev_idx(c), 0)),
        pl.BlockSpec((BATCH, HALO_T, D_MODEL), lambda c: (0, next_idx(c), 0)),
    ]
    args = [x, x, x]
    if lat:
        in_specs += [
            pl.BlockSpec((TT, D_MODEL), lambda c: (c, 0)),
            pl.BlockSpec((HALO_T, D_MODEL), lambda c: (prev_idx(c), 0)),
            pl.BlockSpec((HALO_T, D_MODEL), lambda c: (next_idx(c), 0)),
        ]
        args += [pos, pos, pos]
    in_specs += [
        _mod_spec(0 if lat else 1, 0), _mod_spec(0 if lat else 1, 1), const((1, D_MODEL)),
        const((D_MODEL, IN_COLS if lat else LRU_WIDTH)),
        const((CONV_WIDTH, LRU_WIDTH)), const((1, LRU_WIDTH)),
        const((2, LRU_HEADS, LRU_HEAD_DIM, 2 * LRU_HEAD_DIM)),
        const((2, LRU_WIDTH)), const((2, LRU_WIDTH)), const((2, LRU_WIDTH)),
    ]
    args += [mod, mod, g, w, conv_w, conv_b, wg, b_a, b_x, lam]
    w_slab = pl.BlockSpec((D_MODEL // n_chunks, IN_COLS), lambda c: (c, 0))
    if not lat:
        in_specs.append(w_slab)
        args.append(w)

    def rows(width):
        return pl.BlockSpec((ROWS, width), lambda c: (c, 0))

    out_specs, out_shape = [], []
    if lat:
        n = n_chunks * ROWS
        out_specs += [rows(D_MODEL), rows(LRU_WIDTH),
                      pl.BlockSpec((TT, BATCH * FOURIER_WIDTH), lambda c: (c, 0)),
                      rows(2 * D_MODEL), rows(LRU_WIDTH), rows(LRU_WIDTH), rows(LRU_WIDTH)]
        out_shape += [
            jax.ShapeDtypeStruct((n, D_MODEL), F32),
            jax.ShapeDtypeStruct((n, LRU_WIDTH), BF16),
            jax.ShapeDtypeStruct((seq, BATCH * FOURIER_WIDTH), BF16),
            jax.ShapeDtypeStruct((n, 2 * D_MODEL), BF16),
            jax.ShapeDtypeStruct((n, LRU_WIDTH), BF16),
            jax.ShapeDtypeStruct((n, LRU_WIDTH), BF16),
            jax.ShapeDtypeStruct((n, LRU_WIDTH), BF16),
        ]
    else:
        out_specs.append(w_slab)
        out_shape.append(jax.ShapeDtypeStruct((D_MODEL, IN_COLS), BF16))
    out_specs.append(pl.BlockSpec((None, 4, BATCH, LRU_WIDTH), lambda c: (c, 0, 0, 0)))
    out_shape.append(jax.ShapeDtypeStruct((n_chunks, 4, BATCH, LRU_WIDTH), F32))

    return pl.pallas_call(
        functools.partial(_head_kernel, lat=lat, n_chunks=n_chunks),
        grid=(n_chunks,),
        in_specs=in_specs,
        out_specs=out_specs,
        out_shape=out_shape,
        scratch_shapes=[
            pltpu.VMEM((XROWS_PAD, D_MODEL), F32),
            pltpu.VMEM((XROWS_PAD, D_MODEL), BF16),
            pltpu.VMEM((2, ROWS, LRU_WIDTH), F32),
            pltpu.VMEM((2, ROWS, LRU_WIDTH), F32),
            pltpu.VMEM((GATE_SLOTS, ROWS, LRU_HEAD_DIM), F32),
            pltpu.VMEM((GATE_SLOTS, 2, ROWS, 2 * LRU_HEAD_DIM), F32),
        ],
        name="head_lat" if lat else "head_ctx",
        compiler_params=pltpu.CompilerParams(
            dimension_semantics=("arbitrary",), vmem_limit_bytes=VMEM_LIMIT),
    )(*args)


def _carry_kernel(ec_ref, el_ref, hf_ref, hb_ref):
    n_ctx, n_lat = ec_ref.shape[0], el_ref.shape[0]
    h = jnp.zeros((BATCH, LRU_WIDTH), F32)
    for c in range(n_ctx):
        h = ec_ref[c, 1] * h + ec_ref[c, 0]
    for c in range(n_lat):
        hf_ref[c] = h
        h = el_ref[c, 1] * h + el_ref[c, 0]
    h = jnp.zeros((BATCH, LRU_WIDTH), F32)
    for c in reversed(range(n_ctx)):
        h = ec_ref[c, 3] * h + ec_ref[c, 2]
    for c in reversed(range(n_lat)):
        hb_ref[c] = h
        h = el_ref[c, 3] * h + el_ref[c, 2]


def _carry_call(e_ctx, e_lat):
    shape = jax.ShapeDtypeStruct((e_lat.shape[0], BATCH, LRU_WIDTH), F32)
    return pl.pallas_call(_carry_kernel, out_shape=[shape, shape], name="carry")(e_ctx, e_lat)


def _fourier_kernel(x_ref, chan_ref, time_ref, flip_ref, *refs):
    n_w = len(TAIL_WEIGHT_ROWS)
    w_f32_refs, o_ref, w_bf16_refs, z_ref = refs[:n_w], refs[n_w], refs[n_w + 1:2 * n_w + 1], refs[-1]
    for src, dst in zip(w_f32_refs, w_bf16_refs):
        dst[...] = src[...].astype(BF16)

    half, quarter = SEQ // 2, SEQ // 4
    for g in range(FOURIER_GROUPS):
        sl = slice(g * FOURIER_GROUP_DIM, (g + 1) * FOURIER_GROUP_DIM)
        r = _dot(x_ref[:, sl], chan_ref[...])
        lo, hi = r[:half], r[half:]
        for p, v in enumerate((lo + hi, lo - hi)):
            z_ref[p, 0:half, sl] = v[:, :FOURIER_GROUP_DIM].astype(BF16)
            z_ref[p, half:SEQ, sl] = v[:, FOURIER_GROUP_DIM:].astype(BF16)
    for p in range(2):
        cos_part = _dot(time_ref[p, 0], z_ref[p, 0:half, :])
        sin_part = _dot(time_ref[p, 1], z_ref[p, half:SEQ, :])
        o_ref[p, 0:quarter, :] = (cos_part - sin_part)[0:quarter].astype(BF16)
        mirrored = (cos_part + sin_part).astype(BF16)
        o_ref[p, quarter:half, :] = _dot(flip_ref[p], mirrored).astype(BF16)


def _fourier_call(uf, chan, time, flip, weights):
    def slab(w):
        return pl.BlockSpec((w.shape[0] // BATCH, w.shape[1]), lambda n: (n, 0))

    assert tuple(w.shape[0] for w in weights) == TAIL_WEIGHT_ROWS
    return pl.pallas_call(
        _fourier_kernel,
        grid=(BATCH,),
        in_specs=[
            pl.BlockSpec((SEQ, FOURIER_WIDTH), lambda n: (0, n)),
            pl.BlockSpec((FOURIER_GROUP_DIM, 2 * FOURIER_GROUP_DIM), lambda n: (0, 0)),
            pl.BlockSpec((2, 2, DFT_ROWS, SEQ // 2), lambda n: (0, 0, 0, 0)),
            pl.BlockSpec((2, SEQ // 4, DFT_ROWS), lambda n: (0, 0, 0)),
        ] + [slab(w) for w in weights],
        out_specs=[pl.BlockSpec((2, SEQ // 2, FOURIER_WIDTH), lambda n: (0, 0, n))]
        + [slab(w) for w in weights],
        out_shape=[jax.ShapeDtypeStruct((2, SEQ // 2, BATCH * FOURIER_WIDTH), BF16)]
        + [jax.ShapeDtypeStruct(w.shape, BF16) for w in weights],
        scratch_shapes=[pltpu.VMEM((2, SEQ, FOURIER_WIDTH), BF16)],
        name="fourier",
        compiler_params=pltpu.CompilerParams(
            dimension_semantics=("arbitrary",), vmem_limit_bytes=VMEM_LIMIT),
    )(uf, chan, time, flip, *weights)


def _merge_kernel(ls_ref, pf_ref, pb_ref, hf_ref, hb_ref, uy_ref, ug_ref, yf_ref, xt_ref,
                  gt1_ref, wl_ref, wf_ref, wo_ref, o_ref):
    def tb(v):
        return v.reshape(TT, BATCH, v.shape[-1])

    yf = yf_ref[...].astype(F32).reshape(2, TT // 2, BATCH, FOURIER_WIDTH)
    yf = jnp.swapaxes(yf, 0, 1).reshape(ROWS, FOURIER_WIDTH).astype(BF16)
    y_b = _dot(yf, wf_ref[...])

    hf = jnp.tile(hf_ref[...].astype(BF16), (TT, 1))
    hb = jnp.tile(hb_ref[...].astype(BF16), (TT, 1))
    y_lru = ls_ref[...] + pf_ref[...] * hf + pb_ref[...] * hb
    z = y_lru * _gelu_tanh(uy_ref[...])
    y_a = _dot(z, wl_ref[...])

    g_a = _sigmoid(ug_ref[:, :D_MODEL].astype(F32))
    g_b = _sigmoid(ug_ref[:, D_MODEL:].astype(F32))
    merged = (g_a * y_a + g_b * y_b).astype(BF16)
    x2 = tb(xt_ref[...]) + gt1_ref[...][None] * tb(_dot(merged, wo_ref[...]))
    o_ref[...] = x2.reshape(ROWS, D_MODEL)


def _merge_call(ls, pf, pb, hf, hb, uy, ug, yf, xt, mod, wl, wf, wo):
    def rows(width):
        return pl.BlockSpec((ROWS, width), lambda c: (c, 0))

    def const(shape):
        return pl.BlockSpec(shape, lambda c: (0,) * len(shape))

    state = pl.BlockSpec((None, BATCH, LRU_WIDTH), lambda c: (c, 0, 0))
    return pl.pallas_call(
        _merge_kernel,
        grid=(N_LAT_CHUNKS,),
        in_specs=[
            rows(LRU_WIDTH), rows(LRU_WIDTH), rows(LRU_WIDTH), state, state,
            rows(LRU_WIDTH), rows(2 * D_MODEL),
            pl.BlockSpec((2, TT // 2, BATCH * FOURIER_WIDTH), lambda c: (0, c, 0)),
            rows(D_MODEL), _mod_spec(0, 2),
            const((LRU_WIDTH, D_MODEL)), const((FOURIER_WIDTH, D_MODEL)), const((D_MODEL, D_MODEL)),
        ],
        out_specs=rows(D_MODEL),
        out_shape=jax.ShapeDtypeStruct((N_LAT_CHUNKS * ROWS, D_MODEL), F32),
        name="merge",
        compiler_params=pltpu.CompilerParams(
            dimension_semantics=("arbitrary",), vmem_limit_bytes=VMEM_LIMIT),
    )(ls, pf, pb, hf, hb, uy, ug, yf, xt, mod, wl, wf, wo)


def _mlp_kernel(x2_ref, sh2_ref, sc2_ref, gt2_ref, gm_ref, gf_ref, w1_ref, w2_ref,
                o_hbm, obuf_ref, osem):
    def tb(v):
        return v.reshape(MLP_T, BATCH, v.shape[-1])

    def flat(v):
        return v.reshape(MLP_T * BATCH, v.shape[-1])

    x2 = tb(x2_ref[...])
    h = (tb(_rms(flat(x2), gm_ref[...])) * (1.0 + sc2_ref[...][None]) + sh2_ref[...][None])
    h = flat(h).astype(BF16)
    acc = jnp.zeros((MLP_T * BATCH, D_MODEL), F32)
    for k0 in range(0, D_FF, D_MODEL):
        a = jnp.maximum(_dot(h, w1_ref[:, k0:k0 + D_MODEL]), 0.0)
        acc = acc + _dot((a * a).astype(BF16), w2_ref[k0:k0 + D_MODEL, :])
    out = _rms(flat(x2 + gt2_ref[...][None] * tb(acc)), gf_ref[...])

    c = pl.program_id(0)
    slot = c % 2

    def copies(step, slot_):
        return [pltpu.make_async_copy(obuf_ref.at[slot_, :, b, :],
                                      o_hbm.at[b, pl.ds(step * MLP_T, MLP_T), :], osem.at[slot_, b])
                for b in range(BATCH)]

    @pl.when(c >= 2)
    def _():
        for cp in copies(c - 2, slot):
            cp.wait()

    obuf_ref[slot] = tb(out)
    for cp in copies(c, slot):
        cp.start()

    @pl.when(c == SEQ // MLP_T - 1)
    def _():
        for cp in copies(c - 1, 1 - slot) + copies(c, slot):
            cp.wait()


def _mlp_call(x2, mod, gm, gf, w1, w2):
    def const(shape):
        return pl.BlockSpec(shape, lambda c: (0,) * len(shape))

    vec1 = const((1, D_MODEL))
    return pl.pallas_call(
        _mlp_kernel,
        grid=(SEQ // MLP_T,),
        in_specs=[
            pl.BlockSpec((MLP_T * BATCH, D_MODEL), lambda c: (c, 0)),
            _mod_spec(0, 3), _mod_spec(0, 4), _mod_spec(0, 5), vec1, vec1,
            const((D_MODEL, D_FF)), const((D_FF, D_MODEL)),
        ],
        out_specs=pl.BlockSpec(memory_space=pl.ANY),
        out_shape=jax.ShapeDtypeStruct((BATCH, SEQ, D_MODEL), F32),
        scratch_shapes=[pltpu.VMEM((2, MLP_T, BATCH, D_MODEL), F32),
                        pltpu.SemaphoreType.DMA((2, BATCH))],
        name="mlp",
        compiler_params=pltpu.CompilerParams(
            dimension_semantics=("arbitrary",), vmem_limit_bytes=VMEM_LIMIT),
    )(x2, mod, mod, mod, gm, gf, w1, w2)


def kernel(x, c, ctx, c_ctx, w_mod, b_mod, g_mix, w_in, conv_w, conv_b, w_a, b_a, w_x, b_x,
           lam, w_lru_out, w_f_out, w_out, g_mlp, w1, w2, g_final):
    pos_np, chan_np, time_np, flip_np = _constants()
    pos = jnp.asarray(pos_np)
    chan, time, flip = (jnp.asarray(v).astype(BF16) for v in (chan_np, time_np, flip_np))

    cc = jnp.concatenate(
        [c, c_ctx[None], jnp.zeros((2 * SUBLANES - BATCH - 1, D_MODEL), F32)], axis=0)
    mod = _mod_call(cc, w_mod[0], b_mod[0][None])

    g_mix2 = g_mix[0][None]
    wg = jnp.concatenate([w_a[0], w_x[0]], axis=-1).astype(BF16)
    lru = (conv_w[0], conv_b[0][None], wg, b_a[0], b_x[0], lam[0])

    w_in_b, e_ctx = _head_call(ctx, None, mod, g_mix2, w_in[0], *lru, lat=False)
    xt, uy, uf, ug, ls, pf, pb, e_lat = _head_call(x, pos, mod, g_mix2, w_in_b, *lru, lat=True)
    hf, hb = _carry_call(e_ctx, e_lat)
    yf, wl, wf, wo, w1b, w2b = _fourier_call(
        uf, chan, time, flip, (w_lru_out[0], w_f_out[0], w_out[0], w1[0], w2[0]))
    x2 = _merge_call(ls, pf, pb, hf, hb, uy, ug, yf, xt, mod, wl, wf, wo)
    return _mlp_call(x2, mod, g_mlp[0][None], g_final[None], w1b, w2b)
```

```python
import functools
import math

import numpy as np
import jax
import jax.numpy as jnp
from jax import lax
from jax.experimental import pallas as pl
from jax.experimental.pallas import tpu as pltpu

F32 = jnp.float32
BF16 = jnp.bfloat16

D_MODEL = 1024
BATCH = 8
SEQ = 2048
CTX_LEN = 256
GRID_W = 64
LRU_WIDTH = 1024
LRU_HEADS = 8
LRU_HEAD_DIM = LRU_WIDTH // LRU_HEADS
LRU_C = 8.0
CONV_WIDTH = 4
CONV_PAD_LEFT = 2
FOURIER_WIDTH = 512
FOURIER_GROUPS = 4
FOURIER_GROUP_DIM = FOURIER_WIDTH // FOURIER_GROUPS
IN_COLS = 2 * LRU_WIDTH + FOURIER_WIDTH + 2 * D_MODEL
D_FF = 4 * D_MODEL
N_MOD = 6
EPS = 1e-6
POS_MAX_PERIOD = 10000.0

SUBLANES = 8
BF16_ROWS = 16
assert BATCH == SUBLANES

TT = 64
ROWS = TT * BATCH
HALO_T = SUBLANES
LEAD = CONV_PAD_LEFT * BATCH
TRAIL = (CONV_WIDTH - 1 - CONV_PAD_LEFT) * BATCH
XROWS = LEAD + ROWS + TRAIL
XROWS_PAD = -(-XROWS // BF16_ROWS) * BF16_ROWS
N_CTX_CHUNKS = CTX_LEN // TT
N_LAT_CHUNKS = SEQ // TT
NCHUNK = 512
FRONT_ROWS = 128
GATE_AHEAD = 1
GATE_SLOTS = GATE_AHEAD + 1
DFT_ROWS = -(-(SEQ // 4 + 1) // BF16_ROWS) * BF16_ROWS
TAIL_WEIGHT_ROWS = (LRU_WIDTH, FOURIER_WIDTH, D_MODEL, D_MODEL, D_FF)
MOD_TN = 1024
V7X_VMEM_BYTES = 64 * 1024 * 1024
VMEM_RESERVE = 6 * 1024 * 1024
VMEM_LIMIT = V7X_VMEM_BYTES - VMEM_RESERVE
LOG2E = math.log2(math.e)
TINY = 1e-30


def _gelu_tanh(x):
    k0 = jnp.full((1, 1), math.sqrt(2.0 / math.pi), F32).astype(x.dtype)
    k1 = jnp.full((1, 1), 0.044715, F32).astype(x.dtype)
    return 0.5 * x * (1.0 + jnp.tanh(k0 * (x + k1 * (x * x * x))))


def _sigmoid(x):
    return 0.5 * jnp.tanh(0.5 * x) + 0.5


def _dot(a, b):
    return jnp.dot(a, b, preferred_element_type=F32)


def _rms(x, g):
    ms = jnp.mean(x * x, axis=-1, keepdims=True)
    return x * lax.rsqrt(ms + EPS) * g


@functools.cache
def _constants():
    half = D_MODEL // 4
    freqs = np.exp(-math.log(POS_MAX_PERIOD) * np.arange(half, dtype=np.float32) / half).astype(np.float32)

    def sincos(n):
        ang = np.arange(n, dtype=np.float32)[:, None] * freqs[None, :]
        return np.concatenate([np.sin(ang), np.cos(ang)], axis=-1).astype(np.float32)

    rows = SEQ // GRID_W
    er, ec = sincos(rows), sincos(GRID_W)
    pos = np.concatenate([
        np.broadcast_to(er[:, None, :], (rows, GRID_W, D_MODEL // 2)),
        np.broadcast_to(ec[None, :, :], (rows, GRID_W, D_MODEL // 2)),
    ], axis=-1).reshape(SEQ, D_MODEL).astype(np.float32)

    def dft(n):
        k = np.arange(n, dtype=np.int64)
        ang = 2.0 * np.pi * ((k[:, None] * k[None, :]) % n).astype(np.float64) / n
        return np.cos(ang) / math.sqrt(n), np.sin(ang) / math.sqrt(n)

    cc, sc = dft(FOURIER_GROUP_DIM)
    chan = np.concatenate([cc, sc], axis=1).astype(np.float32)
    ct, st = dft(SEQ)
    half, quarter = SEQ // 2, SEQ // 4
    time = np.zeros((2, 2, DFT_ROWS, half), np.float64)
    flip = np.zeros((2, quarter, DFT_ROWS), np.float32)
    for p in range(2):
        n_rows = quarter + 1 - p
        rows = 2 * np.arange(n_rows) + p
        time[p, 0, :n_rows] = ct[rows, :half]
        time[p, 1, :n_rows] = st[rows, :half]
        flip[p, np.arange(quarter), quarter - p - np.arange(quarter)] = 1.0
    return pos, chan, time.astype(np.float32), flip


def _mod_kernel(c_ref, w_ref, b_ref, o_ref):
    c = c_ref[...]
    a = c * _sigmoid(c)
    w = w_ref[...]
    a_hi = a.astype(BF16)
    a_lo = (a - a_hi.astype(F32)).astype(BF16)
    w_hi = w.astype(BF16)
    w_lo = (w - w_hi.astype(F32)).astype(BF16)
    n = a.shape[0]
    both = _dot(jnp.concatenate([a_hi, a_lo], axis=0), w_hi)
    o_ref[...] = both[:n] + _dot(a_hi, w_lo) + both[n:] + b_ref[...]


def _mod_call(cc, w_mod, b_mod):
    n = w_mod.shape[1]
    return pl.pallas_call(
        _mod_kernel,
        grid=(n // MOD_TN,),
        in_specs=[
            pl.BlockSpec((2 * SUBLANES, D_MODEL), lambda i: (0, 0)),
            pl.BlockSpec((D_MODEL, MOD_TN), lambda i: (0, i)),
            pl.BlockSpec((1, MOD_TN), lambda i: (0, i)),
        ],
        out_specs=pl.BlockSpec((2 * SUBLANES, MOD_TN), lambda i: (0, i)),
        out_shape=jax.ShapeDtypeStruct((2 * SUBLANES, n), F32),
        name="mod",
        compiler_params=pltpu.CompilerParams(dimension_semantics=("arbitrary",)),
    )(cc, w_mod, b_mod)


def _mod_spec(row_block, k):
    return pl.BlockSpec((BATCH, D_MODEL), lambda c: (row_block, k))


def _head_kernel(*refs, lat, n_chunks):
    refs = list(refs)
    xm_ref, xp_ref, xn_ref = refs[:3]
    refs = refs[3:]
    if lat:
        pm_ref, pp_ref, pn_ref = refs[:3]
        refs = refs[3:]
    (sh_ref, sc_ref, g_ref, w_ref, cw_ref, cb_ref, wg_ref, ba_ref, bx_ref, lam_ref) = refs[:10]
    refs = refs[10:]
    if lat:
        xt_out, uy_out, uf_out, ug_out, ls_out, pf_out, pb_out = refs[:7]
        refs = refs[7:]
    else:
        w_slab_ref, w_bf16_out = refs[:2]
        refs = refs[2:]
        w_bf16_out[...] = w_slab_ref[...].astype(BF16)
    e_out, xs_ref, hb_ref, a_ref, b_ref, xh_ref, gate_ref = refs

    c = pl.program_id(0)
    mod_rows = slice(None) if lat else slice(0, 1)

    xm = xm_ref[...]
    xp = xp_ref[...]
    xn = xn_ref[...]
    if lat:
        xm = xm + pm_ref[...][None]
        xp = xp + pp_ref[...][None]
        xn = xn + pn_ref[...][None]
    xs_ref[LEAD:LEAD + ROWS, :] = jnp.swapaxes(xm, 0, 1).reshape(ROWS, D_MODEL)
    xs_ref[0:LEAD, :] = jnp.swapaxes(xp, 0, 1)[HALO_T - CONV_PAD_LEFT:].reshape(LEAD, D_MODEL)
    xs_ref[LEAD + ROWS:XROWS, :] = (
        jnp.swapaxes(xn, 0, 1)[:CONV_WIDTH - 1 - CONV_PAD_LEFT].reshape(TRAIL, D_MODEL))
    xs_ref[XROWS:XROWS_PAD, :] = jnp.zeros((XROWS_PAD - XROWS, D_MODEL), F32)

    assert LEAD % BF16_ROWS == 0 and FRONT_ROWS % BF16_ROWS == 0
    blocks = ([(0, LEAD)] + [(r, r + FRONT_ROWS) for r in range(LEAD, LEAD + ROWS, FRONT_ROWS)]
              + [(LEAD + ROWS, XROWS_PAD)])
    for r0, r1 in blocks:
        xt = xs_ref[r0:r1, :]
        if lat and LEAD <= r0 < LEAD + ROWS:
            xt_out[r0 - LEAD:r1 - LEAD, :] = xt
        h = _rms(xt, g_ref[...]).reshape((r1 - r0) // BATCH, BATCH, D_MODEL)
        h = h * (1.0 + sc_ref[mod_rows, :][None]) + sh_ref[mod_rows, :][None]
        hb_ref[r0:r1, :] = h.reshape(r1 - r0, D_MODEL).astype(BF16)

    keep_first = jnp.where(c == 0, 0.0, 1.0)
    keep_last = jnp.where(c == n_chunks - 1, 0.0, 1.0)
    for n0 in range(0, LRU_WIDTH, NCHUNK):
        w_x = w_ref[:, n0:n0 + NCHUNK]
        xs_ref[:, n0:n0 + NCHUNK] = _dot(hb_ref[...], w_x if lat else w_x.astype(BF16))
    xs_ref[0:LEAD, :] = xs_ref[0:LEAD, :] * keep_first
    xs_ref[LEAD + ROWS:XROWS, :] = xs_ref[LEAD + ROWS:XROWS, :] * keep_last

    def branch_piece(n0):
        r = _dot(hb_ref[LEAD:LEAD + ROWS, :], w_ref[:, n0:n0 + NCHUNK])
        if n0 < 2 * LRU_WIDTH:
            uy_out[:, n0 - LRU_WIDTH:n0 - LRU_WIDTH + NCHUNK] = r.astype(BF16)
        elif n0 < 2 * LRU_WIDTH + FOURIER_WIDTH:
            assert NCHUNK == FOURIER_WIDTH
            uf_out[...] = r.reshape(TT, BATCH * NCHUNK).astype(BF16)
        else:
            c0 = n0 - 2 * LRU_WIDTH - FOURIER_WIDTH
            ug_out[:, c0:c0 + NCHUNK] = r.astype(BF16)

    pieces = list(range(LRU_WIDTH, IN_COLS, NCHUNK)) if lat else []
    per_head = -(-len(pieces) // LRU_HEADS)

    lam = lam_ref[...]
    sp = jnp.maximum(-lam, 0.0) + jnp.log1p(jnp.exp(-jnp.abs(lam)))
    c1 = (-0.5 * LRU_C * LOG2E) * sp

    def conv_gates(hd):
        sl = slice(hd * LRU_HEAD_DIM, (hd + 1) * LRU_HEAD_DIM)
        xh = 0.5 * cb_ref[:, sl]
        for k in range(CONV_WIDTH):
            xh = xh + (0.5 * cw_ref[k:k + 1, sl]) * xs_ref[k * BATCH:k * BATCH + ROWS, sl]
        xh_ref[hd % GATE_SLOTS] = xh
        xhb = xh.astype(BF16)
        for d in range(2):
            gate_ref[hd % GATE_SLOTS, d] = _dot(xhb, wg_ref[d, hd])

    for hd in range(GATE_AHEAD):
        conv_gates(hd)
    def coefficients(hd):
        sl = slice(hd * LRU_HEAD_DIM, (hd + 1) * LRU_HEAD_DIM)
        xh = xh_ref[hd % GATE_SLOTS]
        for d in range(2):
            g = gate_ref[hd % GATE_SLOTS, d]
            ta = jnp.tanh(g[:, :LRU_HEAD_DIM] + 0.5 * ba_ref[d:d + 1, sl])
            ti = jnp.tanh(g[:, LRU_HEAD_DIM:] + 0.5 * bx_ref[d:d + 1, sl])
            a = jnp.exp2(c1[d:d + 1, sl] * ta + c1[d:d + 1, sl])
            v = 1.0 - a * a
            m = v * lax.rsqrt(jnp.maximum(v, TINY))
            a_ref[d, :, sl] = a
            b_ref[d, :, sl] = (m * xh) * (ti + 1.0)

    def scans(hd):
        sl = slice(hd * LRU_HEAD_DIM, (hd + 1) * LRU_HEAD_DIM)
        lf = pf = lb = pb = None
        for t in range(TT):
            rf = slice(t * BATCH, (t + 1) * BATCH)
            rb = slice((TT - 1 - t) * BATCH, (TT - t) * BATCH)
            af, bf = a_ref[0, rf, sl], b_ref[0, rf, sl]
            ab, bb = a_ref[1, rb, sl], b_ref[1, rb, sl]
            if t == 0:
                lf, pf, lb, pb = bf, af, bb, ab
            else:
                lf, pf = af * lf + bf, af * pf
                lb, pb = ab * lb + bb, ab * pb
                b_ref[0, rf, sl] = lf
                a_ref[0, rf, sl] = pf
                b_ref[1, rb, sl] = lb
                a_ref[1, rb, sl] = pb
        e_out[0, :, sl] = lf
        e_out[1, :, sl] = pf
        e_out[2, :, sl] = lb
        e_out[3, :, sl] = pb
        if lat:
            ls_out[:, sl] = (b_ref[0, :, sl] + b_ref[1, :, sl]).astype(BF16)
            pf_out[:, sl] = a_ref[0, :, sl].astype(BF16)
            pb_out[:, sl] = a_ref[1, :, sl].astype(BF16)

    for hd in range(LRU_HEADS):
        for n0 in pieces[hd * per_head:(hd + 1) * per_head]:
            branch_piece(n0)
        coefficients(hd)
        if hd + GATE_AHEAD < LRU_HEADS:
            conv_gates(hd + GATE_AHEAD)
        scans(hd)


def _head_call(x, pos, mod, g, w, conv_w, conv_b, wg, b_a, b_x, lam, *, lat):
    seq = x.shape[1]
    n_chunks = seq // TT
    n_halo = seq // HALO_T
    per_chunk = TT // HALO_T

    def const(shape):
        return pl.BlockSpec(shape, lambda c: (0,) * len(shape))

    def prev_idx(c):
        return jnp.maximum(c * per_chunk - 1, 0)

    def next_idx(c):
        return jnp.minimum((c + 1) * per_chunk, n_halo - 1)

    in_specs = [
        pl.BlockSpec((BATCH, TT, D_MODEL), lambda c: (0, c, 0)),
        pl.BlockSpec((BATCH, HALO_T, D_MODEL), lambda c: (0, prev_idx(c), 0)),
        pl.BlockSpec((BATCH, HALO_T, D_MODEL), lambda c: (0, next_idx(c), 0)),
    ]
    args = [x, x, x]
    if lat:
        in_specs += [
            pl.BlockSpec((TT, D_MODEL), lambda c: (c, 0)),
            pl.BlockSpec((HALO_T, D_MODEL), lambda c: (prev_idx(c), 0)),
            pl.BlockSpec((HALO_T, D_MODEL), lambda c: (next_idx(c), 0)),
        ]
        args += [pos, pos, pos]
    in_specs += [
        _mod_spec(0 if lat else 1, 0), _mod_spec(0 if lat else 1, 1), const((1, D_MODEL)),
        const((D_MODEL, IN_COLS if lat else LRU_WIDTH)),
        const((CONV_WIDTH, LRU_WIDTH)), const((1, LRU_WIDTH)),
        const((2, LRU_HEADS, LRU_HEAD_DIM, 2 * LRU_HEAD_DIM)),
        const((2, LRU_WIDTH)), const((2, LRU_WIDTH)), const((2, LRU_WIDTH)),
    ]
    args += [mod, mod, g, w, conv_w, conv_b, wg, b_a, b_x, lam]
    w_slab = pl.BlockSpec((D_MODEL // n_chunks, IN_COLS), lambda c: (c, 0))
    if not lat:
        in_specs.append(w_slab)
        args.append(w)

    def rows(width):
        return pl.BlockSpec((ROWS, width), lambda c: (c, 0))

    out_specs, out_shape = [], []
    if lat:
        n = n_chunks * ROWS
        out_specs += [rows(D_MODEL), rows(LRU_WIDTH),
                      pl.BlockSpec((TT, BATCH * FOURIER_WIDTH), lambda c: (c, 0)),
                      rows(2 * D_MODEL), rows(LRU_WIDTH), rows(LRU_WIDTH), rows(LRU_WIDTH)]
        out_shape += [
            jax.ShapeDtypeStruct((n, D_MODEL), F32),
            jax.ShapeDtypeStruct((n, LRU_WIDTH), BF16),
            jax.ShapeDtypeStruct((seq, BATCH * FOURIER_WIDTH), BF16),
            jax.ShapeDtypeStruct((n, 2 * D_MODEL), BF16),
            jax.ShapeDtypeStruct((n, LRU_WIDTH), BF16),
            jax.ShapeDtypeStruct((n, LRU_WIDTH), BF16),
            jax.ShapeDtypeStruct((n, LRU_WIDTH), BF16),
        ]
    else:
        out_specs.append(w_slab)
        out_shape.append(jax.ShapeDtypeStruct((D_MODEL, IN_COLS), BF16))
    out_specs.append(pl.BlockSpec((None, 4, BATCH, LRU_WIDTH), lambda c: (c, 0, 0, 0)))
    out_shape.append(jax.ShapeDtypeStruct((n_chunks, 4, BATCH, LRU_WIDTH), F32))

    return pl.pallas_call(
        functools.partial(_head_kernel, lat=lat, n_chunks=n_chunks),
        grid=(n_chunks,),
        in_specs=in_specs,
        out_specs=out_specs,
        out_shape=out_shape,
        scratch_shapes=[
            pltpu.VMEM((XROWS_PAD, D_MODEL), F32),
            pltpu.VMEM((XROWS_PAD, D_MODEL), BF16),
            pltpu.VMEM((2, ROWS, LRU_WIDTH), F32),
            pltpu.VMEM((2, ROWS, LRU_WIDTH), F32),
            pltpu.VMEM((GATE_SLOTS, ROWS, LRU_HEAD_DIM), F32),
            pltpu.VMEM((GATE_SLOTS, 2, ROWS, 2 * LRU_HEAD_DIM), F32),
        ],
        name="head_lat" if lat else "head_ctx",
        compiler_params=pltpu.CompilerParams(
            dimension_semantics=("arbitrary",), vmem_limit_bytes=VMEM_LIMIT),
    )(*args)


def _carry(ec_ref, el_ref, hf_ref, hb_ref):
    n_ctx, n_lat = ec_ref.shape[0], el_ref.shape[0]
    h = jnp.zeros(hf_ref.shape[1:], F32)
    for c in range(n_ctx):
        h = ec_ref[c, 1] * h + ec_ref[c, 0]
    for c in range(n_lat):
        hf_ref[c] = h
        h = el_ref[c, 1] * h + el_ref[c, 0]
    h = jnp.zeros(hb_ref.shape[1:], F32)
    for c in reversed(range(n_ctx)):
        h = ec_ref[c, 3] * h + ec_ref[c, 2]
    for c in reversed(range(n_lat)):
        hb_ref[c] = h
        h = el_ref[c, 3] * h + el_ref[c, 2]


def _fourier_kernel(x_ref, chan_ref, time_ref, flip_ref, ec_ref, el_ref, *refs):
    n_w = len(TAIL_WEIGHT_ROWS)
    w_f32_refs, (o_ref, hf_ref, hb_ref) = refs[:n_w], refs[n_w:n_w + 3]
    w_bf16_refs, z_ref = refs[n_w + 3:2 * n_w + 3], refs[-1]
    for src, dst in zip(w_f32_refs, w_bf16_refs):
        dst[...] = src[...].astype(BF16)
    _carry(ec_ref, el_ref, hf_ref, hb_ref)

    half, quarter = SEQ // 2, SEQ // 4
    for g in range(FOURIER_GROUPS):
        sl = slice(g * FOURIER_GROUP_DIM, (g + 1) * FOURIER_GROUP_DIM)
        r = _dot(x_ref[:, sl], chan_ref[...])
        lo, hi = r[:half], r[half:]
        for p, v in enumerate((lo + hi, lo - hi)):
            z_ref[p, 0:half, sl] = v[:, :FOURIER_GROUP_DIM].astype(BF16)
            z_ref[p, half:SEQ, sl] = v[:, FOURIER_GROUP_DIM:].astype(BF16)
    for p in range(2):
        cos_part = _dot(time_ref[p, 0], z_ref[p, 0:half, :])
        sin_part = _dot(time_ref[p, 1], z_ref[p, half:SEQ, :])
        o_ref[p, 0:quarter, :] = (cos_part - sin_part)[0:quarter].astype(BF16)
        mirrored = (cos_part + sin_part).astype(BF16)
        o_ref[p, quarter:half, :] = _dot(flip_ref[p], mirrored).astype(BF16)


def _fourier_call(uf, chan, time, flip, e_ctx, e_lat, weights):
    def slab(w):
        return pl.BlockSpec((w.shape[0] // BATCH, w.shape[1]), lambda n: (n, 0))

    def ends(e):
        return pl.BlockSpec((e.shape[0], 4, BATCH, LRU_WIDTH // BATCH), lambda n: (0, 0, 0, n))

    assert tuple(w.shape[0] for w in weights) == TAIL_WEIGHT_ROWS
    n_lat = e_lat.shape[0]
    state = pl.BlockSpec((n_lat, BATCH, LRU_WIDTH // BATCH), lambda n: (0, 0, n))
    state_shape = jax.ShapeDtypeStruct((n_lat, BATCH, LRU_WIDTH), F32)
    return pl.pallas_call(
        _fourier_kernel,
        grid=(BATCH,),
        in_specs=[
            pl.BlockSpec((SEQ, FOURIER_WIDTH), lambda n: (0, n)),
            pl.BlockSpec((FOURIER_GROUP_DIM, 2 * FOURIER_GROUP_DIM), lambda n: (0, 0)),
            pl.BlockSpec((2, 2, DFT_ROWS, SEQ // 2), lambda n: (0, 0, 0, 0)),
            pl.BlockSpec((2, SEQ // 4, DFT_ROWS), lambda n: (0, 0, 0)),
            ends(e_ctx), ends(e_lat),
        ] + [slab(w) for w in weights],
        out_specs=[pl.BlockSpec((2, SEQ // 2, FOURIER_WIDTH), lambda n: (0, 0, n)), state, state]
        + [slab(w) for w in weights],
        out_shape=[jax.ShapeDtypeStruct((2, SEQ // 2, BATCH * FOURIER_WIDTH), BF16),
                   state_shape, state_shape]
        + [jax.ShapeDtypeStruct(w.shape, BF16) for w in weights],
        scratch_shapes=[pltpu.VMEM((2, SEQ, FOURIER_WIDTH), BF16)],
        name="fourier",
        compiler_params=pltpu.CompilerParams(
            dimension_semantics=("arbitrary",), vmem_limit_bytes=VMEM_LIMIT),
    )(uf, chan, time, flip, e_ctx, e_lat, *weights)


def _tail_kernel(ls_ref, pf_ref, pb_ref, hf_ref, hb_ref, uy_ref, ug_ref, yf_ref, xt_ref,
                 gt1_ref, sh2_ref, sc2_ref, gt2_ref, gm_ref, gf_ref,
                 wl_ref, wf_ref, wo_ref, w1_ref, w2_ref, o_hbm, obuf_ref, osem):
    def tb(v):
        return v.reshape(TT, BATCH, v.shape[-1])

    def flat(v):
        return v.reshape(ROWS, v.shape[-1])

    yf = yf_ref[...].astype(F32).reshape(2, TT // 2, BATCH, FOURIER_WIDTH)
    yf = jnp.swapaxes(yf, 0, 1).reshape(ROWS, FOURIER_WIDTH).astype(BF16)
    y_b = _dot(yf, wf_ref[...])

    hf = jnp.tile(hf_ref[...].astype(BF16), (TT, 1))
    hb = jnp.tile(hb_ref[...].astype(BF16), (TT, 1))
    y_lru = ls_ref[...] + pf_ref[...] * hf + pb_ref[...] * hb
    z = y_lru * _gelu_tanh(uy_ref[...])
    y_a = _dot(z, wl_ref[...])

    g_a = _sigmoid(ug_ref[:, :D_MODEL].astype(F32))
    g_b = _sigmoid(ug_ref[:, D_MODEL:].astype(F32))
    merged = (g_a * y_a + g_b * y_b).astype(BF16)
    x2 = tb(xt_ref[...]) + gt1_ref[...][None] * tb(_dot(merged, wo_ref[...]))

    h = (tb(_rms(flat(x2), gm_ref[...])) * (1.0 + sc2_ref[...][None]) + sh2_ref[...][None])
    h = flat(h).astype(BF16)
    acc = jnp.zeros((ROWS, D_MODEL), F32)
    for k0 in range(0, D_FF, D_MODEL):
        a = jnp.maximum(_dot(h, w1_ref[:, k0:k0 + D_MODEL]), 0.0)
        acc = acc + _dot((a * a).astype(BF16), w2_ref[k0:k0 + D_MODEL, :])
    out = _rms(flat(x2 + gt2_ref[...][None] * tb(acc)), gf_ref[...])

    c = pl.program_id(0)
    slot = c % 2

    def copies(step, slot_):
        return [pltpu.make_async_copy(obuf_ref.at[slot_, :, b, :],
                                      o_hbm.at[b, pl.ds(step * TT, TT), :], osem.at[slot_, b])
                for b in range(BATCH)]

    @pl.when(c >= 2)
    def _():
        for cp in copies(c - 2, slot):
            cp.wait()

    obuf_ref[slot] = tb(out)
    for cp in copies(c, slot):
        cp.start()

    @pl.when(c == N_LAT_CHUNKS - 1)
    def _():
        for cp in copies(c - 1, 1 - slot) + copies(c, slot):
            cp.wait()


def _tail_call(ls, pf, pb, hf, hb, uy, ug, yf, xt, mod, gm, gf, wl, wf, wo, w1, w2):
    def rows(width):
        return pl.BlockSpec((ROWS, width), lambda c: (c, 0))

    def const(shape):
        return pl.BlockSpec(shape, lambda c: (0,) * len(shape))

    state = pl.BlockSpec((None, BATCH, LRU_WIDTH), lambda c: (c, 0, 0))
    vec1 = const((1, D_MODEL))
    return pl.pallas_call(
        _tail_kernel,
        grid=(N_LAT_CHUNKS,),
        in_specs=[
            rows(LRU_WIDTH), rows(LRU_WIDTH), rows(LRU_WIDTH), state, state,
            rows(LRU_WIDTH), rows(2 * D_MODEL),
            pl.BlockSpec((2, TT // 2, BATCH * FOURIER_WIDTH), lambda c: (0, c, 0)),
            rows(D_MODEL),
            _mod_spec(0, 2), _mod_spec(0, 3), _mod_spec(0, 4), _mod_spec(0, 5), vec1, vec1,
            const((LRU_WIDTH, D_MODEL)), const((FOURIER_WIDTH, D_MODEL)), const((D_MODEL, D_MODEL)),
            const((D_MODEL, D_FF)), const((D_FF, D_MODEL)),
        ],
        out_specs=pl.BlockSpec(memory_space=pl.ANY),
        out_shape=jax.ShapeDtypeStruct((BATCH, SEQ, D_MODEL), F32),
        scratch_shapes=[pltpu.VMEM((2, TT, BATCH, D_MODEL), F32), pltpu.SemaphoreType.DMA((2, BATCH))],
        name="tail",
        compiler_params=pltpu.CompilerParams(
            dimension_semantics=("arbitrary",), vmem_limit_bytes=VMEM_LIMIT),
    )(ls, pf, pb, hf, hb, uy, ug, yf, xt, mod, mod, mod, mod, gm, gf, wl, wf, wo, w1, w2)


def kernel(x, c, ctx, c_ctx, w_mod, b_mod, g_mix, w_in, conv_w, conv_b, w_a, b_a, w_x, b_x,
           lam, w_lru_out, w_f_out, w_out, g_mlp, w1, w2, g_final):
    pos_np, chan_np, time_np, flip_np = _constants()
    pos = jnp.asarray(pos_np)
    chan, time, flip = (jnp.asarray(v).astype(BF16) for v in (chan_np, time_np, flip_np))

    cc = jnp.concatenate(
        [c, c_ctx[None], jnp.zeros((2 * SUBLANES - BATCH - 1, D_MODEL), F32)], axis=0)
    mod = _mod_call(cc, w_mod[0], b_mod[0][None])

    g_mix2 = g_mix[0][None]
    wg = jnp.concatenate([w_a[0], w_x[0]], axis=-1).astype(BF16)
    lru = (conv_w[0], conv_b[0][None], wg, b_a[0], b_x[0], lam[0])

    w_in_b, e_ctx = _head_call(ctx, None, mod, g_mix2, w_in[0], *lru, lat=False)
    xt, uy, uf, ug, ls, pf, pb, e_lat = _head_call(x, pos, mod, g_mix2, w_in_b, *lru, lat=True)
    yf, hf, hb, wl, wf, wo, w1b, w2b = _fourier_call(
        uf, chan, time, flip, e_ctx, e_lat, (w_lru_out[0], w_f_out[0], w_out[0], w1[0], w2[0]))
    return _tail_call(ls, pf, pb, hf, hb, uy, ug, yf, xt, mod,
                      g_mlp[0][None], g_final[None], wl, wf, wo, w1b, w2b)
```

```python
import functools
import math

import numpy as np
import jax
import jax.numpy as jnp
from jax import lax
from jax.experimental import pallas as pl
from jax.experimental.pallas import tpu as pltpu

F32 = jnp.float32
BF16 = jnp.bfloat16

D_MODEL = 1024
BATCH = 8
SEQ = 2048
CTX_LEN = 256
GRID_W = 64
LRU_WIDTH = 1024
LRU_HEADS = 8
LRU_HEAD_DIM = LRU_WIDTH // LRU_HEADS
LRU_C = 8.0
CONV_WIDTH = 4
CONV_PAD_LEFT = 2
FOURIER_WIDTH = 512
FOURIER_GROUPS = 4
FOURIER_GROUP_DIM = FOURIER_WIDTH // FOURIER_GROUPS
IN_COLS = 2 * LRU_WIDTH + FOURIER_WIDTH + 2 * D_MODEL
D_FF = 4 * D_MODEL
N_MOD = 6
EPS = 1e-6
POS_MAX_PERIOD = 10000.0

SUBLANES = 8
BF16_ROWS = 16
assert BATCH == SUBLANES

TT = 64
ROWS = TT * BATCH
HALO_T = SUBLANES
LEAD = CONV_PAD_LEFT * BATCH
TRAIL = (CONV_WIDTH - 1 - CONV_PAD_LEFT) * BATCH
XROWS = LEAD + ROWS + TRAIL
XROWS_PAD = -(-XROWS // BF16_ROWS) * BF16_ROWS
N_CTX_CHUNKS = CTX_LEN // TT
N_LAT_CHUNKS = SEQ // TT
NCHUNK = 512
FRONT_ROWS = 128
GATE_AHEAD = 1
GATE_SLOTS = GATE_AHEAD + 1
DFT_ROWS = -(-(SEQ // 4 + 1) // BF16_ROWS) * BF16_ROWS
TAIL_WEIGHT_ROWS = (LRU_WIDTH, FOURIER_WIDTH, D_MODEL, D_MODEL, D_FF)
MOD_TN = 1024
V7X_VMEM_BYTES = 64 * 1024 * 1024
VMEM_RESERVE = 6 * 1024 * 1024
VMEM_LIMIT = V7X_VMEM_BYTES - VMEM_RESERVE
LOG2E = math.log2(math.e)
TINY = 1e-30


def _gelu_tanh(x):
    k0 = jnp.full((1, 1), math.sqrt(2.0 / math.pi), F32).astype(x.dtype)
    k1 = jnp.full((1, 1), 0.044715, F32).astype(x.dtype)
    return 0.5 * x * (1.0 + jnp.tanh(k0 * (x + k1 * (x * x * x))))


def _sigmoid(x):
    return 0.5 * jnp.tanh(0.5 * x) + 0.5


def _dot(a, b):
    return jnp.dot(a, b, preferred_element_type=F32)


def _rms(x, g):
    ms = jnp.mean(x * x, axis=-1, keepdims=True)
    return x * lax.rsqrt(ms + EPS) * g


@functools.cache
def _constants():
    half = D_MODEL // 4
    freqs = np.exp(-math.log(POS_MAX_PERIOD) * np.arange(half, dtype=np.float32) / half).astype(np.float32)

    def sincos(n):
        ang = np.arange(n, dtype=np.float32)[:, None] * freqs[None, :]
        return np.concatenate([np.sin(ang), np.cos(ang)], axis=-1).astype(np.float32)

    rows = SEQ // GRID_W
    er, ec = sincos(rows), sincos(GRID_W)
    pos = np.concatenate([
        np.broadcast_to(er[:, None, :], (rows, GRID_W, D_MODEL // 2)),
        np.broadcast_to(ec[None, :, :], (rows, GRID_W, D_MODEL // 2)),
    ], axis=-1).reshape(SEQ, D_MODEL).astype(np.float32)

    def dft(n):
        k = np.arange(n, dtype=np.int64)
        ang = 2.0 * np.pi * ((k[:, None] * k[None, :]) % n).astype(np.float64) / n
        return np.cos(ang) / math.sqrt(n), np.sin(ang) / math.sqrt(n)

    cc, sc = dft(FOURIER_GROUP_DIM)
    chan = np.concatenate([cc, sc], axis=1).astype(np.float32)
    ct, st = dft(SEQ)
    half, quarter = SEQ // 2, SEQ // 4
    time = np.zeros((2, 2, DFT_ROWS, half), np.float64)
    flip = np.zeros((2, quarter, DFT_ROWS), np.float32)
    for p in range(2):
        n_rows = quarter + 1 - p
        rows = 2 * np.arange(n_rows) + p
        time[p, 0, :n_rows] = ct[rows, :half]
        time[p, 1, :n_rows] = st[rows, :half]
        flip[p, np.arange(quarter), quarter - p - np.arange(quarter)] = 1.0
    return pos, chan, time.astype(np.float32), flip


def _mod_kernel(c_ref, cx_ref, w_ref, b_ref, o_ref):
    c = jnp.concatenate([c_ref[...], jnp.broadcast_to(cx_ref[...], c_ref.shape)], axis=0)
    a = c * _sigmoid(c)
    w = w_ref[...]
    a_hi = a.astype(BF16)
    a_lo = (a - a_hi.astype(F32)).astype(BF16)
    w_hi = w.astype(BF16)
    w_lo = (w - w_hi.astype(F32)).astype(BF16)
    n = a.shape[0]
    both = _dot(jnp.concatenate([a_hi, a_lo], axis=0), w_hi)
    o_ref[...] = both[:n] + _dot(a_hi, w_lo) + both[n:] + b_ref[...]


def _mod_call(c, c_ctx, w_mod, b_mod):
    n = w_mod.shape[1]
    return pl.pallas_call(
        _mod_kernel,
        grid=(n // MOD_TN,),
        in_specs=[
            pl.BlockSpec((SUBLANES, D_MODEL), lambda i: (0, 0)),
            pl.BlockSpec((1, D_MODEL), lambda i: (0, 0)),
            pl.BlockSpec((D_MODEL, MOD_TN), lambda i: (0, i)),
            pl.BlockSpec((1, MOD_TN), lambda i: (0, i)),
        ],
        out_specs=pl.BlockSpec((2 * SUBLANES, MOD_TN), lambda i: (0, i)),
        out_shape=jax.ShapeDtypeStruct((2 * SUBLANES, n), F32),
        name="mod",
        compiler_params=pltpu.CompilerParams(dimension_semantics=("arbitrary",)),
    )(c, c_ctx, w_mod, b_mod)


def _mod_spec(row_block, k):
    return pl.BlockSpec((BATCH, D_MODEL), lambda c: (row_block, k))


def _head_kernel(*refs, lat, n_chunks):
    refs = list(refs)
    xm_ref, xp_ref, xn_ref = refs[:3]
    refs = refs[3:]
    if lat:
        pm_ref, pp_ref, pn_ref = refs[:3]
        refs = refs[3:]
    (sh_ref, sc_ref, g_ref, w_ref, cw_ref, cb_ref, wg_ref, ba_ref, bx_ref, lam_ref) = refs[:10]
    refs = refs[10:]
    if lat:
        xt_out, uy_out, uf_out, ug_out, ls_out, pf_out, pb_out = refs[:7]
        refs = refs[7:]
    else:
        w_slab_ref, w_bf16_out = refs[:2]
        refs = refs[2:]
        w_bf16_out[...] = w_slab_ref[...].astype(BF16)
    e_out, xs_ref, hb_ref, a_ref, b_ref, xh_ref, gate_ref = refs

    c = pl.program_id(0)
    mod_rows = slice(None) if lat else slice(0, 1)

    xm = xm_ref[...]
    xp = xp_ref[...]
    xn = xn_ref[...]
    if lat:
        xm = xm + pm_ref[...][None]
        xp = xp + pp_ref[...][None]
        xn = xn + pn_ref[...][None]
    xs_ref[LEAD:LEAD + ROWS, :] = jnp.swapaxes(xm, 0, 1).reshape(ROWS, D_MODEL)
    xs_ref[0:LEAD, :] = jnp.swapaxes(xp, 0, 1)[HALO_T - CONV_PAD_LEFT:].reshape(LEAD, D_MODEL)
    xs_ref[LEAD + ROWS:XROWS, :] = (
        jnp.swapaxes(xn, 0, 1)[:CONV_WIDTH - 1 - CONV_PAD_LEFT].reshape(TRAIL, D_MODEL))
    xs_ref[XROWS:XROWS_PAD, :] = jnp.zeros((XROWS_PAD - XROWS, D_MODEL), F32)

    assert LEAD % BF16_ROWS == 0 and FRONT_ROWS % BF16_ROWS == 0
    blocks = ([(0, LEAD)] + [(r, r + FRONT_ROWS) for r in range(LEAD, LEAD + ROWS, FRONT_ROWS)]
              + [(LEAD + ROWS, XROWS_PAD)])
    for r0, r1 in blocks:
        xt = xs_ref[r0:r1, :]
        if lat and LEAD <= r0 < LEAD + ROWS:
            xt_out[r0 - LEAD:r1 - LEAD, :] = xt
        h = _rms(xt, g_ref[...]).reshape((r1 - r0) // BATCH, BATCH, D_MODEL)
        h = h * (1.0 + sc_ref[mod_rows, :][None]) + sh_ref[mod_rows, :][None]
        hb_ref[r0:r1, :] = h.reshape(r1 - r0, D_MODEL).astype(BF16)

    keep_first = jnp.where(c == 0, 0.0, 1.0)
    keep_last = jnp.where(c == n_chunks - 1, 0.0, 1.0)
    for n0 in range(0, LRU_WIDTH, NCHUNK):
        w_x = w_ref[:, n0:n0 + NCHUNK]
        xs_ref[:, n0:n0 + NCHUNK] = _dot(hb_ref[...], w_x if lat else w_x.astype(BF16))
    xs_ref[0:LEAD, :] = xs_ref[0:LEAD, :] * keep_first
    xs_ref[LEAD + ROWS:XROWS, :] = xs_ref[LEAD + ROWS:XROWS, :] * keep_last

    def branch_piece(n0):
        r = _dot(hb_ref[LEAD:LEAD + ROWS, :], w_ref[:, n0:n0 + NCHUNK])
        if n0 < 2 * LRU_WIDTH:
            uy_out[:, n0 - LRU_WIDTH:n0 - LRU_WIDTH + NCHUNK] = r.astype(BF16)
        elif n0 < 2 * LRU_WIDTH + FOURIER_WIDTH:
            assert NCHUNK == FOURIER_WIDTH
            uf_out[...] = r.reshape(TT, BATCH * NCHUNK).astype(BF16)
        else:
            c0 = n0 - 2 * LRU_WIDTH - FOURIER_WIDTH
            ug_out[:, c0:c0 + NCHUNK] = r.astype(BF16)

    pieces = list(range(LRU_WIDTH, IN_COLS, NCHUNK)) if lat else []
    per_head = -(-len(pieces) // LRU_HEADS)

    lam = lam_ref[...]
    sp = jnp.maximum(-lam, 0.0) + jnp.log1p(jnp.exp(-jnp.abs(lam)))
    c1 = (-0.5 * LRU_C * LOG2E) * sp

    def conv_gates(hd):
        sl = slice(hd * LRU_HEAD_DIM, (hd + 1) * LRU_HEAD_DIM)
        xh = 0.5 * cb_ref[:, sl]
        for k in range(CONV_WIDTH):
            xh = xh + (0.5 * cw_ref[k:k + 1, sl]) * xs_ref[k * BATCH:k * BATCH + ROWS, sl]
        xh_ref[hd % GATE_SLOTS] = xh
        xhb = xh.astype(BF16)
        for d in range(2):
            gate_ref[hd % GATE_SLOTS, d] = _dot(xhb, wg_ref[d, hd])

    for hd in range(GATE_AHEAD):
        conv_gates(hd)
    def coefficients(hd):
        sl = slice(hd * LRU_HEAD_DIM, (hd + 1) * LRU_HEAD_DIM)
        xh = xh_ref[hd % GATE_SLOTS]
        for d in range(2):
            g = gate_ref[hd % GATE_SLOTS, d]
            ta = jnp.tanh(g[:, :LRU_HEAD_DIM] + 0.5 * ba_ref[d:d + 1, sl])
            ti = jnp.tanh(g[:, LRU_HEAD_DIM:] + 0.5 * bx_ref[d:d + 1, sl])
            a = jnp.exp2(c1[d:d + 1, sl] * ta + c1[d:d + 1, sl])
            v = 1.0 - a * a
            m = v * lax.rsqrt(jnp.maximum(v, TINY))
            a_ref[d, :, sl] = a
            b_ref[d, :, sl] = (m * xh) * (ti + 1.0)

    def scans(hd):
        sl = slice(hd * LRU_HEAD_DIM, (hd + 1) * LRU_HEAD_DIM)
        lf = pf = lb = pb = None
        for t in range(TT):
            rf = slice(t * BATCH, (t + 1) * BATCH)
            rb = slice((TT - 1 - t) * BATCH, (TT - t) * BATCH)
            af, bf = a_ref[0, rf, sl], b_ref[0, rf, sl]
            ab, bb = a_ref[1, rb, sl], b_ref[1, rb, sl]
            if t == 0:
                lf, pf, lb, pb = bf, af, bb, ab
            else:
                lf, pf = af * lf + bf, af * pf
                lb, pb = ab * lb + bb, ab * pb
                b_ref[0, rf, sl] = lf
                a_ref[0, rf, sl] = pf
                b_ref[1, rb, sl] = lb
                a_ref[1, rb, sl] = pb
        e_out[0, :, sl] = lf
        e_out[1, :, sl] = pf
        e_out[2, :, sl] = lb
        e_out[3, :, sl] = pb
        if lat:
            ls_out[:, sl] = (b_ref[0, :, sl] + b_ref[1, :, sl]).astype(BF16)
            pf_out[:, sl] = a_ref[0, :, sl].astype(BF16)
            pb_out[:, sl] = a_ref[1, :, sl].astype(BF16)

    for hd in range(LRU_HEADS):
        for n0 in pieces[hd * per_head:(hd + 1) * per_head]:
            branch_piece(n0)
        coefficients(hd)
        if hd + GATE_AHEAD < LRU_HEADS:
            conv_gates(hd + GATE_AHEAD)
        scans(hd)


def _head_call(x, pos, mod, g, w, conv_w, conv_b, wg, b_a, b_x, lam, *, lat):
    seq = x.shape[1]
    n_chunks = seq // TT
    n_halo = seq // HALO_T
    per_chunk = TT // HALO_T

    def const(shape):
        return pl.BlockSpec(shape, lambda c: (0,) * len(shape))

    def prev_idx(c):
        return jnp.maximum(c * per_chunk - 1, 0)

    def next_idx(c):
        return jnp.minimum((c + 1) * per_chunk, n_halo - 1)

    in_specs = [
        pl.BlockSpec((BATCH, TT, D_MODEL), lambda c: (0, c, 0)),
        pl.BlockSpec((BATCH, HALO_T, D_MODEL), lambda c: (0, prev_idx(c), 0)),
        pl.BlockSpec((BATCH, HALO_T, D_MODEL), lambda c: (0, next_idx(c), 0)),
    ]
    args = [x, x, x]
    if lat:
        in_specs += [
            pl.BlockSpec((TT, D_MODEL), lambda c: (c, 0)),
            pl.BlockSpec((HALO_T, D_MODEL), lambda c: (prev_idx(c), 0)),
            pl.BlockSpec((HALO_T, D_MODEL), lambda c: (next_idx(c), 0)),
        ]
        args += [pos, pos, pos]
    in_specs += [
        _mod_spec(0 if lat else 1, 0), _mod_spec(0 if lat else 1, 1), const((1, D_MODEL)),
        const((D_MODEL, IN_COLS if lat else LRU_WIDTH)),
        const((CONV_WIDTH, LRU_WIDTH)), const((1, LRU_WIDTH)),
        const((2, LRU_HEADS, LRU_HEAD_DIM, 2 * LRU_HEAD_DIM)),
        const((2, LRU_WIDTH)), const((2, LRU_WIDTH)), const((2, LRU_WIDTH)),
    ]
    args += [mod, mod, g, w, conv_w, conv_b, wg, b_a, b_x, lam]
    w_slab = pl.BlockSpec((D_MODEL // n_chunks, IN_COLS), lambda c: (c, 0))
    if not lat:
        in_specs.append(w_slab)
        args.append(w)

    def rows(width):
        return pl.BlockSpec((ROWS, width), lambda c: (c, 0))

    out_specs, out_shape = [], []
    if lat:
        n = n_chunks * ROWS
        out_specs += [rows(D_MODEL), rows(LRU_WIDTH),
                      pl.BlockSpec((TT, BATCH * FOURIER_WIDTH), lambda c: (c, 0)),
                      rows(2 * D_MODEL), rows(LRU_WIDTH), rows(LRU_WIDTH), rows(LRU_WIDTH)]
        out_shape += [
            jax.ShapeDtypeStruct((n, D_MODEL), F32),
            jax.ShapeDtypeStruct((n, LRU_WIDTH), BF16),
            jax.ShapeDtypeStruct((seq, BATCH * FOURIER_WIDTH), BF16),
            jax.ShapeDtypeStruct((n, 2 * D_MODEL), BF16),
            jax.ShapeDtypeStruct((n, LRU_WIDTH), BF16),
            jax.ShapeDtypeStruct((n, LRU_WIDTH), BF16),
            jax.ShapeDtypeStruct((n, LRU_WIDTH), BF16),
        ]
    else:
        out_specs.append(w_slab)
        out_shape.append(jax.ShapeDtypeStruct((D_MODEL, IN_COLS), BF16))
    out_specs.append(pl.BlockSpec((None, 4, BATCH, LRU_WIDTH), lambda c: (c, 0, 0, 0)))
    out_shape.append(jax.ShapeDtypeStruct((n_chunks, 4, BATCH, LRU_WIDTH), F32))

    return pl.pallas_call(
        functools.partial(_head_kernel, lat=lat, n_chunks=n_chunks),
        grid=(n_chunks,),
        in_specs=in_specs,
        out_specs=out_specs,
        out_shape=out_shape,
        scratch_shapes=[
            pltpu.VMEM((XROWS_PAD, D_MODEL), F32),
            pltpu.VMEM((XROWS_PAD, D_MODEL), BF16),
            pltpu.VMEM((2, ROWS, LRU_WIDTH), F32),
            pltpu.VMEM((2, ROWS, LRU_WIDTH), F32),
            pltpu.VMEM((GATE_SLOTS, ROWS, LRU_HEAD_DIM), F32),
            pltpu.VMEM((GATE_SLOTS, 2, ROWS, 2 * LRU_HEAD_DIM), F32),
        ],
        name="head_lat" if lat else "head_ctx",
        compiler_params=pltpu.CompilerParams(
            dimension_semantics=("arbitrary",), vmem_limit_bytes=VMEM_LIMIT),
    )(*args)


def _carry(ec_ref, el_ref, hf_ref, hb_ref):
    n_ctx, n_lat = ec_ref.shape[0], el_ref.shape[0]
    h = jnp.zeros(hf_ref.shape[1:], F32)
    for c in range(n_ctx):
        h = ec_ref[c, 1] * h + ec_ref[c, 0]
    for c in range(n_lat):
        hf_ref[c] = h
        h = el_ref[c, 1] * h + el_ref[c, 0]
    h = jnp.zeros(hb_ref.shape[1:], F32)
    for c in reversed(range(n_ctx)):
        h = ec_ref[c, 3] * h + ec_ref[c, 2]
    for c in reversed(range(n_lat)):
        hb_ref[c] = h
        h = el_ref[c, 3] * h + el_ref[c, 2]


def _fourier_kernel(x_ref, chan_ref, time_ref, flip_ref, ec_ref, el_ref, *refs):
    n_w = len(TAIL_WEIGHT_ROWS)
    w_f32_refs, (o_ref, hf_ref, hb_ref) = refs[:n_w], refs[n_w:n_w + 3]
    w_bf16_refs, z_ref = refs[n_w + 3:2 * n_w + 3], refs[-1]
    for src, dst in zip(w_f32_refs, w_bf16_refs):
        dst[...] = src[...].astype(BF16)
    _carry(ec_ref, el_ref, hf_ref, hb_ref)

    half, quarter = SEQ // 2, SEQ // 4
    for g in range(FOURIER_GROUPS):
        sl = slice(g * FOURIER_GROUP_DIM, (g + 1) * FOURIER_GROUP_DIM)
        r = _dot(x_ref[:, sl], chan_ref[...])
        lo, hi = r[:half], r[half:]
        for p, v in enumerate((lo + hi, lo - hi)):
            z_ref[p, 0:half, sl] = v[:, :FOURIER_GROUP_DIM].astype(BF16)
            z_ref[p, half:SEQ, sl] = v[:, FOURIER_GROUP_DIM:].astype(BF16)
    for p in range(2):
        cos_part = _dot(time_ref[p, 0], z_ref[p, 0:half, :])
        sin_part = _dot(time_ref[p, 1], z_ref[p, half:SEQ, :])
        o_ref[p, 0:quarter, :] = (cos_part - sin_part)[0:quarter].astype(BF16)
        mirrored = (cos_part + sin_part).astype(BF16)
        o_ref[p, quarter:half, :] = _dot(flip_ref[p], mirrored).astype(BF16)


def _fourier_call(uf, chan, time, flip, e_ctx, e_lat, weights):
    def slab(w):
        return pl.BlockSpec((w.shape[0] // BATCH, w.shape[1]), lambda n: (n, 0))

    def ends(e):
        return pl.BlockSpec((e.shape[0], 4, BATCH, LRU_WIDTH // BATCH), lambda n: (0, 0, 0, n))

    assert tuple(w.shape[0] for w in weights) == TAIL_WEIGHT_ROWS
    n_lat = e_lat.shape[0]
    state = pl.BlockSpec((n_lat, BATCH, LRU_WIDTH // BATCH), lambda n: (0, 0, n))
    state_shape = jax.ShapeDtypeStruct((n_lat, BATCH, LRU_WIDTH), F32)
    return pl.pallas_call(
        _fourier_kernel,
        grid=(BATCH,),
        in_specs=[
            pl.BlockSpec((SEQ, FOURIER_WIDTH), lambda n: (0, n)),
            pl.BlockSpec((FOURIER_GROUP_DIM, 2 * FOURIER_GROUP_DIM), lambda n: (0, 0)),
            pl.BlockSpec((2, 2, DFT_ROWS, SEQ // 2), lambda n: (0, 0, 0, 0)),
            pl.BlockSpec((2, SEQ // 4, DFT_ROWS), lambda n: (0, 0, 0)),
            ends(e_ctx), ends(e_lat),
        ] + [slab(w) for w in weights],
        out_specs=[pl.BlockSpec((2, SEQ // 2, FOURIER_WIDTH), lambda n: (0, 0, n)), state, state]
        + [slab(w) for w in weights],
        out_shape=[jax.ShapeDtypeStruct((2, SEQ // 2, BATCH * FOURIER_WIDTH), BF16),
                   state_shape, state_shape]
        + [jax.ShapeDtypeStruct(w.shape, BF16) for w in weights],
        scratch_shapes=[pltpu.VMEM((2, SEQ, FOURIER_WIDTH), BF16)],
        name="fourier",
        compiler_params=pltpu.CompilerParams(
            dimension_semantics=("arbitrary",), vmem_limit_bytes=VMEM_LIMIT),
    )(uf, chan, time, flip, e_ctx, e_lat, *weights)


def _tail_kernel(ls_ref, pf_ref, pb_ref, hf_ref, hb_ref, uy_ref, ug_ref, yf_ref, xt_ref,
                 gt1_ref, sh2_ref, sc2_ref, gt2_ref, gm_ref, gf_ref,
                 wl_ref, wf_ref, wo_ref, w1_ref, w2_ref, o_hbm, obuf_ref, osem):
    def tb(v):
        return v.reshape(TT, BATCH, v.shape[-1])

    def flat(v):
        return v.reshape(ROWS, v.shape[-1])

    yf = yf_ref[...].astype(F32).reshape(2, TT // 2, BATCH, FOURIER_WIDTH)
    yf = jnp.swapaxes(yf, 0, 1).reshape(ROWS, FOURIER_WIDTH).astype(BF16)
    y_b = _dot(yf, wf_ref[...])

    hf = jnp.tile(hf_ref[...].astype(BF16), (TT, 1))
    hb = jnp.tile(hb_ref[...].astype(BF16), (TT, 1))
    y_lru = ls_ref[...] + pf_ref[...] * hf + pb_ref[...] * hb
    z = y_lru * _gelu_tanh(uy_ref[...])
    y_a = _dot(z, wl_ref[...])

    g_a = _sigmoid(ug_ref[:, :D_MODEL].astype(F32))
    g_b = _sigmoid(ug_ref[:, D_MODEL:].astype(F32))
    merged = (g_a * y_a + g_b * y_b).astype(BF16)
    x2 = tb(xt_ref[...]) + gt1_ref[...][None] * tb(_dot(merged, wo_ref[...]))

    h = (tb(_rms(flat(x2), gm_ref[...])) * (1.0 + sc2_ref[...][None]) + sh2_ref[...][None])
    h = flat(h).astype(BF16)
    acc = jnp.zeros((ROWS, D_MODEL), F32)
    for k0 in range(0, D_FF, D_MODEL):
        a = jnp.maximum(_dot(h, w1_ref[:, k0:k0 + D_MODEL]), 0.0)
        acc = acc + _dot((a * a).astype(BF16), w2_ref[k0:k0 + D_MODEL, :])
    out = _rms(flat(x2 + gt2_ref[...][None] * tb(acc)), gf_ref[...])

    c = pl.program_id(0)
    slot = c % 2

    def copies(step, slot_):
        return [pltpu.make_async_copy(obuf_ref.at[slot_, :, b, :],
                                      o_hbm.at[b, pl.ds(step * TT, TT), :], osem.at[slot_, b])
                for b in range(BATCH)]

    @pl.when(c >= 2)
    def _():
        for cp in copies(c - 2, slot):
            cp.wait()

    obuf_ref[slot] = tb(out)
    for cp in copies(c, slot):
        cp.start()

    @pl.when(c == N_LAT_CHUNKS - 1)
    def _():
        for cp in copies(c - 1, 1 - slot) + copies(c, slot):
            cp.wait()


def _tail_call(ls, pf, pb, hf, hb, uy, ug, yf, xt, mod, gm, gf, wl, wf, wo, w1, w2):
    def rows(width):
        return pl.BlockSpec((ROWS, width), lambda c: (c, 0))

    def const(shape):
        return pl.BlockSpec(shape, lambda c: (0,) * len(shape))

    state = pl.BlockSpec((None, BATCH, LRU_WIDTH), lambda c: (c, 0, 0))
    vec1 = const((1, D_MODEL))
    return pl.pallas_call(
        _tail_kernel,
        grid=(N_LAT_CHUNKS,),
        in_specs=[
            rows(LRU_WIDTH), rows(LRU_WIDTH), rows(LRU_WIDTH), state, state,
            rows(LRU_WIDTH), rows(2 * D_MODEL),
            pl.BlockSpec((2, TT // 2, BATCH * FOURIER_WIDTH), lambda c: (0, c, 0)),
            rows(D_MODEL),
            _mod_spec(0, 2), _mod_spec(0, 3), _mod_spec(0, 4), _mod_spec(0, 5), vec1, vec1,
            const((LRU_WIDTH, D_MODEL)), const((FOURIER_WIDTH, D_MODEL)), const((D_MODEL, D_MODEL)),
            const((D_MODEL, D_FF)), const((D_FF, D_MODEL)),
        ],
        out_specs=pl.BlockSpec(memory_space=pl.ANY),
        out_shape=jax.ShapeDtypeStruct((BATCH, SEQ, D_MODEL), F32),
        scratch_shapes=[pltpu.VMEM((2, TT, BATCH, D_MODEL), F32), pltpu.SemaphoreType.DMA((2, BATCH))],
        name="tail",
        compiler_params=pltpu.CompilerParams(
            dimension_semantics=("arbitrary",), vmem_limit_bytes=VMEM_LIMIT),
    )(ls, pf, pb, hf, hb, uy, ug, yf, xt, mod, mod, mod, mod, gm, gf, wl, wf, wo, w1, w2)


def kernel(x, c, ctx, c_ctx, w_mod, b_mod, g_mix, w_in, conv_w, conv_b, w_a, b_a, w_x, b_x,
           lam, w_lru_out, w_f_out, w_out, g_mlp, w1, w2, g_final):
    pos_np, chan_np, time_np, flip_np = _constants()
    pos = jnp.asarray(pos_np)
    chan, time, flip = (jnp.asarray(v).astype(BF16) for v in (chan_np, time_np, flip_np))

    assert BATCH == SUBLANES
    mod = _mod_call(c, c_ctx[None], w_mod[0], b_mod[0][None])

    g_mix2 = g_mix[0][None]
    wg = jnp.concatenate([w_a[0], w_x[0]], axis=-1).astype(BF16)
    lru = (conv_w[0], conv_b[0][None], wg, b_a[0], b_x[0], lam[0])

    w_in_b, e_ctx = _head_call(ctx, None, mod, g_mix2, w_in[0], *lru, lat=False)
    xt, uy, uf, ug, ls, pf, pb, e_lat = _head_call(x, pos, mod, g_mix2, w_in_b, *lru, lat=True)
    yf, hf, hb, wl, wf, wo, w1b, w2b = _fourier_call(
        uf, chan, time, flip, e_ctx, e_lat, (w_lru_out[0], w_f_out[0], w_out[0], w1[0], w2[0]))
    return _tail_call(ls, pf, pb, hf, hb, uy, ug, yf, xt, mod,
                      g_mlp[0][None], g_final[None], wl, wf, wo, w1b, w2b)
```

```python
import functools
import math

import numpy as np
import jax
import jax.numpy as jnp
from jax import lax
from jax.experimental import pallas as pl
from jax.experimental.pallas import tpu as pltpu

F32 = jnp.float32
BF16 = jnp.bfloat16

D_MODEL = 1024
BATCH = 8
SEQ = 2048
CTX_LEN = 256
GRID_W = 64
LRU_WIDTH = 1024
LRU_HEADS = 8
LRU_HEAD_DIM = LRU_WIDTH // LRU_HEADS
LRU_C = 8.0
CONV_WIDTH = 4
CONV_PAD_LEFT = 2
FOURIER_WIDTH = 512
FOURIER_GROUPS = 4
FOURIER_GROUP_DIM = FOURIER_WIDTH // FOURIER_GROUPS
IN_COLS = 2 * LRU_WIDTH + FOURIER_WIDTH + 2 * D_MODEL
D_FF = 4 * D_MODEL
N_MOD = 6
EPS = 1e-6
POS_MAX_PERIOD = 10000.0

SUBLANES = 8
BF16_ROWS = 16
assert BATCH == SUBLANES

TT = 64
ROWS = TT * BATCH
HALO_T = SUBLANES
LEAD = CONV_PAD_LEFT * BATCH
TRAIL = (CONV_WIDTH - 1 - CONV_PAD_LEFT) * BATCH
XROWS = LEAD + ROWS + TRAIL
XROWS_PAD = -(-XROWS // BF16_ROWS) * BF16_ROWS
N_CTX_CHUNKS = CTX_LEN // TT
N_LAT_CHUNKS = SEQ // TT
NCHUNK = 512
FRONT_ROWS = 128
GATE_AHEAD = 1
GATE_SLOTS = GATE_AHEAD + 1
DFT_ROWS = -(-(SEQ // 4 + 1) // BF16_ROWS) * BF16_ROWS
TAIL_WEIGHT_ROWS = (LRU_WIDTH, FOURIER_WIDTH, D_MODEL, D_MODEL, D_FF)
MOD_TN = 1024
V7X_VMEM_BYTES = 64 * 1024 * 1024
VMEM_RESERVE = 6 * 1024 * 1024
VMEM_LIMIT = V7X_VMEM_BYTES - VMEM_RESERVE
LOG2E = math.log2(math.e)
TINY = 1e-30


def _gelu_tanh(x):
    k0 = jnp.full((1, 1), math.sqrt(2.0 / math.pi), F32).astype(x.dtype)
    k1 = jnp.full((1, 1), 0.044715, F32).astype(x.dtype)
    return 0.5 * x * (1.0 + jnp.tanh(k0 * (x + k1 * (x * x * x))))


def _sigmoid(x):
    return 0.5 * jnp.tanh(0.5 * x) + 0.5


def _dot(a, b):
    return jnp.dot(a, b, preferred_element_type=F32)


def _rms(x, g):
    ms = jnp.mean(x * x, axis=-1, keepdims=True)
    return x * lax.rsqrt(ms + EPS) * g


@functools.cache
def _constants():
    half = D_MODEL // 4
    freqs = np.exp(-math.log(POS_MAX_PERIOD) * np.arange(half, dtype=np.float32) / half).astype(np.float32)

    def sincos(n):
        ang = np.arange(n, dtype=np.float32)[:, None] * freqs[None, :]
        return np.concatenate([np.sin(ang), np.cos(ang)], axis=-1).astype(np.float32)

    rows = SEQ // GRID_W
    er, ec = sincos(rows), sincos(GRID_W)
    pos = np.concatenate([
        np.broadcast_to(er[:, None, :], (rows, GRID_W, D_MODEL // 2)),
        np.broadcast_to(ec[None, :, :], (rows, GRID_W, D_MODEL // 2)),
    ], axis=-1).reshape(SEQ, D_MODEL).astype(np.float32)

    def dft(n):
        k = np.arange(n, dtype=np.int64)
        ang = 2.0 * np.pi * ((k[:, None] * k[None, :]) % n).astype(np.float64) / n
        return np.cos(ang) / math.sqrt(n), np.sin(ang) / math.sqrt(n)

    cc, sc = dft(FOURIER_GROUP_DIM)
    chan = np.concatenate([cc, sc], axis=1).astype(np.float32)
    ct, st = dft(SEQ)
    half, quarter = SEQ // 2, SEQ // 4
    time = np.zeros((2, 2, DFT_ROWS, half), np.float64)
    flip = np.zeros((2, quarter, DFT_ROWS), np.float32)
    for p in range(2):
        n_rows = quarter + 1 - p
        rows = 2 * np.arange(n_rows) + p
        time[p, 0, :n_rows] = ct[rows, :half]
        time[p, 1, :n_rows] = st[rows, :half]
        flip[p, np.arange(quarter), quarter - p - np.arange(quarter)] = 1.0
    return pos, chan, time.astype(np.float32), flip


def _mod_kernel(c_ref, cx_ref, w_ref, b_ref, o_ref):
    c = jnp.concatenate([c_ref[...], jnp.broadcast_to(cx_ref[...], c_ref.shape)], axis=0)
    a = c * _sigmoid(c)
    w = w_ref[...]
    a_hi = a.astype(BF16)
    a_lo = (a - a_hi.astype(F32)).astype(BF16)
    w_hi = w.astype(BF16)
    w_lo = (w - w_hi.astype(F32)).astype(BF16)
    n = a.shape[0]
    both = _dot(jnp.concatenate([a_hi, a_lo], axis=0), w_hi)
    o_ref[...] = both[:n] + _dot(a_hi, w_lo) + both[n:] + b_ref[...]


def _mod_call(c, c_ctx, w_mod, b_mod):
    n = w_mod.shape[1]
    return pl.pallas_call(
        _mod_kernel,
        grid=(n // MOD_TN,),
        in_specs=[
            pl.BlockSpec((SUBLANES, D_MODEL), lambda i: (0, 0)),
            pl.BlockSpec((1, D_MODEL), lambda i: (0, 0)),
            pl.BlockSpec((D_MODEL, MOD_TN), lambda i: (0, i)),
            pl.BlockSpec((1, MOD_TN), lambda i: (0, i)),
        ],
        out_specs=pl.BlockSpec((2 * SUBLANES, MOD_TN), lambda i: (0, i)),
        out_shape=jax.ShapeDtypeStruct((2 * SUBLANES, n), F32),
        name="mod",
        compiler_params=pltpu.CompilerParams(dimension_semantics=("arbitrary",)),
    )(c, c_ctx, w_mod, b_mod)


def _mod_spec(row_block, k):
    return pl.BlockSpec((BATCH, D_MODEL), lambda c: (row_block, k))


def _head_kernel(*refs, lat, n_chunks):
    refs = list(refs)
    xm_ref, xp_ref, xn_ref = refs[:3]
    refs = refs[3:]
    if lat:
        pm_ref, pp_ref, pn_ref = refs[:3]
        refs = refs[3:]
    (sh_ref, sc_ref, g_ref, w_ref, cw_ref, cb_ref, wg_ref, ba_ref, bx_ref, lam_ref) = refs[:10]
    refs = refs[10:]
    if lat:
        xt_out, uy_out, uf_out, ug_out, ls_out, pf_out, pb_out = refs[:7]
        refs = refs[7:]
    else:
        w_slab_ref, wx_ref, w_bf16_out, wg_out = refs[:4]
        refs = refs[4:]
        w_bf16_out[...] = w_slab_ref[...].astype(BF16)
    e_out, xs_ref, hb_ref, a_ref, b_ref, xh_ref, gate_ref = refs

    c = pl.program_id(0)
    if not lat:
        wa_ref = wg_ref

        @pl.when(c == 0)
        def _():
            wg_out[:, :, :, :LRU_HEAD_DIM] = wa_ref[...].astype(BF16)
            wg_out[:, :, :, LRU_HEAD_DIM:] = wx_ref[...].astype(BF16)

        wg_ref = wg_out
    mod_rows = slice(None) if lat else slice(0, 1)

    xm = xm_ref[...]
    xp = xp_ref[...]
    xn = xn_ref[...]
    if lat:
        xm = xm + pm_ref[...][None]
        xp = xp + pp_ref[...][None]
        xn = xn + pn_ref[...][None]
    xs_ref[LEAD:LEAD + ROWS, :] = jnp.swapaxes(xm, 0, 1).reshape(ROWS, D_MODEL)
    xs_ref[0:LEAD, :] = jnp.swapaxes(xp, 0, 1)[HALO_T - CONV_PAD_LEFT:].reshape(LEAD, D_MODEL)
    xs_ref[LEAD + ROWS:XROWS, :] = (
        jnp.swapaxes(xn, 0, 1)[:CONV_WIDTH - 1 - CONV_PAD_LEFT].reshape(TRAIL, D_MODEL))
    xs_ref[XROWS:XROWS_PAD, :] = jnp.zeros((XROWS_PAD - XROWS, D_MODEL), F32)

    assert LEAD % BF16_ROWS == 0 and FRONT_ROWS % BF16_ROWS == 0
    blocks = ([(0, LEAD)] + [(r, r + FRONT_ROWS) for r in range(LEAD, LEAD + ROWS, FRONT_ROWS)]
              + [(LEAD + ROWS, XROWS_PAD)])
    for r0, r1 in blocks:
        xt = xs_ref[r0:r1, :]
        if lat and LEAD <= r0 < LEAD + ROWS:
            xt_out[r0 - LEAD:r1 - LEAD, :] = xt
        h = _rms(xt, g_ref[...]).reshape((r1 - r0) // BATCH, BATCH, D_MODEL)
        h = h * (1.0 + sc_ref[mod_rows, :][None]) + sh_ref[mod_rows, :][None]
        hb_ref[r0:r1, :] = h.reshape(r1 - r0, D_MODEL).astype(BF16)

    keep_first = jnp.where(c == 0, 0.0, 1.0)
    keep_last = jnp.where(c == n_chunks - 1, 0.0, 1.0)
    for n0 in range(0, LRU_WIDTH, NCHUNK):
        w_x = w_ref[:, n0:n0 + NCHUNK]
        xs_ref[:, n0:n0 + NCHUNK] = _dot(hb_ref[...], w_x if lat else w_x.astype(BF16))
    xs_ref[0:LEAD, :] = xs_ref[0:LEAD, :] * keep_first
    xs_ref[LEAD + ROWS:XROWS, :] = xs_ref[LEAD + ROWS:XROWS, :] * keep_last

    def branch_piece(n0):
        r = _dot(hb_ref[LEAD:LEAD + ROWS, :], w_ref[:, n0:n0 + NCHUNK])
        if n0 < 2 * LRU_WIDTH:
            uy_out[:, n0 - LRU_WIDTH:n0 - LRU_WIDTH + NCHUNK] = r.astype(BF16)
        elif n0 < 2 * LRU_WIDTH + FOURIER_WIDTH:
            assert NCHUNK == FOURIER_WIDTH
            uf_out[...] = r.reshape(TT, BATCH * NCHUNK).astype(BF16)
        else:
            c0 = n0 - 2 * LRU_WIDTH - FOURIER_WIDTH
            ug_out[:, c0:c0 + NCHUNK] = r.astype(BF16)

    pieces = list(range(LRU_WIDTH, IN_COLS, NCHUNK)) if lat else []
    per_head = -(-len(pieces) // LRU_HEADS)

    lam = lam_ref[...]
    sp = jnp.maximum(-lam, 0.0) + jnp.log1p(jnp.exp(-jnp.abs(lam)))
    c1 = (-0.5 * LRU_C * LOG2E) * sp

    def conv_gates(hd):
        sl = slice(hd * LRU_HEAD_DIM, (hd + 1) * LRU_HEAD_DIM)
        xh = 0.5 * cb_ref[:, sl]
        for k in range(CONV_WIDTH):
            xh = xh + (0.5 * cw_ref[k:k + 1, sl]) * xs_ref[k * BATCH:k * BATCH + ROWS, sl]
        xh_ref[hd % GATE_SLOTS] = xh
        xhb = xh.astype(BF16)
        for d in range(2):
            gate_ref[hd % GATE_SLOTS, d] = _dot(xhb, wg_ref[d, hd])

    for hd in range(GATE_AHEAD):
        conv_gates(hd)
    def coefficients(hd):
        sl = slice(hd * LRU_HEAD_DIM, (hd + 1) * LRU_HEAD_DIM)
        xh = xh_ref[hd % GATE_SLOTS]
        for d in range(2):
            g = gate_ref[hd % GATE_SLOTS, d]
            ta = jnp.tanh(g[:, :LRU_HEAD_DIM] + 0.5 * ba_ref[d:d + 1, sl])
            ti = jnp.tanh(g[:, LRU_HEAD_DIM:] + 0.5 * bx_ref[d:d + 1, sl])
            a = jnp.exp2(c1[d:d + 1, sl] * ta + c1[d:d + 1, sl])
            v = 1.0 - a * a
            m = v * lax.rsqrt(jnp.maximum(v, TINY))
            a_ref[d, :, sl] = a
            b_ref[d, :, sl] = (m * xh) * (ti + 1.0)

    def scans(hd):
        sl = slice(hd * LRU_HEAD_DIM, (hd + 1) * LRU_HEAD_DIM)
        lf = pf = lb = pb = None
        for t in range(TT):
            rf = slice(t * BATCH, (t + 1) * BATCH)
            rb = slice((TT - 1 - t) * BATCH, (TT - t) * BATCH)
            af, bf = a_ref[0, rf, sl], b_ref[0, rf, sl]
            ab, bb = a_ref[1, rb, sl], b_ref[1, rb, sl]
            if t == 0:
                lf, pf, lb, pb = bf, af, bb, ab
            else:
                lf, pf = af * lf + bf, af * pf
                lb, pb = ab * lb + bb, ab * pb
                b_ref[0, rf, sl] = lf
                a_ref[0, rf, sl] = pf
                b_ref[1, rb, sl] = lb
                a_ref[1, rb, sl] = pb
        e_out[0, :, sl] = lf
        e_out[1, :, sl] = pf
        e_out[2, :, sl] = lb
        e_out[3, :, sl] = pb
        if lat:
            ls_out[:, sl] = (b_ref[0, :, sl] + b_ref[1, :, sl]).astype(BF16)
            pf_out[:, sl] = a_ref[0, :, sl].astype(BF16)
            pb_out[:, sl] = a_ref[1, :, sl].astype(BF16)

    for hd in range(LRU_HEADS):
        for n0 in pieces[hd * per_head:(hd + 1) * per_head]:
            branch_piece(n0)
        coefficients(hd)
        if hd + GATE_AHEAD < LRU_HEADS:
            conv_gates(hd + GATE_AHEAD)
        scans(hd)


def _head_call(x, pos, mod, g, w, conv_w, conv_b, wg, b_a, b_x, lam, *, lat):
    seq = x.shape[1]
    n_chunks = seq // TT
    n_halo = seq // HALO_T
    per_chunk = TT // HALO_T

    def const(shape):
        return pl.BlockSpec(shape, lambda c: (0,) * len(shape))

    def prev_idx(c):
        return jnp.maximum(c * per_chunk - 1, 0)

    def next_idx(c):
        return jnp.minimum((c + 1) * per_chunk, n_halo - 1)

    in_specs = [
        pl.BlockSpec((BATCH, TT, D_MODEL), lambda c: (0, c, 0)),
        pl.BlockSpec((BATCH, HALO_T, D_MODEL), lambda c: (0, prev_idx(c), 0)),
        pl.BlockSpec((BATCH, HALO_T, D_MODEL), lambda c: (0, next_idx(c), 0)),
    ]
    args = [x, x, x]
    if lat:
        in_specs += [
            pl.BlockSpec((TT, D_MODEL), lambda c: (c, 0)),
            pl.BlockSpec((HALO_T, D_MODEL), lambda c: (prev_idx(c), 0)),
            pl.BlockSpec((HALO_T, D_MODEL), lambda c: (next_idx(c), 0)),
        ]
        args += [pos, pos, pos]
    in_specs += [
        _mod_spec(0 if lat else 1, 0), _mod_spec(0 if lat else 1, 1), const((1, D_MODEL)),
        const((D_MODEL, IN_COLS if lat else LRU_WIDTH)),
        const((CONV_WIDTH, LRU_WIDTH)), const((1, LRU_WIDTH)),
        const((2, LRU_HEADS, LRU_HEAD_DIM, (2 if lat else 1) * LRU_HEAD_DIM)),
        const((2, LRU_WIDTH)), const((2, LRU_WIDTH)), const((2, LRU_WIDTH)),
    ]
    args += [mod, mod, g, w, conv_w, conv_b, wg if lat else wg[0], b_a, b_x, lam]
    w_slab = pl.BlockSpec((D_MODEL // n_chunks, IN_COLS), lambda c: (c, 0))
    wg_spec = const((2, LRU_HEADS, LRU_HEAD_DIM, 2 * LRU_HEAD_DIM))
    if not lat:
        in_specs += [w_slab, const((2, LRU_HEADS, LRU_HEAD_DIM, LRU_HEAD_DIM))]
        args += [w, wg[1]]

    def rows(width):
        return pl.BlockSpec((ROWS, width), lambda c: (c, 0))

    out_specs, out_shape = [], []
    if lat:
        n = n_chunks * ROWS
        out_specs += [rows(D_MODEL), rows(LRU_WIDTH),
                      pl.BlockSpec((TT, BATCH * FOURIER_WIDTH), lambda c: (c, 0)),
                      rows(2 * D_MODEL), rows(LRU_WIDTH), rows(LRU_WIDTH), rows(LRU_WIDTH)]
        out_shape += [
            jax.ShapeDtypeStruct((n, D_MODEL), F32),
            jax.ShapeDtypeStruct((n, LRU_WIDTH), BF16),
            jax.ShapeDtypeStruct((seq, BATCH * FOURIER_WIDTH), BF16),
            jax.ShapeDtypeStruct((n, 2 * D_MODEL), BF16),
            jax.ShapeDtypeStruct((n, LRU_WIDTH), BF16),
            jax.ShapeDtypeStruct((n, LRU_WIDTH), BF16),
            jax.ShapeDtypeStruct((n, LRU_WIDTH), BF16),
        ]
    else:
        out_specs += [w_slab, wg_spec]
        out_shape += [jax.ShapeDtypeStruct((D_MODEL, IN_COLS), BF16),
                      jax.ShapeDtypeStruct(wg_spec.block_shape, BF16)]
    out_specs.append(pl.BlockSpec((None, 4, BATCH, LRU_WIDTH), lambda c: (c, 0, 0, 0)))
    out_shape.append(jax.ShapeDtypeStruct((n_chunks, 4, BATCH, LRU_WIDTH), F32))

    return pl.pallas_call(
        functools.partial(_head_kernel, lat=lat, n_chunks=n_chunks),
        grid=(n_chunks,),
        in_specs=in_specs,
        out_specs=out_specs,
        out_shape=out_shape,
        scratch_shapes=[
            pltpu.VMEM((XROWS_PAD, D_MODEL), F32),
            pltpu.VMEM((XROWS_PAD, D_MODEL), BF16),
            pltpu.VMEM((2, ROWS, LRU_WIDTH), F32),
            pltpu.VMEM((2, ROWS, LRU_WIDTH), F32),
            pltpu.VMEM((GATE_SLOTS, ROWS, LRU_HEAD_DIM), F32),
            pltpu.VMEM((GATE_SLOTS, 2, ROWS, 2 * LRU_HEAD_DIM), F32),
        ],
        name="head_lat" if lat else "head_ctx",
        compiler_params=pltpu.CompilerParams(
            dimension_semantics=("arbitrary",), vmem_limit_bytes=VMEM_LIMIT),
    )(*args)


def _carry(ec_ref, el_ref, hf_ref, hb_ref):
    n_ctx, n_lat = ec_ref.shape[0], el_ref.shape[0]
    h = jnp.zeros(hf_ref.shape[1:], F32)
    for c in range(n_ctx):
        h = ec_ref[c, 1] * h + ec_ref[c, 0]
    for c in range(n_lat):
        hf_ref[c] = h
        h = el_ref[c, 1] * h + el_ref[c, 0]
    h = jnp.zeros(hb_ref.shape[1:], F32)
    for c in reversed(range(n_ctx)):
        h = ec_ref[c, 3] * h + ec_ref[c, 2]
    for c in reversed(range(n_lat)):
        hb_ref[c] = h
        h = el_ref[c, 3] * h + el_ref[c, 2]


def _fourier_kernel(x_ref, chan_ref, time_ref, flip_ref, ec_ref, el_ref, *refs):
    n_w = len(TAIL_WEIGHT_ROWS)
    w_f32_refs, (o_ref, hf_ref, hb_ref) = refs[:n_w], refs[n_w:n_w + 3]
    w_bf16_refs, z_ref = refs[n_w + 3:2 * n_w + 3], refs[-1]
    for src, dst in zip(w_f32_refs, w_bf16_refs):
        dst[...] = src[...].astype(BF16)
    _carry(ec_ref, el_ref, hf_ref, hb_ref)

    half, quarter = SEQ // 2, SEQ // 4
    for g in range(FOURIER_GROUPS):
        sl = slice(g * FOURIER_GROUP_DIM, (g + 1) * FOURIER_GROUP_DIM)
        r = _dot(x_ref[:, sl], chan_ref[...])
        lo, hi = r[:half], r[half:]
        for p, v in enumerate((lo + hi, lo - hi)):
            z_ref[p, 0:half, sl] = v[:, :FOURIER_GROUP_DIM].astype(BF16)
            z_ref[p, half:SEQ, sl] = v[:, FOURIER_GROUP_DIM:].astype(BF16)
    for p in range(2):
        cos_part = _dot(time_ref[p, 0], z_ref[p, 0:half, :])
        sin_part = _dot(time_ref[p, 1], z_ref[p, half:SEQ, :])
        o_ref[p, 0:quarter, :] = (cos_part - sin_part)[0:quarter].astype(BF16)
        mirrored = (cos_part + sin_part).astype(BF16)
        o_ref[p, quarter:half, :] = _dot(flip_ref[p], mirrored).astype(BF16)


def _fourier_call(uf, chan, time, flip, e_ctx, e_lat, weights):
    def slab(w):
        return pl.BlockSpec((w.shape[0] // BATCH, w.shape[1]), lambda n: (n, 0))

    def ends(e):
        return pl.BlockSpec((e.shape[0], 4, BATCH, LRU_WIDTH // BATCH), lambda n: (0, 0, 0, n))

    assert tuple(w.shape[0] for w in weights) == TAIL_WEIGHT_ROWS
    n_lat = e_lat.shape[0]
    state = pl.BlockSpec((n_lat, BATCH, LRU_WIDTH // BATCH), lambda n: (0, 0, n))
    state_shape = jax.ShapeDtypeStruct((n_lat, BATCH, LRU_WIDTH), F32)
    return pl.pallas_call(
        _fourier_kernel,
        grid=(BATCH,),
        in_specs=[
            pl.BlockSpec((SEQ, FOURIER_WIDTH), lambda n: (0, n)),
            pl.BlockSpec((FOURIER_GROUP_DIM, 2 * FOURIER_GROUP_DIM), lambda n: (0, 0)),
            pl.BlockSpec((2, 2, DFT_ROWS, SEQ // 2), lambda n: (0, 0, 0, 0)),
            pl.BlockSpec((2, SEQ // 4, DFT_ROWS), lambda n: (0, 0, 0)),
            ends(e_ctx), ends(e_lat),
        ] + [slab(w) for w in weights],
        out_specs=[pl.BlockSpec((2, SEQ // 2, FOURIER_WIDTH), lambda n: (0, 0, n)), state, state]
        + [slab(w) for w in weights],
        out_shape=[jax.ShapeDtypeStruct((2, SEQ // 2, BATCH * FOURIER_WIDTH), BF16),
                   state_shape, state_shape]
        + [jax.ShapeDtypeStruct(w.shape, BF16) for w in weights],
        scratch_shapes=[pltpu.VMEM((2, SEQ, FOURIER_WIDTH), BF16)],
        name="fourier",
        compiler_params=pltpu.CompilerParams(
            dimension_semantics=("arbitrary",), vmem_limit_bytes=VMEM_LIMIT),
    )(uf, chan, time, flip, e_ctx, e_lat, *weights)


def _tail_kernel(ls_ref, pf_ref, pb_ref, hf_ref, hb_ref, uy_ref, ug_ref, yf_ref, xt_ref,
                 gt1_ref, sh2_ref, sc2_ref, gt2_ref, gm_ref, gf_ref,
                 wl_ref, wf_ref, wo_ref, w1_ref, w2_ref, o_hbm, obuf_ref, osem):
    def tb(v):
        return v.reshape(TT, BATCH, v.shape[-1])

    def flat(v):
        return v.reshape(ROWS, v.shape[-1])

    yf = yf_ref[...].astype(F32).reshape(2, TT // 2, BATCH, FOURIER_WIDTH)
    yf = jnp.swapaxes(yf, 0, 1).reshape(ROWS, FOURIER_WIDTH).astype(BF16)
    y_b = _dot(yf, wf_ref[...])

    hf = jnp.tile(hf_ref[...].astype(BF16), (TT, 1))
    hb = jnp.tile(hb_ref[...].astype(BF16), (TT, 1))
    y_lru = ls_ref[...] + pf_ref[...] * hf + pb_ref[...] * hb
    z = y_lru * _gelu_tanh(uy_ref[...])
    y_a = _dot(z, wl_ref[...])

    g_a = _sigmoid(ug_ref[:, :D_MODEL].astype(F32))
    g_b = _sigmoid(ug_ref[:, D_MODEL:].astype(F32))
    merged = (g_a * y_a + g_b * y_b).astype(BF16)
    x2 = tb(xt_ref[...]) + gt1_ref[...][None] * tb(_dot(merged, wo_ref[...]))

    h = (tb(_rms(flat(x2), gm_ref[...])) * (1.0 + sc2_ref[...][None]) + sh2_ref[...][None])
    h = flat(h).astype(BF16)
    acc = jnp.zeros((ROWS, D_MODEL), F32)
    for k0 in range(0, D_FF, D_MODEL):
        a = jnp.maximum(_dot(h, w1_ref[:, k0:k0 + D_MODEL]), 0.0)
        acc = acc + _dot((a * a).astype(BF16), w2_ref[k0:k0 + D_MODEL, :])
    out = _rms(flat(x2 + gt2_ref[...][None] * tb(acc)), gf_ref[...])

    c = pl.program_id(0)
    slot = c % 2

    def copies(step, slot_):
        return [pltpu.make_async_copy(obuf_ref.at[slot_, :, b, :],
                                      o_hbm.at[b, pl.ds(step * TT, TT), :], osem.at[slot_, b])
                for b in range(BATCH)]

    @pl.when(c >= 2)
    def _():
        for cp in copies(c - 2, slot):
            cp.wait()

    obuf_ref[slot] = tb(out)
    for cp in copies(c, slot):
        cp.start()

    @pl.when(c == N_LAT_CHUNKS - 1)
    def _():
        for cp in copies(c - 1, 1 - slot) + copies(c, slot):
            cp.wait()


def _tail_call(ls, pf, pb, hf, hb, uy, ug, yf, xt, mod, gm, gf, wl, wf, wo, w1, w2):
    def rows(width):
        return pl.BlockSpec((ROWS, width), lambda c: (c, 0))

    def const(shape):
        return pl.BlockSpec(shape, lambda c: (0,) * len(shape))

    state = pl.BlockSpec((None, BATCH, LRU_WIDTH), lambda c: (c, 0, 0))
    vec1 = const((1, D_MODEL))
    return pl.pallas_call(
        _tail_kernel,
        grid=(N_LAT_CHUNKS,),
        in_specs=[
            rows(LRU_WIDTH), rows(LRU_WIDTH), rows(LRU_WIDTH), state, state,
            rows(LRU_WIDTH), rows(2 * D_MODEL),
            pl.BlockSpec((2, TT // 2, BATCH * FOURIER_WIDTH), lambda c: (0, c, 0)),
            rows(D_MODEL),
            _mod_spec(0, 2), _mod_spec(0, 3), _mod_spec(0, 4), _mod_spec(0, 5), vec1, vec1,
            const((LRU_WIDTH, D_MODEL)), const((FOURIER_WIDTH, D_MODEL)), const((D_MODEL, D_MODEL)),
            const((D_MODEL, D_FF)), const((D_FF, D_MODEL)),
        ],
        out_specs=pl.BlockSpec(memory_space=pl.ANY),
        out_shape=jax.ShapeDtypeStruct((BATCH, SEQ, D_MODEL), F32),
        scratch_shapes=[pltpu.VMEM((2, TT, BATCH, D_MODEL), F32), pltpu.SemaphoreType.DMA((2, BATCH))],
        name="tail",
        compiler_params=pltpu.CompilerParams(
            dimension_semantics=("arbitrary",), vmem_limit_bytes=VMEM_LIMIT),
    )(ls, pf, pb, hf, hb, uy, ug, yf, xt, mod, mod, mod, mod, gm, gf, wl, wf, wo, w1, w2)


def kernel(x, c, ctx, c_ctx, w_mod, b_mod, g_mix, w_in, conv_w, conv_b, w_a, b_a, w_x, b_x,
           lam, w_lru_out, w_f_out, w_out, g_mlp, w1, w2, g_final):
    pos_np, chan_np, time_np, flip_np = _constants()
    pos = jnp.asarray(pos_np)
    chan, time, flip = (jnp.asarray(v).astype(BF16) for v in (chan_np, time_np, flip_np))

    assert BATCH == SUBLANES
    mod = _mod_call(c, c_ctx[None], w_mod[0], b_mod[0][None])

    g_mix2 = g_mix[0][None]
    conv = (conv_w[0], conv_b[0][None])
    lru = (b_a[0], b_x[0], lam[0])

    w_in_b, wg, e_ctx = _head_call(
        ctx, None, mod, g_mix2, w_in[0], *conv, (w_a[0], w_x[0]), *lru, lat=False)
    xt, uy, uf, ug, ls, pf, pb, e_lat = _head_call(
        x, pos, mod, g_mix2, w_in_b, *conv, wg, *lru, lat=True)
    yf, hf, hb, wl, wf, wo, w1b, w2b = _fourier_call(
        uf, chan, time, flip, e_ctx, e_lat, (w_lru_out[0], w_f_out[0], w_out[0], w1[0], w2[0]))
    return _tail_call(ls, pf, pb, hf, hb, uy, ug, yf, xt, mod,
                      g_mlp[0][None], g_final[None], wl, wf, wo, w1b, w2b)
```

```python
import functools
import math

import numpy as np
import jax
import jax.numpy as jnp
from jax import lax
from jax.experimental import pallas as pl
from jax.experimental.pallas import tpu as pltpu

F32 = jnp.float32
BF16 = jnp.bfloat16

D_MODEL = 1024
BATCH = 8
SEQ = 2048
CTX_LEN = 256
GRID_W = 64
LRU_WIDTH = 1024
LRU_HEADS = 8
LRU_HEAD_DIM = LRU_WIDTH // LRU_HEADS
LRU_C = 8.0
CONV_WIDTH = 4
CONV_PAD_LEFT = 2
FOURIER_WIDTH = 512
FOURIER_GROUPS = 4
FOURIER_GROUP_DIM = FOURIER_WIDTH // FOURIER_GROUPS
IN_COLS = 2 * LRU_WIDTH + FOURIER_WIDTH + 2 * D_MODEL
D_FF = 4 * D_MODEL
N_MOD = 6
EPS = 1e-6
POS_MAX_PERIOD = 10000.0

SUBLANES = 8
BF16_ROWS = 16
assert BATCH == SUBLANES

TT = 64
ROWS = TT * BATCH
HALO_T = SUBLANES
LEAD = CONV_PAD_LEFT * BATCH
TRAIL = (CONV_WIDTH - 1 - CONV_PAD_LEFT) * BATCH
XROWS = LEAD + ROWS + TRAIL
XROWS_PAD = -(-XROWS // BF16_ROWS) * BF16_ROWS
N_CTX_CHUNKS = CTX_LEN // TT
N_LAT_CHUNKS = SEQ // TT
NCHUNK = 512
FRONT_ROWS = 128
GATE_AHEAD = 1
GATE_SLOTS = GATE_AHEAD + 1
DFT_ROWS = -(-(SEQ // 4 + 1) // BF16_ROWS) * BF16_ROWS
TAIL_WEIGHT_ROWS = (LRU_WIDTH, FOURIER_WIDTH, D_MODEL, D_MODEL, D_FF)
MOD_TN = 1024
V7X_VMEM_BYTES = 64 * 1024 * 1024
VMEM_RESERVE = 6 * 1024 * 1024
VMEM_LIMIT = V7X_VMEM_BYTES - VMEM_RESERVE
LOG2E = math.log2(math.e)
TINY = 1e-30


def _gelu_tanh(x):
    k0 = jnp.full((1, 1), math.sqrt(2.0 / math.pi), F32).astype(x.dtype)
    k1 = jnp.full((1, 1), 0.044715, F32).astype(x.dtype)
    return 0.5 * x * (1.0 + jnp.tanh(k0 * (x + k1 * (x * x * x))))


def _sigmoid(x):
    return 0.5 * jnp.tanh(0.5 * x) + 0.5


def _dot(a, b):
    return jnp.dot(a, b, preferred_element_type=F32)


def _rms(x, g):
    ms = jnp.mean(x * x, axis=-1, keepdims=True)
    return x * lax.rsqrt(ms + EPS) * g


@functools.cache
def _constants():
    half = D_MODEL // 4
    freqs = np.exp(-math.log(POS_MAX_PERIOD) * np.arange(half, dtype=np.float32) / half).astype(np.float32)

    def sincos(n):
        ang = np.arange(n, dtype=np.float32)[:, None] * freqs[None, :]
        return np.concatenate([np.sin(ang), np.cos(ang)], axis=-1).astype(np.float32)

    rows = SEQ // GRID_W
    er, ec = sincos(rows), sincos(GRID_W)
    pos = np.concatenate([
        np.broadcast_to(er[:, None, :], (rows, GRID_W, D_MODEL // 2)),
        np.broadcast_to(ec[None, :, :], (rows, GRID_W, D_MODEL // 2)),
    ], axis=-1).reshape(SEQ, D_MODEL).astype(np.float32)

    def dft(n):
        k = np.arange(n, dtype=np.int64)
        ang = 2.0 * np.pi * ((k[:, None] * k[None, :]) % n).astype(np.float64) / n
        return np.cos(ang) / math.sqrt(n), np.sin(ang) / math.sqrt(n)

    cc, sc = dft(FOURIER_GROUP_DIM)
    chan = np.concatenate([cc, sc], axis=1).astype(np.float32)
    ct, st = dft(SEQ)
    half, quarter = SEQ // 2, SEQ // 4
    time = np.zeros((2, 2, DFT_ROWS, half), np.float64)
    flip = np.zeros((2, quarter, DFT_ROWS), np.float32)
    for p in range(2):
        n_rows = quarter + 1 - p
        rows = 2 * np.arange(n_rows) + p
        time[p, 0, :n_rows] = ct[rows, :half]
        time[p, 1, :n_rows] = st[rows, :half]
        flip[p, np.arange(quarter), quarter - p - np.arange(quarter)] = 1.0
    return pos, chan, time.astype(np.float32), flip


def _mod_kernel(c_ref, cx_ref, w_ref, b_ref, o_ref):
    c = jnp.concatenate([c_ref[...], jnp.broadcast_to(cx_ref[...], c_ref.shape)], axis=0)
    a = c * _sigmoid(c)
    w = w_ref[...]
    a_hi = a.astype(BF16)
    a_lo = (a - a_hi.astype(F32)).astype(BF16)
    w_hi = w.astype(BF16)
    w_lo = (w - w_hi.astype(F32)).astype(BF16)
    n = a.shape[0]
    both = _dot(jnp.concatenate([a_hi, a_lo], axis=0), w_hi)
    o_ref[...] = both[:n] + _dot(a_hi, w_lo) + both[n:] + b_ref[...]


def _mod_call(c, c_ctx, w_mod, b_mod):
    n = w_mod.shape[1]
    return pl.pallas_call(
        _mod_kernel,
        grid=(n // MOD_TN,),
        in_specs=[
            pl.BlockSpec((SUBLANES, D_MODEL), lambda i: (0, 0)),
            pl.BlockSpec((1, D_MODEL), lambda i: (0, 0)),
            pl.BlockSpec((D_MODEL, MOD_TN), lambda i: (0, i)),
            pl.BlockSpec((1, MOD_TN), lambda i: (0, i)),
        ],
        out_specs=pl.BlockSpec((2 * SUBLANES, MOD_TN), lambda i: (0, i)),
        out_shape=jax.ShapeDtypeStruct((2 * SUBLANES, n), F32),
        name="mod",
        compiler_params=pltpu.CompilerParams(dimension_semantics=("arbitrary",)),
    )(c, c_ctx, w_mod, b_mod)


def _mod_spec(row_block, k):
    return pl.BlockSpec((BATCH, D_MODEL), lambda c: (row_block, k))


def _head_kernel(*refs, lat, n_chunks):
    refs = list(refs)
    xm_ref, xp_ref, xn_ref = refs[:3]
    refs = refs[3:]
    if lat:
        pm_ref, pp_ref, pn_ref = refs[:3]
        refs = refs[3:]
    (sh_ref, sc_ref, g_ref, w_ref, cw_ref, cb_ref, wg_ref, ba_ref, bx_ref, lam_ref) = refs[:10]
    refs = refs[10:]
    if lat:
        xt_out, uy_out, uf_out, ug_out, ls_out, pf_out, pb_out = refs[:7]
        refs = refs[7:]
    else:
        w_slab_ref, wx_ref, w_bf16_out, wg_out = refs[:4]
        refs = refs[4:]
        w_bf16_out[...] = w_slab_ref[...].astype(BF16)
    e_out, xs_ref, hb_ref, a_ref, b_ref, xh_ref, gate_ref = refs

    c = pl.program_id(0)
    if not lat:
        wa_ref = wg_ref

        @pl.when(c == 0)
        def _():
            wg_out[:, :, :, :LRU_HEAD_DIM] = wa_ref[...].astype(BF16)
            wg_out[:, :, :, LRU_HEAD_DIM:] = wx_ref[...].astype(BF16)

        wg_ref = wg_out
    mod_rows = slice(None) if lat else slice(0, 1)

    xm = xm_ref[...]
    xp = xp_ref[...]
    xn = xn_ref[...]
    if lat:
        xm = xm + pm_ref[...][None]
        xp = xp + pp_ref[...][None]
        xn = xn + pn_ref[...][None]
    xs_ref[LEAD:LEAD + ROWS, :] = jnp.swapaxes(xm, 0, 1).reshape(ROWS, D_MODEL)
    xs_ref[0:LEAD, :] = jnp.swapaxes(xp, 0, 1)[HALO_T - CONV_PAD_LEFT:].reshape(LEAD, D_MODEL)
    xs_ref[LEAD + ROWS:XROWS, :] = (
        jnp.swapaxes(xn, 0, 1)[:CONV_WIDTH - 1 - CONV_PAD_LEFT].reshape(TRAIL, D_MODEL))
    xs_ref[XROWS:XROWS_PAD, :] = jnp.zeros((XROWS_PAD - XROWS, D_MODEL), F32)

    assert LEAD % BF16_ROWS == 0 and FRONT_ROWS % BF16_ROWS == 0
    blocks = ([(0, LEAD)] + [(r, r + FRONT_ROWS) for r in range(LEAD, LEAD + ROWS, FRONT_ROWS)]
              + [(LEAD + ROWS, XROWS_PAD)])
    for r0, r1 in blocks:
        xt = xs_ref[r0:r1, :]
        if lat and LEAD <= r0 < LEAD + ROWS:
            xt_out[r0 - LEAD:r1 - LEAD, :] = xt
        h = _rms(xt, g_ref[...]).reshape((r1 - r0) // BATCH, BATCH, D_MODEL)
        h = h * (1.0 + sc_ref[mod_rows, :][None]) + sh_ref[mod_rows, :][None]
        hb_ref[r0:r1, :] = h.reshape(r1 - r0, D_MODEL).astype(BF16)

    keep_first = jnp.where(c == 0, 0.0, 1.0)
    keep_last = jnp.where(c == n_chunks - 1, 0.0, 1.0)
    for n0 in range(0, LRU_WIDTH, NCHUNK):
        w_x = w_ref[:, n0:n0 + NCHUNK]
        xs_ref[:, n0:n0 + NCHUNK] = _dot(hb_ref[...], w_x if lat else w_x.astype(BF16))
    xs_ref[0:LEAD, :] = xs_ref[0:LEAD, :] * keep_first
    xs_ref[LEAD + ROWS:XROWS, :] = xs_ref[LEAD + ROWS:XROWS, :] * keep_last

    def branch_piece(n0):
        r = _dot(hb_ref[LEAD:LEAD + ROWS, :], w_ref[:, n0:n0 + NCHUNK])
        if n0 < 2 * LRU_WIDTH:
            uy_out[:, n0 - LRU_WIDTH:n0 - LRU_WIDTH + NCHUNK] = r.astype(BF16)
        elif n0 < 2 * LRU_WIDTH + FOURIER_WIDTH:
            assert NCHUNK == FOURIER_WIDTH
            uf_out[...] = r.reshape(TT, BATCH * NCHUNK).astype(BF16)
        else:
            c0 = n0 - 2 * LRU_WIDTH - FOURIER_WIDTH
            ug_out[:, c0:c0 + NCHUNK] = r.astype(BF16)

    pieces = list(range(LRU_WIDTH, IN_COLS, NCHUNK)) if lat else []
    per_head = -(-len(pieces) // LRU_HEADS)

    lam = lam_ref[...]
    sp = jnp.maximum(-lam, 0.0) + jnp.log1p(jnp.exp(-jnp.abs(lam)))
    c1 = (-0.5 * LRU_C * LOG2E) * sp

    def conv_gates(hd):
        sl = slice(hd * LRU_HEAD_DIM, (hd + 1) * LRU_HEAD_DIM)
        xh = 0.5 * cb_ref[:, sl]
        for k in range(CONV_WIDTH):
            xh = xh + (0.5 * cw_ref[k:k + 1, sl]) * xs_ref[k * BATCH:k * BATCH + ROWS, sl]
        xh_ref[hd % GATE_SLOTS] = xh
        xhb = xh.astype(BF16)
        for d in range(2):
            gate_ref[hd % GATE_SLOTS, d] = _dot(xhb, wg_ref[d, hd])

    for hd in range(GATE_AHEAD):
        conv_gates(hd)
    def coefficients(hd):
        sl = slice(hd * LRU_HEAD_DIM, (hd + 1) * LRU_HEAD_DIM)
        xh = xh_ref[hd % GATE_SLOTS]
        for d in range(2):
            g = gate_ref[hd % GATE_SLOTS, d]
            ta = jnp.tanh(g[:, :LRU_HEAD_DIM] + 0.5 * ba_ref[d:d + 1, sl])
            ti = jnp.tanh(g[:, LRU_HEAD_DIM:] + 0.5 * bx_ref[d:d + 1, sl])
            a = jnp.exp2(c1[d:d + 1, sl] * ta + c1[d:d + 1, sl])
            v = 1.0 - a * a
            m = v * lax.rsqrt(jnp.maximum(v, TINY))
            a_ref[d, :, sl] = a
            b_ref[d, :, sl] = (m * xh) * (ti + 1.0)

    def scans(hd):
        sl = slice(hd * LRU_HEAD_DIM, (hd + 1) * LRU_HEAD_DIM)
        lf = pf = lb = pb = None
        for t in range(TT):
            rf = slice(t * BATCH, (t + 1) * BATCH)
            rb = slice((TT - 1 - t) * BATCH, (TT - t) * BATCH)
            af, bf = a_ref[0, rf, sl], b_ref[0, rf, sl]
            ab, bb = a_ref[1, rb, sl], b_ref[1, rb, sl]
            if t == 0:
                lf, pf, lb, pb = bf, af, bb, ab
            else:
                lf, pf = af * lf + bf, af * pf
                lb, pb = ab * lb + bb, ab * pb
                if lat:
                    b_ref[0, rf, sl] = lf
                    a_ref[0, rf, sl] = pf
                    b_ref[1, rb, sl] = lb
                    a_ref[1, rb, sl] = pb
        e_out[0, :, sl] = lf
        e_out[1, :, sl] = pf
        e_out[2, :, sl] = lb
        e_out[3, :, sl] = pb
        if lat:
            ls_out[:, sl] = (b_ref[0, :, sl] + b_ref[1, :, sl]).astype(BF16)
            pf_out[:, sl] = a_ref[0, :, sl].astype(BF16)
            pb_out[:, sl] = a_ref[1, :, sl].astype(BF16)

    for hd in range(LRU_HEADS):
        for n0 in pieces[hd * per_head:(hd + 1) * per_head]:
            branch_piece(n0)
        coefficients(hd)
        if hd + GATE_AHEAD < LRU_HEADS:
            conv_gates(hd + GATE_AHEAD)
        scans(hd)


def _head_call(x, pos, mod, g, w, conv_w, conv_b, wg, b_a, b_x, lam, *, lat):
    seq = x.shape[1]
    n_chunks = seq // TT
    n_halo = seq // HALO_T
    per_chunk = TT // HALO_T

    def const(shape):
        return pl.BlockSpec(shape, lambda c: (0,) * len(shape))

    def prev_idx(c):
        return jnp.maximum(c * per_chunk - 1, 0)

    def next_idx(c):
        return jnp.minimum((c + 1) * per_chunk, n_halo - 1)

    in_specs = [
        pl.BlockSpec((BATCH, TT, D_MODEL), lambda c: (0, c, 0)),
        pl.BlockSpec((BATCH, HALO_T, D_MODEL), lambda c: (0, prev_idx(c), 0)),
        pl.BlockSpec((BATCH, HALO_T, D_MODEL), lambda c: (0, next_idx(c), 0)),
    ]
    args = [x, x, x]
    if lat:
        in_specs += [
            pl.BlockSpec((TT, D_MODEL), lambda c: (c, 0)),
            pl.BlockSpec((HALO_T, D_MODEL), lambda c: (prev_idx(c), 0)),
            pl.BlockSpec((HALO_T, D_MODEL), lambda c: (next_idx(c), 0)),
        ]
        args += [pos, pos, pos]
    in_specs += [
        _mod_spec(0 if lat else 1, 0), _mod_spec(0 if lat else 1, 1), const((1, D_MODEL)),
        const((D_MODEL, IN_COLS if lat else LRU_WIDTH)),
        const((CONV_WIDTH, LRU_WIDTH)), const((1, LRU_WIDTH)),
        const((2, LRU_HEADS, LRU_HEAD_DIM, (2 if lat else 1) * LRU_HEAD_DIM)),
        const((2, LRU_WIDTH)), const((2, LRU_WIDTH)), const((2, LRU_WIDTH)),
    ]
    args += [mod, mod, g, w, conv_w, conv_b, wg if lat else wg[0], b_a, b_x, lam]
    w_slab = pl.BlockSpec((D_MODEL // n_chunks, IN_COLS), lambda c: (c, 0))
    wg_spec = const((2, LRU_HEADS, LRU_HEAD_DIM, 2 * LRU_HEAD_DIM))
    if not lat:
        in_specs += [w_slab, const((2, LRU_HEADS, LRU_HEAD_DIM, LRU_HEAD_DIM))]
        args += [w, wg[1]]

    def rows(width):
        return pl.BlockSpec((ROWS, width), lambda c: (c, 0))

    out_specs, out_shape = [], []
    if lat:
        n = n_chunks * ROWS
        out_specs += [rows(D_MODEL), rows(LRU_WIDTH),
                      pl.BlockSpec((TT, BATCH * FOURIER_WIDTH), lambda c: (c, 0)),
                      rows(2 * D_MODEL), rows(LRU_WIDTH), rows(LRU_WIDTH), rows(LRU_WIDTH)]
        out_shape += [
            jax.ShapeDtypeStruct((n, D_MODEL), F32),
            jax.ShapeDtypeStruct((n, LRU_WIDTH), BF16),
            jax.ShapeDtypeStruct((seq, BATCH * FOURIER_WIDTH), BF16),
            jax.ShapeDtypeStruct((n, 2 * D_MODEL), BF16),
            jax.ShapeDtypeStruct((n, LRU_WIDTH), BF16),
            jax.ShapeDtypeStruct((n, LRU_WIDTH), BF16),
            jax.ShapeDtypeStruct((n, LRU_WIDTH), BF16),
        ]
    else:
        out_specs += [w_slab, wg_spec]
        out_shape += [jax.ShapeDtypeStruct((D_MODEL, IN_COLS), BF16),
                      jax.ShapeDtypeStruct(wg_spec.block_shape, BF16)]
    out_specs.append(pl.BlockSpec((None, 4, BATCH, LRU_WIDTH), lambda c: (c, 0, 0, 0)))
    out_shape.append(jax.ShapeDtypeStruct((n_chunks, 4, BATCH, LRU_WIDTH), F32))

    return pl.pallas_call(
        functools.partial(_head_kernel, lat=lat, n_chunks=n_chunks),
        grid=(n_chunks,),
        in_specs=in_specs,
        out_specs=out_specs,
        out_shape=out_shape,
        scratch_shapes=[
            pltpu.VMEM((XROWS_PAD, D_MODEL), F32),
            pltpu.VMEM((XROWS_PAD, D_MODEL), BF16),
            pltpu.VMEM((2, ROWS, LRU_WIDTH), F32),
            pltpu.VMEM((2, ROWS, LRU_WIDTH), F32),
            pltpu.VMEM((GATE_SLOTS, ROWS, LRU_HEAD_DIM), F32),
            pltpu.VMEM((GATE_SLOTS, 2, ROWS, 2 * LRU_HEAD_DIM), F32),
        ],
        name="head_lat" if lat else "head_ctx",
        compiler_params=pltpu.CompilerParams(
            dimension_semantics=("arbitrary",), vmem_limit_bytes=VMEM_LIMIT),
    )(*args)


def _carry(ec_ref, el_ref, hf_ref, hb_ref):
    n_ctx, n_lat = ec_ref.shape[0], el_ref.shape[0]
    h = jnp.zeros(hf_ref.shape[1:], F32)
    for c in range(n_ctx):
        h = ec_ref[c, 1] * h + ec_ref[c, 0]
    for c in range(n_lat):
        hf_ref[c] = h
        h = el_ref[c, 1] * h + el_ref[c, 0]
    h = jnp.zeros(hb_ref.shape[1:], F32)
    for c in reversed(range(n_ctx)):
        h = ec_ref[c, 3] * h + ec_ref[c, 2]
    for c in reversed(range(n_lat)):
        hb_ref[c] = h
        h = el_ref[c, 3] * h + el_ref[c, 2]


def _fourier_kernel(x_ref, chan_ref, time_ref, flip_ref, ec_ref, el_ref, *refs):
    n_w = len(TAIL_WEIGHT_ROWS)
    w_f32_refs, (o_ref, hf_ref, hb_ref) = refs[:n_w], refs[n_w:n_w + 3]
    w_bf16_refs, z_ref = refs[n_w + 3:2 * n_w + 3], refs[-1]
    for src, dst in zip(w_f32_refs, w_bf16_refs):
        dst[...] = src[...].astype(BF16)
    _carry(ec_ref, el_ref, hf_ref, hb_ref)

    half, quarter = SEQ // 2, SEQ // 4
    for g in range(FOURIER_GROUPS):
        sl = slice(g * FOURIER_GROUP_DIM, (g + 1) * FOURIER_GROUP_DIM)
        r = _dot(x_ref[:, sl], chan_ref[...])
        lo, hi = r[:half], r[half:]
        for p, v in enumerate((lo + hi, lo - hi)):
            z_ref[p, 0:half, sl] = v[:, :FOURIER_GROUP_DIM].astype(BF16)
            z_ref[p, half:SEQ, sl] = v[:, FOURIER_GROUP_DIM:].astype(BF16)
    for p in range(2):
        cos_part = _dot(time_ref[p, 0], z_ref[p, 0:half, :])
        sin_part = _dot(time_ref[p, 1], z_ref[p, half:SEQ, :])
        o_ref[p, 0:quarter, :] = (cos_part - sin_part)[0:quarter].astype(BF16)
        mirrored = (cos_part + sin_part).astype(BF16)
        o_ref[p, quarter:half, :] = _dot(flip_ref[p], mirrored).astype(BF16)


def _fourier_call(uf, chan, time, flip, e_ctx, e_lat, weights):
    def slab(w):
        return pl.BlockSpec((w.shape[0] // BATCH, w.shape[1]), lambda n: (n, 0))

    def ends(e):
        return pl.BlockSpec((e.shape[0], 4, BATCH, LRU_WIDTH // BATCH), lambda n: (0, 0, 0, n))

    assert tuple(w.shape[0] for w in weights) == TAIL_WEIGHT_ROWS
    n_lat = e_lat.shape[0]
    state = pl.BlockSpec((n_lat, BATCH, LRU_WIDTH // BATCH), lambda n: (0, 0, n))
    state_shape = jax.ShapeDtypeStruct((n_lat, BATCH, LRU_WIDTH), F32)
    return pl.pallas_call(
        _fourier_kernel,
        grid=(BATCH,),
        in_specs=[
            pl.BlockSpec((SEQ, FOURIER_WIDTH), lambda n: (0, n)),
            pl.BlockSpec((FOURIER_GROUP_DIM, 2 * FOURIER_GROUP_DIM), lambda n: (0, 0)),
            pl.BlockSpec((2, 2, DFT_ROWS, SEQ // 2), lambda n: (0, 0, 0, 0)),
            pl.BlockSpec((2, SEQ // 4, DFT_ROWS), lambda n: (0, 0, 0)),
            ends(e_ctx), ends(e_lat),
        ] + [slab(w) for w in weights],
        out_specs=[pl.BlockSpec((2, SEQ // 2, FOURIER_WIDTH), lambda n: (0, 0, n)), state, state]
        + [slab(w) for w in weights],
        out_shape=[jax.ShapeDtypeStruct((2, SEQ // 2, BATCH * FOURIER_WIDTH), BF16),
                   state_shape, state_shape]
        + [jax.ShapeDtypeStruct(w.shape, BF16) for w in weights],
        scratch_shapes=[pltpu.VMEM((2, SEQ, FOURIER_WIDTH), BF16)],
        name="fourier",
        compiler_params=pltpu.CompilerParams(
            dimension_semantics=("arbitrary",), vmem_limit_bytes=VMEM_LIMIT),
    )(uf, chan, time, flip, e_ctx, e_lat, *weights)


def _tail_kernel(ls_ref, pf_ref, pb_ref, hf_ref, hb_ref, uy_ref, ug_ref, yf_ref, xt_ref,
                 gt1_ref, sh2_ref, sc2_ref, gt2_ref, gm_ref, gf_ref,
                 wl_ref, wf_ref, wo_ref, w1_ref, w2_ref, o_hbm, obuf_ref, osem):
    def tb(v):
        return v.reshape(TT, BATCH, v.shape[-1])

    def flat(v):
        return v.reshape(ROWS, v.shape[-1])

    yf = yf_ref[...].astype(F32).reshape(2, TT // 2, BATCH, FOURIER_WIDTH)
    yf = jnp.swapaxes(yf, 0, 1).reshape(ROWS, FOURIER_WIDTH).astype(BF16)
    y_b = _dot(yf, wf_ref[...])

    hf = jnp.tile(hf_ref[...].astype(BF16), (TT, 1))
    hb = jnp.tile(hb_ref[...].astype(BF16), (TT, 1))
    y_lru = ls_ref[...] + pf_ref[...] * hf + pb_ref[...] * hb
    z = y_lru * _gelu_tanh(uy_ref[...])
    y_a = _dot(z, wl_ref[...])

    g_a = _sigmoid(ug_ref[:, :D_MODEL].astype(F32))
    g_b = _sigmoid(ug_ref[:, D_MODEL:].astype(F32))
    merged = (g_a * y_a + g_b * y_b).astype(BF16)
    x2 = tb(xt_ref[...]) + gt1_ref[...][None] * tb(_dot(merged, wo_ref[...]))

    h = (tb(_rms(flat(x2), gm_ref[...])) * (1.0 + sc2_ref[...][None]) + sh2_ref[...][None])
    h = flat(h).astype(BF16)
    acc = jnp.zeros((ROWS, D_MODEL), F32)
    for k0 in range(0, D_FF, D_MODEL):
        a = jnp.maximum(_dot(h, w1_ref[:, k0:k0 + D_MODEL]), 0.0)
        acc = acc + _dot((a * a).astype(BF16), w2_ref[k0:k0 + D_MODEL, :])
    out = _rms(flat(x2 + gt2_ref[...][None] * tb(acc)), gf_ref[...])

    c = pl.program_id(0)
    slot = c % 2

    def copies(step, slot_):
        return [pltpu.make_async_copy(obuf_ref.at[slot_, :, b, :],
                                      o_hbm.at[b, pl.ds(step * TT, TT), :], osem.at[slot_, b])
                for b in range(BATCH)]

    @pl.when(c >= 2)
    def _():
        for cp in copies(c - 2, slot):
            cp.wait()

    obuf_ref[slot] = tb(out)
    for cp in copies(c, slot):
        cp.start()

    @pl.when(c == N_LAT_CHUNKS - 1)
    def _():
        for cp in copies(c - 1, 1 - slot) + copies(c, slot):
            cp.wait()


def _tail_call(ls, pf, pb, hf, hb, uy, ug, yf, xt, mod, gm, gf, wl, wf, wo, w1, w2):
    def rows(width):
        return pl.BlockSpec((ROWS, width), lambda c: (c, 0))

    def const(shape):
        return pl.BlockSpec(shape, lambda c: (0,) * len(shape))

    state = pl.BlockSpec((None, BATCH, LRU_WIDTH), lambda c: (c, 0, 0))
    vec1 = const((1, D_MODEL))
    return pl.pallas_call(
        _tail_kernel,
        grid=(N_LAT_CHUNKS,),
        in_specs=[
            rows(LRU_WIDTH), rows(LRU_WIDTH), rows(LRU_WIDTH), state, state,
            rows(LRU_WIDTH), rows(2 * D_MODEL),
            pl.BlockSpec((2, TT // 2, BATCH * FOURIER_WIDTH), lambda c: (0, c, 0)),
            rows(D_MODEL),
            _mod_spec(0, 2), _mod_spec(0, 3), _mod_spec(0, 4), _mod_spec(0, 5), vec1, vec1,
            const((LRU_WIDTH, D_MODEL)), const((FOURIER_WIDTH, D_MODEL)), const((D_MODEL, D_MODEL)),
            const((D_MODEL, D_FF)), const((D_FF, D_MODEL)),
        ],
        out_specs=pl.BlockSpec(memory_space=pl.ANY),
        out_shape=jax.ShapeDtypeStruct((BATCH, SEQ, D_MODEL), F32),
        scratch_shapes=[pltpu.VMEM((2, TT, BATCH, D_MODEL), F32), pltpu.SemaphoreType.DMA((2, BATCH))],
        name="tail",
        compiler_params=pltpu.CompilerParams(
            dimension_semantics=("arbitrary",), vmem_limit_bytes=VMEM_LIMIT),
    )(ls, pf, pb, hf, hb, uy, ug, yf, xt, mod, mod, mod, mod, gm, gf, wl, wf, wo, w1, w2)


def kernel(x, c, ctx, c_ctx, w_mod, b_mod, g_mix, w_in, conv_w, conv_b, w_a, b_a, w_x, b_x,
           lam, w_lru_out, w_f_out, w_out, g_mlp, w1, w2, g_final):
    pos_np, chan_np, time_np, flip_np = _constants()
    pos = jnp.asarray(pos_np)
    chan, time, flip = (jnp.asarray(v).astype(BF16) for v in (chan_np, time_np, flip_np))

    assert BATCH == SUBLANES
    mod = _mod_call(c, c_ctx[None], w_mod[0], b_mod[0][None])

    g_mix2 = g_mix[0][None]
    conv = (conv_w[0], conv_b[0][None])
    lru = (b_a[0], b_x[0], lam[0])

    w_in_b, wg, e_ctx = _head_call(
        ctx, None, mod, g_mix2, w_in[0], *conv, (w_a[0], w_x[0]), *lru, lat=False)
    xt, uy, uf, ug, ls, pf, pb, e_lat = _head_call(
        x, pos, mod, g_mix2, w_in_b, *conv, wg, *lru, lat=True)
    yf, hf, hb, wl, wf, wo, w1b, w2b = _fourier_call(
        uf, chan, time, flip, e_ctx, e_lat, (w_lru_out[0], w_f_out[0], w_out[0], w1[0], w2[0]))
    return _tail_call(ls, pf, pb, hf, hb, uy, ug, yf, xt, mod,
                      g_mlp[0][None], g_final[None], wl, wf, wo, w1b, w2b)
```

```python
import functools
import math

import numpy as np
import jax
import jax.numpy as jnp
from jax import lax
from jax.experimental import pallas as pl
from jax.experimental.pallas import tpu as pltpu

F32 = jnp.float32
BF16 = jnp.bfloat16

D_MODEL = 1024
BATCH = 8
SEQ = 2048
CTX_LEN = 256
GRID_W = 64
LRU_WIDTH = 1024
LRU_HEADS = 8
LRU_HEAD_DIM = LRU_WIDTH // LRU_HEADS
LRU_C = 8.0
CONV_WIDTH = 4
CONV_PAD_LEFT = 2
FOURIER_WIDTH = 512
FOURIER_GROUPS = 4
FOURIER_GROUP_DIM = FOURIER_WIDTH // FOURIER_GROUPS
IN_COLS = 2 * LRU_WIDTH + FOURIER_WIDTH + 2 * D_MODEL
D_FF = 4 * D_MODEL
N_MOD = 6
EPS = 1e-6
POS_MAX_PERIOD = 10000.0

SUBLANES = 8
BF16_ROWS = 16
assert BATCH == SUBLANES

TT = 64
ROWS = TT * BATCH
HALO_T = SUBLANES
LEAD = CONV_PAD_LEFT * BATCH
TRAIL = (CONV_WIDTH - 1 - CONV_PAD_LEFT) * BATCH
XROWS = LEAD + ROWS + TRAIL
XROWS_PAD = -(-XROWS // BF16_ROWS) * BF16_ROWS
N_CTX_CHUNKS = CTX_LEN // TT
N_LAT_CHUNKS = SEQ // TT
NCHUNK = 512
FRONT_ROWS = 128
GATE_AHEAD = 1
GATE_SLOTS = GATE_AHEAD + 1
DFT_ROWS = -(-(SEQ // 4 + 1) // BF16_ROWS) * BF16_ROWS
TAIL_WEIGHT_ROWS = (LRU_WIDTH, FOURIER_WIDTH, D_MODEL, D_MODEL, D_FF)
MOD_TN = 1024
MOD_K_SPLIT = 4
V7X_VMEM_BYTES = 64 * 1024 * 1024
VMEM_RESERVE = 6 * 1024 * 1024
VMEM_LIMIT = V7X_VMEM_BYTES - VMEM_RESERVE
LOG2E = math.log2(math.e)
TINY = 1e-30


def _gelu_tanh(x):
    k0 = jnp.full((1, 1), math.sqrt(2.0 / math.pi), F32).astype(x.dtype)
    k1 = jnp.full((1, 1), 0.044715, F32).astype(x.dtype)
    return 0.5 * x * (1.0 + jnp.tanh(k0 * (x + k1 * (x * x * x))))


def _sigmoid(x):
    return 0.5 * jnp.tanh(0.5 * x) + 0.5


def _dot(a, b):
    return jnp.dot(a, b, preferred_element_type=F32)


def _rms(x, g):
    ms = jnp.mean(x * x, axis=-1, keepdims=True)
    return x * lax.rsqrt(ms + EPS) * g


@functools.cache
def _constants():
    half = D_MODEL // 4
    freqs = np.exp(-math.log(POS_MAX_PERIOD) * np.arange(half, dtype=np.float32) / half).astype(np.float32)

    def sincos(n):
        ang = np.arange(n, dtype=np.float32)[:, None] * freqs[None, :]
        return np.concatenate([np.sin(ang), np.cos(ang)], axis=-1).astype(np.float32)

    rows = SEQ // GRID_W
    er, ec = sincos(rows), sincos(GRID_W)
    pos = np.concatenate([
        np.broadcast_to(er[:, None, :], (rows, GRID_W, D_MODEL // 2)),
        np.broadcast_to(ec[None, :, :], (rows, GRID_W, D_MODEL // 2)),
    ], axis=-1).reshape(SEQ, D_MODEL).astype(np.float32)

    def dft(n):
        k = np.arange(n, dtype=np.int64)
        ang = 2.0 * np.pi * ((k[:, None] * k[None, :]) % n).astype(np.float64) / n
        return np.cos(ang) / math.sqrt(n), np.sin(ang) / math.sqrt(n)

    cc, sc = dft(FOURIER_GROUP_DIM)
    chan = np.concatenate([cc, sc], axis=1).astype(np.float32)
    ct, st = dft(SEQ)
    half, quarter = SEQ // 2, SEQ // 4
    time = np.zeros((2, 2, DFT_ROWS, half), np.float64)
    flip = np.zeros((2, quarter, DFT_ROWS), np.float32)
    for p in range(2):
        n_rows = quarter + 1 - p
        rows = 2 * np.arange(n_rows) + p
        time[p, 0, :n_rows] = ct[rows, :half]
        time[p, 1, :n_rows] = st[rows, :half]
        flip[p, np.arange(quarter), quarter - p - np.arange(quarter)] = 1.0
    return pos, chan, time.astype(np.float32), flip


def _mod_kernel(c_ref, cx_ref, *refs):
    w_refs, b_ref, o_ref = refs[:MOD_K_SPLIT], refs[MOD_K_SPLIT], refs[MOD_K_SPLIT + 1]
    c = jnp.concatenate([c_ref[...], jnp.broadcast_to(cx_ref[...], c_ref.shape)], axis=0)
    a = c * _sigmoid(c)
    a_hi = a.astype(BF16)
    a_lo = (a - a_hi.astype(F32)).astype(BF16)
    n = a.shape[0]
    rows = D_MODEL // MOD_K_SPLIT
    acc = b_ref[...]
    for k, w_ref in enumerate(w_refs):
        ks = slice(k * rows, (k + 1) * rows)
        w = w_ref[...]
        w_hi = w.astype(BF16)
        w_lo = (w - w_hi.astype(F32)).astype(BF16)
        both = _dot(jnp.concatenate([a_hi[:, ks], a_lo[:, ks]], axis=0), w_hi)
        acc = acc + (both[:n] + _dot(a_hi[:, ks], w_lo) + both[n:])
    o_ref[...] = acc


def _mod_call(c, c_ctx, w_mod, b_mod):
    n = w_mod.shape[1]
    return pl.pallas_call(
        _mod_kernel,
        grid=(n // MOD_TN,),
        in_specs=[
            pl.BlockSpec((SUBLANES, D_MODEL), lambda i: (0, 0)),
            pl.BlockSpec((1, D_MODEL), lambda i: (0, 0)),
        ] + [
            pl.BlockSpec((D_MODEL // MOD_K_SPLIT, MOD_TN), functools.partial(lambda k, i: (k, i), k))
            for k in range(MOD_K_SPLIT)
        ] + [
            pl.BlockSpec((1, MOD_TN), lambda i: (0, i)),
        ],
        out_specs=pl.BlockSpec((2 * SUBLANES, MOD_TN), lambda i: (0, i)),
        out_shape=jax.ShapeDtypeStruct((2 * SUBLANES, n), F32),
        name="mod",
        compiler_params=pltpu.CompilerParams(dimension_semantics=("arbitrary",)),
    )(c, c_ctx, *([w_mod] * MOD_K_SPLIT), b_mod)


def _mod_spec(row_block, k):
    return pl.BlockSpec((BATCH, D_MODEL), lambda c: (row_block, k))


def _head_kernel(*refs, lat, n_chunks):
    refs = list(refs)
    xm_ref, xp_ref, xn_ref = refs[:3]
    refs = refs[3:]
    if lat:
        pm_ref, pp_ref, pn_ref = refs[:3]
        refs = refs[3:]
    (sh_ref, sc_ref, g_ref, w_ref, cw_ref, cb_ref, wg_ref, ba_ref, bx_ref, lam_ref) = refs[:10]
    refs = refs[10:]
    if lat:
        xt_out, uy_out, uf_out, ug_out, ls_out, pf_out, pb_out = refs[:7]
        refs = refs[7:]
    else:
        w_slab_ref, wx_ref, w_bf16_out, wg_out = refs[:4]
        refs = refs[4:]
        w_bf16_out[...] = w_slab_ref[...].astype(BF16)
    e_out, xs_ref, hb_ref, a_ref, b_ref, xh_ref, gate_ref = refs

    c = pl.program_id(0)
    if not lat:
        wa_ref = wg_ref

        @pl.when(c == 0)
        def _():
            wg_out[:, :, :, :LRU_HEAD_DIM] = wa_ref[...].astype(BF16)
            wg_out[:, :, :, LRU_HEAD_DIM:] = wx_ref[...].astype(BF16)

        wg_ref = wg_out
    mod_rows = slice(None) if lat else slice(0, 1)

    xm = xm_ref[...]
    xp = xp_ref[...]
    xn = xn_ref[...]
    if lat:
        xm = xm + pm_ref[...][None]
        xp = xp + pp_ref[...][None]
        xn = xn + pn_ref[...][None]
    xs_ref[LEAD:LEAD + ROWS, :] = jnp.swapaxes(xm, 0, 1).reshape(ROWS, D_MODEL)
    xs_ref[0:LEAD, :] = jnp.swapaxes(xp, 0, 1)[HALO_T - CONV_PAD_LEFT:].reshape(LEAD, D_MODEL)
    xs_ref[LEAD + ROWS:XROWS, :] = (
        jnp.swapaxes(xn, 0, 1)[:CONV_WIDTH - 1 - CONV_PAD_LEFT].reshape(TRAIL, D_MODEL))
    xs_ref[XROWS:XROWS_PAD, :] = jnp.zeros((XROWS_PAD - XROWS, D_MODEL), F32)

    assert LEAD % BF16_ROWS == 0 and FRONT_ROWS % BF16_ROWS == 0
    blocks = ([(0, LEAD)] + [(r, r + FRONT_ROWS) for r in range(LEAD, LEAD + ROWS, FRONT_ROWS)]
              + [(LEAD + ROWS, XROWS_PAD)])
    for r0, r1 in blocks:
        xt = xs_ref[r0:r1, :]
        if lat and LEAD <= r0 < LEAD + ROWS:
            xt_out[r0 - LEAD:r1 - LEAD, :] = xt
        h = _rms(xt, g_ref[...]).reshape((r1 - r0) // BATCH, BATCH, D_MODEL)
        h = h * (1.0 + sc_ref[mod_rows, :][None]) + sh_ref[mod_rows, :][None]
        hb_ref[r0:r1, :] = h.reshape(r1 - r0, D_MODEL).astype(BF16)

    keep_first = jnp.where(c == 0, 0.0, 1.0)
    keep_last = jnp.where(c == n_chunks - 1, 0.0, 1.0)
    for n0 in range(0, LRU_WIDTH, NCHUNK):
        w_x = w_ref[:, n0:n0 + NCHUNK]
        xs_ref[:, n0:n0 + NCHUNK] = _dot(hb_ref[...], w_x if lat else w_x.astype(BF16))
    xs_ref[0:LEAD, :] = xs_ref[0:LEAD, :] * keep_first
    xs_ref[LEAD + ROWS:XROWS, :] = xs_ref[LEAD + ROWS:XROWS, :] * keep_last

    def branch_piece(n0):
        r = _dot(hb_ref[LEAD:LEAD + ROWS, :], w_ref[:, n0:n0 + NCHUNK])
        if n0 < 2 * LRU_WIDTH:
            uy_out[:, n0 - LRU_WIDTH:n0 - LRU_WIDTH + NCHUNK] = r.astype(BF16)
        elif n0 < 2 * LRU_WIDTH + FOURIER_WIDTH:
            assert NCHUNK == FOURIER_WIDTH
            uf_out[...] = r.reshape(TT, BATCH * NCHUNK).astype(BF16)
        else:
            c0 = n0 - 2 * LRU_WIDTH - FOURIER_WIDTH
            ug_out[:, c0:c0 + NCHUNK] = r.astype(BF16)

    pieces = list(range(LRU_WIDTH, IN_COLS, NCHUNK)) if lat else []
    per_head = -(-len(pieces) // LRU_HEADS)

    lam = lam_ref[...]
    sp = jnp.maximum(-lam, 0.0) + jnp.log1p(jnp.exp(-jnp.abs(lam)))
    c1 = (-0.5 * LRU_C * LOG2E) * sp

    def conv_gates(hd):
        sl = slice(hd * LRU_HEAD_DIM, (hd + 1) * LRU_HEAD_DIM)
        xh = 0.5 * cb_ref[:, sl]
        for k in range(CONV_WIDTH):
            xh = xh + (0.5 * cw_ref[k:k + 1, sl]) * xs_ref[k * BATCH:k * BATCH + ROWS, sl]
        xh_ref[hd % GATE_SLOTS] = xh
        xhb = xh.astype(BF16)
        for d in range(2):
            gate_ref[hd % GATE_SLOTS, d] = _dot(xhb, wg_ref[d, hd])

    for hd in range(GATE_AHEAD):
        conv_gates(hd)
    def coefficients(hd):
        sl = slice(hd * LRU_HEAD_DIM, (hd + 1) * LRU_HEAD_DIM)
        xh = xh_ref[hd % GATE_SLOTS]
        for d in range(2):
            g = gate_ref[hd % GATE_SLOTS, d]
            ta = jnp.tanh(g[:, :LRU_HEAD_DIM] + 0.5 * ba_ref[d:d + 1, sl])
            ti = jnp.tanh(g[:, LRU_HEAD_DIM:] + 0.5 * bx_ref[d:d + 1, sl])
            a = jnp.exp2(c1[d:d + 1, sl] * ta + c1[d:d + 1, sl])
            v = 1.0 - a * a
            m = v * lax.rsqrt(jnp.maximum(v, TINY))
            a_ref[d, :, sl] = a
            b_ref[d, :, sl] = (m * xh) * (ti + 1.0)

    def scans(hd):
        sl = slice(hd * LRU_HEAD_DIM, (hd + 1) * LRU_HEAD_DIM)
        lf = pf = lb = pb = None
        for t in range(TT):
            rf = slice(t * BATCH, (t + 1) * BATCH)
            rb = slice((TT - 1 - t) * BATCH, (TT - t) * BATCH)
            af, bf = a_ref[0, rf, sl], b_ref[0, rf, sl]
            ab, bb = a_ref[1, rb, sl], b_ref[1, rb, sl]
            if t == 0:
                lf, pf, lb, pb = bf, af, bb, ab
            else:
                lf, pf = af * lf + bf, af * pf
                lb, pb = ab * lb + bb, ab * pb
                if lat:
                    b_ref[0, rf, sl] = lf
                    a_ref[0, rf, sl] = pf
                    b_ref[1, rb, sl] = lb
                    a_ref[1, rb, sl] = pb
        e_out[0, :, sl] = lf
        e_out[1, :, sl] = pf
        e_out[2, :, sl] = lb
        e_out[3, :, sl] = pb
        if lat:
            ls_out[:, sl] = (b_ref[0, :, sl] + b_ref[1, :, sl]).astype(BF16)
            pf_out[:, sl] = a_ref[0, :, sl].astype(BF16)
            pb_out[:, sl] = a_ref[1, :, sl].astype(BF16)

    for hd in range(LRU_HEADS):
        for n0 in pieces[hd * per_head:(hd + 1) * per_head]:
            branch_piece(n0)
        coefficients(hd)
        if hd + GATE_AHEAD < LRU_HEADS:
            conv_gates(hd + GATE_AHEAD)
        scans(hd)


def _head_call(x, pos, mod, g, w, conv_w, conv_b, wg, b_a, b_x, lam, *, lat):
    seq = x.shape[1]
    n_chunks = seq // TT
    n_halo = seq // HALO_T
    per_chunk = TT // HALO_T

    def const(shape):
        return pl.BlockSpec(shape, lambda c: (0,) * len(shape))

    def prev_idx(c):
        return jnp.maximum(c * per_chunk - 1, 0)

    def next_idx(c):
        return jnp.minimum((c + 1) * per_chunk, n_halo - 1)

    in_specs = [
        pl.BlockSpec((BATCH, TT, D_MODEL), lambda c: (0, c, 0)),
        pl.BlockSpec((BATCH, HALO_T, D_MODEL), lambda c: (0, prev_idx(c), 0)),
        pl.BlockSpec((BATCH, HALO_T, D_MODEL), lambda c: (0, next_idx(c), 0)),
    ]
    args = [x, x, x]
    if lat:
        in_specs += [
            pl.BlockSpec((TT, D_MODEL), lambda c: (c, 0)),
            pl.BlockSpec((HALO_T, D_MODEL), lambda c: (prev_idx(c), 0)),
            pl.BlockSpec((HALO_T, D_MODEL), lambda c: (next_idx(c), 0)),
        ]
        args += [pos, pos, pos]
    in_specs += [
        _mod_spec(0 if lat else 1, 0), _mod_spec(0 if lat else 1, 1), const((1, D_MODEL)),
        const((D_MODEL, IN_COLS if lat else LRU_WIDTH)),
        const((CONV_WIDTH, LRU_WIDTH)), const((1, LRU_WIDTH)),
        const((2, LRU_HEADS, LRU_HEAD_DIM, (2 if lat else 1) * LRU_HEAD_DIM)),
        const((2, LRU_WIDTH)), const((2, LRU_WIDTH)), const((2, LRU_WIDTH)),
    ]
    args += [mod, mod, g, w, conv_w, conv_b, wg if lat else wg[0], b_a, b_x, lam]
    w_slab = pl.BlockSpec((D_MODEL // n_chunks, IN_COLS), lambda c: (c, 0))
    wg_spec = const((2, LRU_HEADS, LRU_HEAD_DIM, 2 * LRU_HEAD_DIM))
    if not lat:
        in_specs += [w_slab, const((2, LRU_HEADS, LRU_HEAD_DIM, LRU_HEAD_DIM))]
        args += [w, wg[1]]

    def rows(width):
        return pl.BlockSpec((ROWS, width), lambda c: (c, 0))

    out_specs, out_shape = [], []
    if lat:
        n = n_chunks * ROWS
        out_specs += [rows(D_MODEL), rows(LRU_WIDTH),
                      pl.BlockSpec((TT, BATCH * FOURIER_WIDTH), lambda c: (c, 0)),
                      rows(2 * D_MODEL), rows(LRU_WIDTH), rows(LRU_WIDTH), rows(LRU_WIDTH)]
        out_shape += [
            jax.ShapeDtypeStruct((n, D_MODEL), F32),
            jax.ShapeDtypeStruct((n, LRU_WIDTH), BF16),
            jax.ShapeDtypeStruct((seq, BATCH * FOURIER_WIDTH), BF16),
            jax.ShapeDtypeStruct((n, 2 * D_MODEL), BF16),
            jax.ShapeDtypeStruct((n, LRU_WIDTH), BF16),
            jax.ShapeDtypeStruct((n, LRU_WIDTH), BF16),
            jax.ShapeDtypeStruct((n, LRU_WIDTH), BF16),
        ]
    else:
        out_specs += [w_slab, wg_spec]
        out_shape += [jax.ShapeDtypeStruct((D_MODEL, IN_COLS), BF16),
                      jax.ShapeDtypeStruct(wg_spec.block_shape, BF16)]
    out_specs.append(pl.BlockSpec((None, 4, BATCH, LRU_WIDTH), lambda c: (c, 0, 0, 0)))
    out_shape.append(jax.ShapeDtypeStruct((n_chunks, 4, BATCH, LRU_WIDTH), F32))

    return pl.pallas_call(
        functools.partial(_head_kernel, lat=lat, n_chunks=n_chunks),
        grid=(n_chunks,),
        in_specs=in_specs,
        out_specs=out_specs,
        out_shape=out_shape,
        scratch_shapes=[
            pltpu.VMEM((XROWS_PAD, D_MODEL), F32),
            pltpu.VMEM((XROWS_PAD, D_MODEL), BF16),
            pltpu.VMEM((2, ROWS, LRU_WIDTH), F32),
            pltpu.VMEM((2, ROWS, LRU_WIDTH), F32),
            pltpu.VMEM((GATE_SLOTS, ROWS, LRU_HEAD_DIM), F32),
            pltpu.VMEM((GATE_SLOTS, 2, ROWS, 2 * LRU_HEAD_DIM), F32),
        ],
        name="head_lat" if lat else "head_ctx",
        compiler_params=pltpu.CompilerParams(
            dimension_semantics=("arbitrary",), vmem_limit_bytes=VMEM_LIMIT),
    )(*args)


def _carry(ec_ref, el_ref, hf_ref, hb_ref):
    n_ctx, n_lat = ec_ref.shape[0], el_ref.shape[0]
    h = jnp.zeros(hf_ref.shape[1:], F32)
    for c in range(n_ctx):
        h = ec_ref[c, 1] * h + ec_ref[c, 0]
    for c in range(n_lat):
        hf_ref[c] = h
        h = el_ref[c, 1] * h + el_ref[c, 0]
    h = jnp.zeros(hb_ref.shape[1:], F32)
    for c in reversed(range(n_ctx)):
        h = ec_ref[c, 3] * h + ec_ref[c, 2]
    for c in reversed(range(n_lat)):
        hb_ref[c] = h
        h = el_ref[c, 3] * h + el_ref[c, 2]


def _fourier_kernel(x_ref, chan_ref, time_ref, flip_ref, ec_ref, el_ref, *refs):
    n_w = len(TAIL_WEIGHT_ROWS)
    w_f32_refs, (o_ref, hf_ref, hb_ref) = refs[:n_w], refs[n_w:n_w + 3]
    w_bf16_refs, z_ref = refs[n_w + 3:2 * n_w + 3], refs[-1]
    for src, dst in zip(w_f32_refs, w_bf16_refs):
        dst[...] = src[...].astype(BF16)
    _carry(ec_ref, el_ref, hf_ref, hb_ref)

    half, quarter = SEQ // 2, SEQ // 4
    for g in range(FOURIER_GROUPS):
        sl = slice(g * FOURIER_GROUP_DIM, (g + 1) * FOURIER_GROUP_DIM)
        r = _dot(x_ref[:, sl], chan_ref[...])
        lo, hi = r[:half], r[half:]
        for p, v in enumerate((lo + hi, lo - hi)):
            z_ref[p, 0:half, sl] = v[:, :FOURIER_GROUP_DIM].astype(BF16)
            z_ref[p, half:SEQ, sl] = v[:, FOURIER_GROUP_DIM:].astype(BF16)
    for p in range(2):
        cos_part = _dot(time_ref[p, 0], z_ref[p, 0:half, :])
        sin_part = _dot(time_ref[p, 1], z_ref[p, half:SEQ, :])
        o_ref[p, 0:quarter, :] = (cos_part - sin_part)[0:quarter].astype(BF16)
        mirrored = (cos_part + sin_part).astype(BF16)
        o_ref[p, quarter:half, :] = _dot(flip_ref[p], mirrored).astype(BF16)


def _fourier_call(uf, chan, time, flip, e_ctx, e_lat, weights):
    def slab(w):
        return pl.BlockSpec((w.shape[0] // BATCH, w.shape[1]), lambda n: (n, 0))

    def ends(e):
        return pl.BlockSpec((e.shape[0], 4, BATCH, LRU_WIDTH // BATCH), lambda n: (0, 0, 0, n))

    assert tuple(w.shape[0] for w in weights) == TAIL_WEIGHT_ROWS
    n_lat = e_lat.shape[0]
    state = pl.BlockSpec((n_lat, BATCH, LRU_WIDTH // BATCH), lambda n: (0, 0, n))
    state_shape = jax.ShapeDtypeStruct((n_lat, BATCH, LRU_WIDTH), F32)
    return pl.pallas_call(
        _fourier_kernel,
        grid=(BATCH,),
        in_specs=[
            pl.BlockSpec((SEQ, FOURIER_WIDTH), lambda n: (0, n)),
            pl.BlockSpec((FOURIER_GROUP_DIM, 2 * FOURIER_GROUP_DIM), lambda n: (0, 0)),
            pl.BlockSpec((2, 2, DFT_ROWS, SEQ // 2), lambda n: (0, 0, 0, 0)),
            pl.BlockSpec((2, SEQ // 4, DFT_ROWS), lambda n: (0, 0, 0)),
            ends(e_ctx), ends(e_lat),
        ] + [slab(w) for w in weights],
        out_specs=[pl.BlockSpec((2, SEQ // 2, FOURIER_WIDTH), lambda n: (0, 0, n)), state, state]
        + [slab(w) for w in weights],
        out_shape=[jax.ShapeDtypeStruct((2, SEQ // 2, BATCH * FOURIER_WIDTH), BF16),
                   state_shape, state_shape]
        + [jax.ShapeDtypeStruct(w.shape, BF16) for w in weights],
        scratch_shapes=[pltpu.VMEM((2, SEQ, FOURIER_WIDTH), BF16)],
        name="fourier",
        compiler_params=pltpu.CompilerParams(
            dimension_semantics=("arbitrary",), vmem_limit_bytes=VMEM_LIMIT),
    )(uf, chan, time, flip, e_ctx, e_lat, *weights)


def _tail_kernel(ls_ref, pf_ref, pb_ref, hf_ref, hb_ref, uy_ref, ug_ref, yf_ref, xt_ref,
                 gt1_ref, sh2_ref, sc2_ref, gt2_ref, gm_ref, gf_ref,
                 wl_ref, wf_ref, wo_ref, w1_ref, w2_ref, o_hbm, obuf_ref, osem):
    def tb(v):
        return v.reshape(TT, BATCH, v.shape[-1])

    def flat(v):
        return v.reshape(ROWS, v.shape[-1])

    yf = yf_ref[...].astype(F32).reshape(2, TT // 2, BATCH, FOURIER_WIDTH)
    yf = jnp.swapaxes(yf, 0, 1).reshape(ROWS, FOURIER_WIDTH).astype(BF16)
    y_b = _dot(yf, wf_ref[...])

    hf = jnp.tile(hf_ref[...].astype(BF16), (TT, 1))
    hb = jnp.tile(hb_ref[...].astype(BF16), (TT, 1))
    y_lru = ls_ref[...] + pf_ref[...] * hf + pb_ref[...] * hb
    z = y_lru * _gelu_tanh(uy_ref[...])
    y_a = _dot(z, wl_ref[...])

    g_a = _sigmoid(ug_ref[:, :D_MODEL].astype(F32))
    g_b = _sigmoid(ug_ref[:, D_MODEL:].astype(F32))
    merged = (g_a * y_a + g_b * y_b).astype(BF16)
    x2 = tb(xt_ref[...]) + gt1_ref[...][None] * tb(_dot(merged, wo_ref[...]))

    h = (tb(_rms(flat(x2), gm_ref[...])) * (1.0 + sc2_ref[...][None]) + sh2_ref[...][None])
    h = flat(h).astype(BF16)
    acc = jnp.zeros((ROWS, D_MODEL), F32)
    for k0 in range(0, D_FF, D_MODEL):
        a = jnp.maximum(_dot(h, w1_ref[:, k0:k0 + D_MODEL]), 0.0)
        acc = acc + _dot((a * a).astype(BF16), w2_ref[k0:k0 + D_MODEL, :])
    out = _rms(flat(x2 + gt2_ref[...][None] * tb(acc)), gf_ref[...])

    c = pl.program_id(0)
    slot = c % 2

    def copies(step, slot_):
        return [pltpu.make_async_copy(obuf_ref.at[slot_, :, b, :],
                                      o_hbm.at[b, pl.ds(step * TT, TT), :], osem.at[slot_, b])
                for b in range(BATCH)]

    @pl.when(c >= 2)
    def _():
        for cp in copies(c - 2, slot):
            cp.wait()

    obuf_ref[slot] = tb(out)
    for cp in copies(c, slot):
        cp.start()

    @pl.when(c == N_LAT_CHUNKS - 1)
    def _():
        for cp in copies(c - 1, 1 - slot) + copies(c, slot):
            cp.wait()


def _tail_call(ls, pf, pb, hf, hb, uy, ug, yf, xt, mod, gm, gf, wl, wf, wo, w1, w2):
    def rows(width):
        return pl.BlockSpec((ROWS, width), lambda c: (c, 0))

    def const(shape):
        return pl.BlockSpec(shape, lambda c: (0,) * len(shape))

    state = pl.BlockSpec((None, BATCH, LRU_WIDTH), lambda c: (c, 0, 0))
    vec1 = const((1, D_MODEL))
    return pl.pallas_call(
        _tail_kernel,
        grid=(N_LAT_CHUNKS,),
        in_specs=[
            rows(LRU_WIDTH), rows(LRU_WIDTH), rows(LRU_WIDTH), state, state,
            rows(LRU_WIDTH), rows(2 * D_MODEL),
            pl.BlockSpec((2, TT // 2, BATCH * FOURIER_WIDTH), lambda c: (0, c, 0)),
            rows(D_MODEL),
            _mod_spec(0, 2), _mod_spec(0, 3), _mod_spec(0, 4), _mod_spec(0, 5), vec1, vec1,
            const((LRU_WIDTH, D_MODEL)), const((FOURIER_WIDTH, D_MODEL)), const((D_MODEL, D_MODEL)),
            const((D_MODEL, D_FF)), const((D_FF, D_MODEL)),
        ],
        out_specs=pl.BlockSpec(memory_space=pl.ANY),
        out_shape=jax.ShapeDtypeStruct((BATCH, SEQ, D_MODEL), F32),
        scratch_shapes=[pltpu.VMEM((2, TT, BATCH, D_MODEL), F32), pltpu.SemaphoreType.DMA((2, BATCH))],
        name="tail",
        compiler_params=pltpu.CompilerParams(
            dimension_semantics=("arbitrary",), vmem_limit_bytes=VMEM_LIMIT),
    )(ls, pf, pb, hf, hb, uy, ug, yf, xt, mod, mod, mod, mod, gm, gf, wl, wf, wo, w1, w2)


def kernel(x, c, ctx, c_ctx, w_mod, b_mod, g_mix, w_in, conv_w, conv_b, w_a, b_a, w_x, b_x,
           lam, w_lru_out, w_f_out, w_out, g_mlp, w1, w2, g_final):
    pos_np, chan_np, time_np, flip_np = _constants()
    pos = jnp.asarray(pos_np)
    chan, time, flip = (jnp.asarray(v).astype(BF16) for v in (chan_np, time_np, flip_np))

    assert BATCH == SUBLANES
    mod = _mod_call(c, c_ctx[None], w_mod[0], b_mod[0][None])

    g_mix2 = g_mix[0][None]
    conv = (conv_w[0], conv_b[0][None])
    lru = (b_a[0], b_x[0], lam[0])

    w_in_b, wg, e_ctx = _head_call(
        ctx, None, mod, g_mix2, w_in[0], *conv, (w_a[0], w_x[0]), *lru, lat=False)
    xt, uy, uf, ug, ls, pf, pb, e_lat = _head_call(
        x, pos, mod, g_mix2, w_in_b, *conv, wg, *lru, lat=True)
    yf, hf, hb, wl, wf, wo, w1b, w2b = _fourier_call(
        uf, chan, time, flip, e_ctx, e_lat, (w_lru_out[0], w_f_out[0], w_out[0], w1[0], w2[0]))
    return _tail_call(ls, pf, pb, hf, hb, uy, ug, yf, xt, mod,
                      g_mlp[0][None], g_final[None], wl, wf, wo, w1b, w2b)
```
